```python
import math
import jax
import jax.numpy as jnp
from jax import lax
import numpy as np

D_MODEL = 2048
BATCH = 4
SEQ = 2048
DEPTH = 2

GRID_W = 64
CTX_LEN = 256
ROPE_BASE = 10000.0
NORM_EPS = 1e-6
Q_BLOCK = 128
CHUNK = 64

DIFF_HEADS = 8
DIFF_QK = 32
DIFF_V = 64
NA_HEADS = 8
NA_DIM = 64
NA_WIN_R = 8
NA_WIN_C = 16
HG_HEADS = 4
HG_DIM = 128
GLA_HEADS = 4
GLA_DK = 64
GLA_DV = 128
GLA_GATE_RANK = 16
GLA_GATE_TAU = 16.0

MIX_WIDTH = DIFF_HEADS * DIFF_V + NA_HEADS * NA_DIM + HG_HEADS * HG_DIM + GLA_HEADS * GLA_DV

IN_SPLITS = (
    DIFF_HEADS * 2 * DIFF_QK, DIFF_HEADS * 2 * DIFF_QK, DIFF_HEADS * DIFF_V,
    NA_HEADS * NA_DIM, NA_HEADS * NA_DIM, NA_HEADS * NA_DIM,
    HG_HEADS * HG_DIM, HG_HEADS * HG_DIM, HG_HEADS * HG_DIM, HG_HEADS * HG_DIM, HG_HEADS * HG_DIM,
    GLA_HEADS * GLA_DK, GLA_HEADS * GLA_DK, GLA_HEADS * GLA_DV, 2 * GLA_GATE_RANK, GLA_HEADS * GLA_DV,
)
IN_WIDTH = sum(IN_SPLITS)

FFN_DENSE = 5632
N_EXPERTS = 8
TOP_K = 2
FFN_EXPERT = 7168
N_DENSE_LAYERS = (DEPTH + 1) // 2
N_MOE_LAYERS = DEPTH // 2

kernel_name = 'hybrid_diffusion_parallel_heads'


def rmsnorm(x, g):
    xf = x.astype(jnp.float32)
    y = xf * lax.rsqrt(jnp.mean(xf * xf, axis=-1, keepdims=True) + NORM_EPS)
    return (y * g.astype(jnp.float32)).astype(x.dtype)


def split_cols(p):
    return jnp.split(p, np.cumsum(IN_SPLITS)[:-1].tolist(), axis=-1)


def axial_rope(n_tok, dim):
    pairs = dim // 2
    per_axis = pairs // 2
    t = jnp.arange(n_tok)
    row = (t // GRID_W).astype(jnp.float32)
    col = (t % GRID_W).astype(jnp.float32)
    inv = ROPE_BASE ** (-jnp.arange(per_axis, dtype=jnp.float32) / per_axis)
    ang = jnp.concatenate([row[:, None] * inv, col[:, None] * inv], axis=-1)
    return jnp.cos(ang), jnp.sin(ang)


def apply_rope(x, cos, sin):
    shape = (1, x.shape[1]) + (1,) * (x.ndim - 3) + (cos.shape[-1],)
    cs = cos.reshape(shape).astype(x.dtype)
    sn = sin.reshape(shape).astype(x.dtype)
    x1, x2 = jnp.split(x, 2, axis=-1)
    return jnp.concatenate([x1 * cs - x2 * sn, x1 * sn + x2 * cs], axis=-1)


def dense_attention(q, k, v, scale):
    s = jnp.einsum('bqhd,bkhd->bhqk', q, k).astype(jnp.float32) * scale
    p = jax.nn.softmax(s, axis=-1).astype(v.dtype)
    return jnp.einsum('bhqk,bkhd->bqhd', p, v)


def diff_core(q, k, v, lam):
    s = jnp.einsum('bqhmd,bkhmd->bhmqk', q, k).astype(jnp.float32) * (DIFF_QK ** -0.5)
    p = jax.nn.softmax(s, axis=-1)
    w = (p[:, :, 0] - lam * p[:, :, 1]).astype(v.dtype)
    return jnp.einsum('bhqk,bkhd->bqhd', w, v)


def diff_attention_mixer(q_l, k_l, v_l, q_c, k_c, v_c, lam_params, norm_g, lambda_init, ctx_out):
    B, S, _ = q_l.shape
    n_ctx = k_c.shape[1]
    qk_shape = lambda n: (B, n, DIFF_HEADS, 2, DIFF_QK)
    cos, sin = axial_rope(S, DIFF_QK)
    q_l = apply_rope(q_l.reshape(qk_shape(S)), cos, sin)
    k_l = apply_rope(k_l.reshape(qk_shape(S)), cos, sin)
    v_l = v_l.reshape(B, S, DIFF_HEADS, DIFF_V)
    k_c = k_c.reshape(qk_shape(n_ctx))
    v_c = v_c.reshape(B, n_ctx, DIFF_HEADS, DIFF_V)
    lp = lam_params.astype(jnp.float32)
    lam = jnp.exp(jnp.sum(lp[0] * lp[1])) - jnp.exp(jnp.sum(lp[2] * lp[3])) + lambda_init
    k_all = jnp.concatenate([k_c, k_l], axis=1)
    v_all = jnp.concatenate([v_c, v_l], axis=1)
    n_blk = S // Q_BLOCK
    q_blocks = q_l.reshape(B, n_blk, Q_BLOCK, DIFF_HEADS, 2, DIFF_QK).transpose(1, 0, 2, 3, 4, 5)
    o_l = lax.map(lambda qb: diff_core(qb, k_all, v_all, lam), q_blocks)
    o_l = o_l.transpose(1, 0, 2, 3, 4).reshape(B, S, DIFF_HEADS, DIFF_V)

    def post(o):
        return (rmsnorm(o, norm_g) * (1.0 - lambda_init)).reshape(o.shape[0], o.shape[1], -1)

    o_c = post(diff_core(q_c.reshape(qk_shape(n_ctx)), k_c, v_c, lam)) if ctx_out else None
    return o_c, post(o_l)


def neighbourhood_attention_mixer(q_l, k_l, v_l, q_c, k_c, v_c, rel_bias, ctx_out):
    B, S, _ = q_l.shape
    n_ctx = k_c.shape[1]
    rows = S // GRID_W
    win_r = min(NA_WIN_R, rows)
    scale = NA_DIM ** -0.5
    r = jnp.arange(rows)
    row_start = jnp.clip(r - win_r // 2, 0, rows - win_r)
    row_idx = row_start[:, None] + jnp.arange(win_r)[None, :]
    cc = jnp.arange(GRID_W)
    col_start = jnp.clip(cc - NA_WIN_C // 2, 0, GRID_W - NA_WIN_C)
    col_ok = (cc[None, :] >= col_start[:, None]) & (cc[None, :] < col_start[:, None] + NA_WIN_C)
    dr = row_idx - r[:, None] + (NA_WIN_R - 1)
    dc = jnp.clip(cc[None, :] - cc[:, None], -(NA_WIN_C - 1), NA_WIN_C - 1) + (NA_WIN_C - 1)
    bias = rel_bias.astype(jnp.float32)[:, dr[:, None, :, None], dc[None, :, None, :]]

    q_g = q_l.reshape(B, rows, GRID_W, NA_HEADS, NA_DIM)
    k_g = k_l.reshape(B, rows, GRID_W, NA_HEADS, NA_DIM)[:, row_idx]
    v_g = v_l.reshape(B, rows, GRID_W, NA_HEADS, NA_DIM)[:, row_idx]
    k_c = k_c.reshape(B, n_ctx, NA_HEADS, NA_DIM)
    v_c = v_c.reshape(B, n_ctx, NA_HEADS, NA_DIM)

    s_lat = jnp.einsum('brchd,brkwhd->bhrckw', q_g, k_g).astype(jnp.float32) * scale + bias[None]
    s_lat = jnp.where(col_ok[:, None, :], s_lat, -jnp.inf)
    s_ctx = jnp.einsum('brchd,bjhd->bhrcj', q_g, k_c).astype(jnp.float32) * scale
    n_lat = win_r * GRID_W
    s = jnp.concatenate([s_lat.reshape(B, NA_HEADS, rows, GRID_W, n_lat), s_ctx], axis=-1)
    p = jax.nn.softmax(s, axis=-1).astype(v_l.dtype)
    p_lat = p[..., :n_lat].reshape(B, NA_HEADS, rows, GRID_W, win_r, GRID_W)
    p_ctx = p[..., n_lat:]
    o_l = (jnp.einsum('bhrckw,brkwhd->brchd', p_lat, v_g)
           + jnp.einsum('bhrcj,bjhd->brchd', p_ctx, v_c)).reshape(B, S, NA_HEADS * NA_DIM)
    o_c = None
    if ctx_out:
        o_c = dense_attention(q_c.reshape(B, n_ctx, NA_HEADS, NA_DIM), k_c, v_c, scale).reshape(B, n_ctx, -1)
    return o_c, o_l


def chunk_gla(q, k, v, log_g, state0):
    B, T, H, _ = q.shape
    Dv = v.shape[-1]
    n = T // CHUNK

    def blocks(a):
        return a.astype(jnp.float32).reshape(B, n, CHUNK, H, a.shape[-1]).transpose(1, 0, 3, 2, 4)

    causal = jnp.tril(jnp.ones((CHUNK, CHUNK), dtype=bool))[:, :, None]

    def step(S, xs):
        qi, ki, vi, gi = xs
        b = jnp.cumsum(gi, axis=2)
        rel = jnp.exp(jnp.where(causal, b[:, :, :, None, :] - b[:, :, None, :, :], -jnp.inf))
        att = jnp.einsum('bhtd,bhsd,bhtsd->bhts', qi, ki, rel)
        o = att @ vi + jnp.einsum('bhtd,bhdv->bhtv', qi * jnp.exp(b), S)
        b_end = b[:, :, -1:, :]
        S = jnp.exp(b_end[:, :, 0, :])[..., None] * S + jnp.einsum('bhsd,bhsv->bhdv', ki * jnp.exp(b_end - b), vi)
        return S, o

    S_end, o = lax.scan(step, state0, (blocks(q), blocks(k), blocks(v), blocks(log_g)))
    return o.transpose(1, 0, 3, 2, 4).reshape(B, T, H, Dv).astype(v.dtype), S_end


def gla_final_state(k, v, log_g):
    b = jnp.cumsum(log_g.astype(jnp.float32), axis=1)
    w = k.astype(jnp.float32) * jnp.exp(b[:, -1:] - b)
    return jnp.einsum('bthd,bthv->bhdv', w, v.astype(jnp.float32))


def bidirectional_gla(inp_c, inp_l, scale, ctx_out):
    q_c, v_c, dirs_c = inp_c
    q_l, v_l, dirs_l = inp_l
    B, _, H, Dk = q_l.shape
    Dv = v_l.shape[-1]
    zero = jnp.zeros((B, H, Dk, Dv), jnp.float32)
    q_c = q_c * scale
    q_l = q_l * scale
    o_c = None
    o_l = None
    for d, rev in ((0, False), (1, True)):
        fl = (lambda a: jnp.flip(a, axis=1)) if rev else (lambda a: a)
        k_c, g_c = dirs_c[d]
        k_l, g_l = dirs_l[d]
        if ctx_out:
            oc, s_ctx = chunk_gla(fl(q_c), fl(k_c), fl(v_c), fl(g_c), zero)
            o_c = fl(oc) if o_c is None else o_c + fl(oc)
        else:
            s_ctx = gla_final_state(fl(k_c), fl(v_c), fl(g_c))
        ol, _ = chunk_gla(fl(q_l), fl(k_l), fl(v_l), fl(g_l), s_ctx)
        o_l = fl(ol) if o_l is None else o_l + fl(ol)
    return o_c, o_l


def hgrn2_mixer(parts_c, parts_l, lb_f, lb_b, norm_g, ctx_out):
    def prep(parts):
        q, i, f_f, f_b, g = parts
        B, T, _ = q.shape
        heads = lambda a: a.reshape(B, T, HG_HEADS, HG_DIM)
        dirs = []
        for f, lb in ((f_f, lb_f), (f_b, lb_b)):
            f = f.astype(jnp.float32)
            log_f = jnp.logaddexp(jnp.log(lb), jnp.log1p(-lb) + jax.nn.log_sigmoid(f))
            k = (1.0 - lb) * jax.nn.sigmoid(-f)
            dirs.append((heads(k), heads(log_f)))
        return (heads(jax.nn.silu(q)), heads(i), dirs), g

    in_c, g_c = prep(parts_c)
    in_l, g_l = prep(parts_l)
    o_c, o_l = bidirectional_gla(in_c, in_l, 1.0, ctx_out)

    def post(o, g):
        return rmsnorm(o, norm_g).reshape(o.shape[0], o.shape[1], -1) * jax.nn.silu(g)

    return (post(o_c, g_c) if ctx_out else None), post(o_l, g_l)


def gla_mixer(parts_c, parts_l, w2, b2, norm_g, ctx_out):
    def prep(parts):
        q, k, v, z, r = parts
        B, T, _ = q.shape
        k = k.reshape(B, T, GLA_HEADS, GLA_DK)
        dirs = []
        for d, zd in enumerate(jnp.split(z, 2, axis=-1)):
            log_g = jax.nn.log_sigmoid((zd @ w2[d] + b2[d]).astype(jnp.float32)) / GLA_GATE_TAU
            dirs.append((k, log_g.reshape(B, T, GLA_HEADS, GLA_DK)))
        return (q.reshape(B, T, GLA_HEADS, GLA_DK), v.reshape(B, T, GLA_HEADS, GLA_DV), dirs), r

    in_c, r_c = prep(parts_c)
    in_l, r_l = prep(parts_l)
    o_c, o_l = bidirectional_gla(in_c, in_l, GLA_DK ** -0.5, ctx_out)

    def post(o, r):
        return rmsnorm(o, norm_g).reshape(o.shape[0], o.shape[1], -1) * jax.nn.silu(r)

    return (post(o_c, r_c) if ctx_out else None), post(o_l, r_l)


def swiglu(h, w1, w3, w2):
    return (jax.nn.silu(h @ w1) * (h @ w3)) @ w2


def moe_swiglu(h, router_w, w1, w3, w2):
    logits = (h @ router_w).astype(jnp.float32)
    top_val, top_idx = lax.top_k(logits, TOP_K)
    top_w = jax.nn.softmax(top_val, axis=-1)
    gate = jnp.sum(jax.nn.one_hot(top_idx, N_EXPERTS, dtype=jnp.float32) * top_w[..., None], axis=-2)
    gate = gate.astype(h.dtype)
    out = jnp.zeros_like(h)
    for e in range(N_EXPERTS):
        out = out + gate[..., e:e + 1] * swiglu(h, w1[e], w3[e], w2[e])
    return out


def channel_mixer(h, l, ffn_w1, ffn_w3, ffn_w2, moe_router, moe_w1, moe_w3, moe_w2):
    if l % 2 == 0:
        return swiglu(h, ffn_w1[l // 2], ffn_w3[l // 2], ffn_w2[l // 2])
    return moe_swiglu(h, moe_router[l // 2], moe_w1[l // 2], moe_w3[l // 2], moe_w2[l // 2])


def setup_inputs(seed: int = 0) -> dict:
    key = jax.random.key(seed)
    ks = jax.random.split(key, 26)
    D = D_MODEL
    nrm = lambda k, shape, s: jax.random.normal(k, shape, jnp.float32) * s
    return {
        'x': nrm(ks[0], (BATCH, SEQ, D), 1.0),
        'c': nrm(ks[1], (BATCH, D), 1.0),
        'ctx': nrm(ks[2], (BATCH, CTX_LEN, D), 1.0),
        'c_ctx': nrm(ks[3], (D,), 1.0),
        'ada_w': nrm(ks[4], (DEPTH, D, 6 * D), 0.5 * D ** -0.5),
        'ada_b': nrm(ks[5], (DEPTH, 6 * D), 0.02),
        'norm_mix': 1.0 + nrm(ks[6], (DEPTH, D), 0.02),
        'norm_ffn': 1.0 + nrm(ks[7], (DEPTH, D), 0.02),
        'w_in': nrm(ks[8], (DEPTH, D, IN_WIDTH), D ** -0.5),
        'w_out': nrm(ks[9], (DEPTH, MIX_WIDTH, D), MIX_WIDTH ** -0.5),
        'diff_lambda': nrm(ks[10], (DEPTH, 4, DIFF_QK), 0.1),
        'diff_norm': 1.0 + nrm(ks[11], (DEPTH, DIFF_V), 0.02),
        'na_rel_bias': nrm(ks[12], (DEPTH, NA_HEADS, 2 * NA_WIN_R - 1, 2 * NA_WIN_C - 1), 0.1),
        'hgrn_lower_bounds': nrm(ks[13], (2, DEPTH, HG_HEADS * HG_DIM), 0.1),
        'hgrn_norm': 1.0 + nrm(ks[14], (DEPTH, HG_DIM), 0.02),
        'gla_gate_w2': nrm(ks[15], (DEPTH, 2, GLA_GATE_RANK, GLA_HEADS * GLA_DK), GLA_GATE_RANK ** -0.5),
        'gla_gate_b': nrm(ks[16], (DEPTH, 2, GLA_HEADS * GLA_DK), 0.1),
        'gla_norm': 1.0 + nrm(ks[17], (DEPTH, GLA_DV), 0.02),
        'ffn_w1': nrm(ks[18], (N_DENSE_LAYERS, D, FFN_DENSE), D ** -0.5),
        'ffn_w3': nrm(ks[19], (N_DENSE_LAYERS, D, FFN_DENSE), D ** -0.5),
        'ffn_w2': nrm(ks[20], (N_DENSE_LAYERS, FFN_DENSE, D), FFN_DENSE ** -0.5),
        'moe_router': nrm(ks[21], (N_MOE_LAYERS, D, N_EXPERTS), D ** -0.5),
        'moe_w1': nrm(ks[22], (N_MOE_LAYERS, N_EXPERTS, D, FFN_EXPERT), D ** -0.5),
        'moe_w3': nrm(ks[23], (N_MOE_LAYERS, N_EXPERTS, D, FFN_EXPERT), D ** -0.5),
        'moe_w2': nrm(ks[24], (N_MOE_LAYERS, N_EXPERTS, FFN_EXPERT, D), FFN_EXPERT ** -0.5),
        'final_norm': 1.0 + nrm(ks[25], (D,), 0.02),
    }


def reference(x, c, ctx, c_ctx, ada_w, ada_b, norm_mix, norm_ffn, w_in, w_out, diff_lambda, diff_norm,
              na_rel_bias, hgrn_lower_bounds, hgrn_norm, gla_gate_w2, gla_gate_b, gla_norm,
              ffn_w1, ffn_w3, ffn_w2, moe_router, moe_w1, moe_w3, moe_w2, final_norm):
    lb_soft = jax.nn.softmax(hgrn_lower_bounds.astype(jnp.float32), axis=1)
    lower_bounds = jnp.clip(jnp.cumsum(lb_soft, axis=1) - lb_soft[:, :1], 0.0, 1.0 - 1e-6)
    cond_l = jax.nn.silu(c)[:, None, :]
    cond_c = jax.nn.silu(c_ctx)[None, None, :]
    xc = ctx
    for l in range(DEPTH):
        ctx_out = l < DEPTH - 1
        mod_l = jnp.split(cond_l @ ada_w[l] + ada_b[l], 6, axis=-1)
        mod_c = jnp.split(cond_c @ ada_w[l] + ada_b[l], 6, axis=-1)
        h_l = rmsnorm(x, norm_mix[l]) * (1.0 + mod_l[1]) + mod_l[0]
        h_c = rmsnorm(xc, norm_mix[l]) * (1.0 + mod_c[1]) + mod_c[0]
        pl = split_cols(h_l @ w_in[l])
        pc = split_cols(h_c @ w_in[l])
        lambda_init = 0.8 - 0.6 * math.exp(-0.3 * l)
        a_c, a_l = diff_attention_mixer(pl[0], pl[1], pl[2], pc[0], pc[1], pc[2],
                                        diff_lambda[l], diff_norm[l], lambda_init, ctx_out)
        n_c, n_l = neighbourhood_attention_mixer(pl[3], pl[4], pl[5], pc[3], pc[4], pc[5],
                                                 na_rel_bias[l], ctx_out)
        r_c, r_l = hgrn2_mixer(pc[6:11], pl[6:11], lower_bounds[0, l], lower_bounds[1, l],
                               hgrn_norm[l], ctx_out)
        g_c, g_l = gla_mixer(pc[11:16], pl[11:16], gla_gate_w2[l], gla_gate_b[l], gla_norm[l], ctx_out)
        x = x + mod_l[2] * (jnp.concatenate([a_l, n_l, r_l, g_l], axis=-1) @ w_out[l])
        h_l = rmsnorm(x, norm_ffn[l]) * (1.0 + mod_l[4]) + mod_l[3]
        x = x + mod_l[5] * channel_mixer(h_l, l, ffn_w1, ffn_w3, ffn_w2, moe_router, moe_w1, moe_w3, moe_w2)
        if ctx_out:
            xc = xc + mod_c[2] * (jnp.concatenate([a_c, n_c, r_c, g_c], axis=-1) @ w_out[l])
            h_c = rmsnorm(xc, norm_ffn[l]) * (1.0 + mod_c[4]) + mod_c[3]
            xc = xc + mod_c[5] * channel_mixer(h_c, l, ffn_w1, ffn_w3, ffn_w2, moe_router, moe_w1, moe_w3, moe_w2)
    return rmsnorm(x, final_norm)
```

```python
import functools
import math

import numpy as np
import jax
import jax.numpy as jnp
from jax import lax
from jax.experimental import pallas as pl
from jax.experimental.pallas import tpu as pltpu

F32 = jnp.float32
BF16 = jnp.bfloat16

D = 2048
NB = 4
SEQ = 2048
NCTX = 256
T = NCTX + SEQ
DEPTH = 2
GRID_W = 64
ROPE_BASE = 10000.0
EPS = 1e-6

DIFF_QK = 32
NA_DIM = 64
NA_WIN_R = 8
NA_WIN_C = 16
NA_GROUP = 4
NA_KROWS = NA_GROUP + NA_WIN_R - 1
GLA_DK = 64
GLA_TAU = 16.0
GLA_RANK = 16
CH = 64
SUB = 16

FFN_DENSE = 5632
N_EXPERTS = 8
FFN_EXPERT = 7168

IN_PAD = 7680
A_Q, A_K, A_V = 0, 4, 8
B_Q, B_K, B_V = 12, 16, 20
C_Q, C_I, C_FF, C_FB, C_G = 6, 7, 8, 9, 10
D_QK, D_V, D_R = 11, 12, 13
D_Z = 56

V7X_VMEM_BYTES = 64 * 1024 * 1024


def _cparams(sem, vmem_mb):
    assert vmem_mb * 1024 * 1024 < V7X_VMEM_BYTES
    return pltpu.CompilerParams(dimension_semantics=sem, vmem_limit_bytes=vmem_mb * 1024 * 1024)


def _sigmoid(x):
    return 1.0 / (1.0 + jnp.exp(-x))


def _silu(x):
    return x * _sigmoid(x)


def _log_sigmoid(x):
    return jnp.minimum(x, 0.0) - jnp.log(1.0 + jnp.exp(-jnp.abs(x)))


def _nt(a, b):
    return lax.dot_general(a, b, (((1,), (1,)), ((), ())), preferred_element_type=F32)


def _tn(a, b):
    return lax.dot_general(a, b, (((0,), (0,)), ((), ())), preferred_element_type=F32)


def _ada_kernel(c_ref, w_ref, b_ref, o_ref):
    s = _silu(c_ref[...]).astype(BF16)
    o_ref[0] = jnp.dot(s, w_ref[0].astype(BF16), preferred_element_type=F32) + b_ref[0]


def _ada_call(cond8, ada_w, ada_b):
    tn = 1536
    n = 6 * D
    return pl.pallas_call(
        _ada_kernel,
        grid=(DEPTH, n // tn),
        in_specs=[
            pl.BlockSpec((8, D), lambda l, j: (0, 0)),
            pl.BlockSpec((1, D, tn), lambda l, j: (l, 0, j)),
            pl.BlockSpec((1, 1, tn), lambda l, j: (l, 0, j)),
        ],
        out_specs=pl.BlockSpec((1, 8, tn), lambda l, j: (l, 0, j)),
        out_shape=jax.ShapeDtypeStruct((DEPTH, 8, n), F32),
        compiler_params=_cparams(("parallel", "parallel"), 40),
        name="ada_mod",
    )(cond8, ada_w, ada_b.reshape(DEPTH, 1, n))


def _mod_spec(k, ctx, nd):
    if nd == 3:
        if ctx:
            return pl.BlockSpec((1, 1, D), lambda b, i, j: (NB * 6 + k, 0, 0))
        return pl.BlockSpec((1, 1, D), lambda b, i, j: (b * 6 + k, 0, 0))
    if ctx:
        return pl.BlockSpec((1, 1, D), lambda b, i: (NB * 6 + k, 0, 0))
    return pl.BlockSpec((1, 1, D), lambda b, i: (b * 6 + k, 0, 0))


def _row_is_ctx(row0, tm):
    return (row0 + lax.broadcasted_iota(jnp.int32, (tm, 1), 0)) < NCTX


def _modnorm(x, g, is_c, shl, scl, shc, scc):
    ms = jnp.mean(x * x, axis=-1, keepdims=True)
    y = x * lax.rsqrt(ms + EPS) * g
    sc = jnp.where(is_c, scc, scl)
    sh = jnp.where(is_c, shc, shl)
    return y * (1.0 + sc) + sh


ROW_STEP = 128


def _modnorm_rows(x_ref, h_ref, row0, tm, g, shl, scl, shc, scc):
    def body(r, carry):
        rows = pl.ds(pl.multiple_of(r * ROW_STEP, ROW_STEP), ROW_STEP)
        is_c = _row_is_ctx(row0 + r * ROW_STEP, ROW_STEP)
        h_ref[rows, :] = _modnorm(x_ref[0, rows, :], g, is_c, shl, scl, shc, scc).astype(BF16)
        return carry

    lax.fori_loop(0, tm // ROW_STEP, body, 0)


def _proj_kernel(x_ref, g_ref, shl_ref, scl_ref, shc_ref, scc_ref, w_ref, o_ref, h_ref, *, tm):
    i = pl.program_id(1)

    @pl.when(pl.program_id(2) == 0)
    def _():
        _modnorm_rows(x_ref, h_ref, i * tm, tm, g_ref[...], shl_ref[0], scl_ref[0], shc_ref[0], scc_ref[0])

    o_ref[0] = jnp.dot(h_ref[...], w_ref[...], preferred_element_type=F32)


def _proj_call(xall, g, mods, w):
    tm, tn = 768, 1536
    return pl.pallas_call(
        functools.partial(_proj_kernel, tm=tm),
        grid=(NB, T // tm, IN_PAD // tn),
        in_specs=[
            pl.BlockSpec((1, tm, D), lambda b, i, j: (b, i, 0)),
            pl.BlockSpec((1, D), lambda b, i, j: (0, 0)),
            _mod_spec(0, False, 3), _mod_spec(1, False, 3), _mod_spec(0, True, 3), _mod_spec(1, True, 3),
            pl.BlockSpec((D, tn), lambda b, i, j: (0, j)),
        ],
        out_specs=pl.BlockSpec((1, tm, tn), lambda b, i, j: (b, i, j)),
        out_shape=jax.ShapeDtypeStruct((NB, T, IN_PAD), F32),
        scratch_shapes=[pltpu.VMEM((tm, D), BF16)],
        compiler_params=_cparams(("parallel", "parallel", "arbitrary"), 48),
        name="in_proj",
    )(xall, g, mods, mods, mods, mods, w)


def _rope(x, cos, sin):
    lane = lax.broadcasted_iota(jnp.int32, x.shape, 1)
    first = (lane & (DIFF_QK - 1)) < (DIFF_QK // 2)
    rot = jnp.where(first, -pltpu.roll(x, 128 - DIFF_QK // 2, 1), pltpu.roll(x, DIFF_QK // 2, 1))
    return x * cos + rot * sin


def _attn_a_kernel(lam_ref, q_ref, k_ref, v_ref, cq_ref, sq_ref, ck_ref, sk_ref, g_ref, o_ref,
                   ks_ref, vs_ref, *, post_scale):
    t = pl.program_id(2)

    @pl.when(t == 0)
    def _():
        ks_ref[...] = _rope(k_ref[0], ck_ref[...], sk_ref[...]).astype(BF16)
        vs_ref[...] = v_ref[0].astype(BF16)

    q = _rope(q_ref[0], cq_ref[...], sq_ref[...]) * (DIFF_QK ** -0.5)
    lam = lam_ref[0]
    lane = lax.broadcasted_iota(jnp.int32, (1, 128), 1)

    def attend(nk):
        kk = ks_ref[0:nk, :]
        vv = vs_ref[0:nk, :]
        outs = []
        for hh in range(2):
            w = None
            for m in range(2):
                lo = 64 * hh + DIFF_QK * m
                qm = jnp.where((lane >= lo) & (lane < lo + DIFF_QK), q, 0.0).astype(BF16)
                s = _nt(qm, kk)
                e = jnp.exp(s - jnp.max(s, axis=-1, keepdims=True))
                r = 1.0 / jnp.sum(e, axis=-1, keepdims=True)
                w = e * r if m == 0 else w - e * (lam * r)
            outs.append(jnp.dot(w.astype(BF16), vv, preferred_element_type=F32))
        o = jnp.where(lane < 64, outs[0], outs[1])
        sq = o * o
        s0 = jnp.sum(jnp.where(lane < 64, sq, 0.0), axis=-1, keepdims=True)
        s1 = jnp.sum(jnp.where(lane >= 64, sq, 0.0), axis=-1, keepdims=True)
        ms = jnp.where(lane < 64, s0, s1) * (1.0 / 64.0)
        o_ref[0] = (o * lax.rsqrt(ms + EPS) * (g_ref[...] * post_scale)).astype(o_ref.dtype)

    @pl.when(t == 0)
    def _():
        attend(NCTX)

    @pl.when(t > 0)
    def _():
        attend(T)


def _attn_a_call(p, lam, cos, sin, g2, post_scale):
    tq = 256
    return pl.pallas_call(
        functools.partial(_attn_a_kernel, post_scale=post_scale),
        grid=(NB, 4, T // tq),
        in_specs=[
            pl.BlockSpec(memory_space=pltpu.SMEM),
            pl.BlockSpec((1, tq, 128), lambda b, h, t: (b, t, A_Q + h)),
            pl.BlockSpec((1, T, 128), lambda b, h, t: (b, 0, A_K + h)),
            pl.BlockSpec((1, T, 128), lambda b, h, t: (b, 0, A_V + h)),
            pl.BlockSpec((tq, 128), lambda b, h, t: (t, 0)),
            pl.BlockSpec((tq, 128), lambda b, h, t: (t, 0)),
            pl.BlockSpec((T, 128), lambda b, h, t: (0, 0)),
            pl.BlockSpec((T, 128), lambda b, h, t: (0, 0)),
            pl.BlockSpec((1, 128), lambda b, h, t: (0, 0)),
        ],
        out_specs=pl.BlockSpec((1, tq, 128), lambda b, h, t: (b, t, h)),
        out_shape=jax.ShapeDtypeStruct((NB, T, 512), BF16),
        scratch_shapes=[pltpu.VMEM((T, 128), BF16), pltpu.VMEM((T, 128), BF16)],
        compiler_params=_cparams(("parallel", "parallel", "arbitrary"), 48),
        name="diff_attn",
    )(lam, p, p, p, cos, sin, cos, sin, g2)


def _attn_na_kernel(q_ref, k_ref, v_ref, bias_ref, o_ref):
    g = pl.program_id(2)
    q = q_ref[0] * (NA_DIM ** -0.5)
    lane = lax.broadcasted_iota(jnp.int32, (1, 128), 1)
    kc = k_ref[0, 0:NCTX, :].astype(BF16)
    vc = v_ref[0, 0:NCTX, :].astype(BF16)

    def qhead(hh):
        return jnp.where((lane >= 64 * hh) & (lane < 64 * hh + 64), q, 0.0).astype(BF16)

    def store(outs):
        o_ref[0] = jnp.where(lane < 64, outs[0], outs[1]).astype(o_ref.dtype)

    @pl.when(g == 0)
    def _():
        outs = []
        for hh in range(2):
            s = _nt(qhead(hh), kc)
            e = jnp.exp(s - jnp.max(s, axis=-1, keepdims=True))
            r = 1.0 / jnp.sum(e, axis=-1, keepdims=True)
            outs.append(jnp.dot(e.astype(BF16), vc, preferred_element_type=F32) * r)
        store(outs)

    @pl.when(g > 0)
    def _():
        u0 = jnp.clip(NA_GROUP * (g - 1) - NA_WIN_R // 2, 0, SEQ // GRID_W - NA_KROWS)
        start = pl.multiple_of(NCTX + GRID_W * u0, GRID_W)
        kw = k_ref[0, pl.ds(start, NA_KROWS * GRID_W), :].astype(BF16)
        vw = v_ref[0, pl.ds(start, NA_KROWS * GRID_W), :].astype(BF16)
        outs = []
        for hh in range(2):
            qm = qhead(hh)
            sl = _nt(qm, kw) + bias_ref[hh, 0]
            sc = _nt(qm, kc)
            mx = jnp.maximum(jnp.max(sl, axis=-1, keepdims=True), jnp.max(sc, axis=-1, keepdims=True))
            el = jnp.exp(sl - mx)
            ec = jnp.exp(sc - mx)
            r = 1.0 / (jnp.sum(el, axis=-1, keepdims=True) + jnp.sum(ec, axis=-1, keepdims=True))
            o = (jnp.dot(el.astype(BF16), vw, preferred_element_type=F32)
                 + jnp.dot(ec.astype(BF16), vc, preferred_element_type=F32))
            outs.append(o * r)
        store(outs)


def _attn_na_call(p, bias):
    tq = NA_GROUP * GRID_W
    nk = NA_KROWS * GRID_W
    return pl.pallas_call(
        _attn_na_kernel,
        grid=(NB, 4, T // tq),
        in_specs=[
            pl.BlockSpec((1, tq, 128), lambda b, h, g: (b, g, B_Q + h)),
            pl.BlockSpec((1, T, 128), lambda b, h, g: (b, 0, B_K + h)),
            pl.BlockSpec((1, T, 128), lambda b, h, g: (b, 0, B_V + h)),
            pl.BlockSpec((2, 1, tq, nk), lambda b, h, g: (h, jnp.maximum(g - 1, 0), 0, 0)),
        ],
        out_specs=pl.BlockSpec((1, tq, 128), lambda b, h, g: (b, g, h)),
        out_shape=jax.ShapeDtypeStruct((NB, T, 512), BF16),
        compiler_params=_cparams(("parallel", "parallel", "arbitrary"), 48),
        name="nbr_attn",
    )(p, p, p, bias)


def _na_bias_table(rel_bias):
    rows = SEQ // GRID_W
    n_grp = rows // NA_GROUP
    cq = np.arange(GRID_W)
    col_start = np.clip(cq - NA_WIN_C // 2, 0, GRID_W - NA_WIN_C)
    col_ok = (cq[None, :] >= col_start[:, None]) & (cq[None, :] < col_start[:, None] + NA_WIN_C)
    dc = np.clip(cq[None, :] - cq[:, None], -(NA_WIN_C - 1), NA_WIN_C - 1) + (NA_WIN_C - 1)
    gi = np.arange(n_grp)[:, None, None]
    qr = np.arange(NA_GROUP)[None, :, None]
    ku = np.arange(NA_KROWS)[None, None, :]
    r = NA_GROUP * gi + qr
    u = np.clip(NA_GROUP * gi - NA_WIN_R // 2, 0, rows - NA_KROWS) + ku
    row_start = np.clip(r - NA_WIN_R // 2, 0, rows - NA_WIN_R)
    row_ok = (u >= row_start) & (u < row_start + NA_WIN_R)
    dr = np.clip(u - r + (NA_WIN_R - 1), 0, 2 * NA_WIN_R - 2)
    rb_c = rel_bias.astype(F32)[:, :, dc]
    tab = rb_c[:, dr]
    ok = row_ok[:, :, :, None, None] & col_ok[None, None, None, :, :]
    tab = jnp.where(ok[None], tab, -jnp.inf)
    tab = tab.transpose(0, 1, 2, 4, 3, 5)
    return tab.reshape(rel_bias.shape[0], n_grp, NA_GROUP * GRID_W, NA_KROWS * GRID_W)


def _cumsum_rows(g, rev):
    r = lax.broadcasted_iota(jnp.int32, (CH, CH), 0)
    c = lax.broadcasted_iota(jnp.int32, (CH, CH), 1)
    tri = jnp.where((c >= r) if rev else (c <= r), 1.0, 0.0).astype(BF16)
    g1 = g.astype(BF16)
    r1 = g - g1.astype(F32)
    g2 = r1.astype(BF16)
    g3 = (r1 - g2.astype(F32)).astype(BF16)
    dot = lambda a: jnp.dot(tri, a, preferred_element_type=F32)
    return dot(g1) + dot(g2) + dot(g3)


def _row_bcast(a, s):
    return jnp.concatenate(
        [jnp.broadcast_to(a[SUB * j + s:SUB * j + s + 1, :], (SUB, 128)) for j in range(CH // SUB)], axis=0)


def _gla_chunk(q, k, g, vs, sts, masks, rev):
    nh = len(vs)
    b = _cumsum_rows(g, rev)
    b_end = b[0:1] if rev else b[CH - 1:CH]
    row = lax.broadcasted_iota(jnp.int32, (CH, 1), 0)
    r2 = lax.broadcasted_iota(jnp.int32, (CH, CH), 0)
    c2 = lax.broadcasted_iota(jnp.int32, (CH, CH), 1)

    def headq(a, h):
        return a if masks[h] is None else jnp.where(masks[h], a, 0.0)

    atts = [jnp.zeros((CH, CH), F32) for _ in range(nh)]
    for n in (32, 16):
        lo, hi = (n, 3 * n) if rev else (n - 1, 3 * n - 1)
        if n == 32:
            ref = b[lo:lo + 1]
        else:
            ref = jnp.where(row < 32, b[lo:lo + 1], b[hi:hi + 1])
        qn = q * jnp.exp(jnp.minimum(b - ref, 0.0))
        kn = (k * jnp.exp(jnp.minimum(ref - b, 0.0))).astype(BF16)
        sh = (2 * n).bit_length() - 1
        same = (r2 >> sh) == (c2 >> sh)
        rin = r2 & (2 * n - 1)
        cin = c2 & (2 * n - 1)
        valid = (same & (rin < n) & (cin >= n)) if rev else (same & (rin >= n) & (cin < n))
        for h in range(nh):
            a = _nt(headq(qn, h).astype(BF16), kn)
            atts[h] = atts[h] + jnp.where(valid, a, 0.0)

    diag = [jnp.zeros((CH, CH), F32) for _ in range(nh)]
    for s in range(SUB):
        e = q * _row_bcast(k, s) * jnp.exp(jnp.minimum(b - _row_bcast(b, s), 0.0))
        hit = (c2 & (SUB - 1)) == s
        for h in range(nh):
            rs = jnp.sum(headq(e, h), axis=-1, keepdims=True)
            diag[h] = jnp.where(hit, rs, diag[h])
    sub_sh = SUB.bit_length() - 1
    dvalid = ((r2 >> sub_sh) == (c2 >> sub_sh)) & ((c2 >= r2) if rev else (c2 <= r2))

    qe = (q * jnp.exp(b)).astype(BF16)
    kd = k * jnp.exp(b_end - b)
    dec = jnp.exp(b_end)
    outs, new = [], []
    for h in range(nh):
        att = atts[h] + jnp.where(dvalid, diag[h], 0.0)
        vb = vs[h].astype(BF16)
        o = jnp.dot(att.astype(BF16), vb, preferred_element_type=F32) + _nt(qe, sts[h].astype(BF16))
        outs.append(o)
        new.append(sts[h] * dec + _tn(vb, headq(kd, h).astype(BF16)))
    return outs, new


def _chunk_maps(col_block, width_blocks=1):
    n_c = NCTX // CH
    n_all = T // CH

    def fwd(b, s):
        return (b, s, col_block)

    def bwd(b, s):
        return (b, jnp.where(s < n_c, n_c - 1 - s, n_all + n_c - 1 - s), col_block)

    return fwd, bwd


def _hgrn_kernel(qf_ref, if_ref, ff_ref, qb_ref, ib_ref, fb_ref, lb_ref, of_ref, ob_ref, st_ref):
    @pl.when(pl.program_id(1) == 0)
    def _():
        st_ref[...] = jnp.zeros_like(st_ref)

    dirs = ((qf_ref, if_ref, ff_ref, of_ref), (qb_ref, ib_ref, fb_ref, ob_ref))
    for d, (q_ref, i_ref, f_ref, o_ref) in enumerate(dirs):
        for h in range(4):
            sl = slice(128 * h, 128 * h + 128)
            lb = lb_ref[d:d + 1, sl]
            q = _silu(q_ref[0, :, sl])
            f = f_ref[0, :, sl]
            e = jnp.exp(-jnp.abs(f))
            inv = 1.0 / (1.0 + e)
            k = (1.0 - lb) * jnp.where(f >= 0.0, e * inv, inv)
            a = jnp.log(lb)
            c = jnp.log(1.0 - lb) + (jnp.minimum(f, 0.0) - jnp.log(1.0 + e))
            g = jnp.maximum(a, c) + jnp.log(1.0 + jnp.exp(-jnp.abs(a - c)))
            outs, new = _gla_chunk(q, k, g, [i_ref[0, :, sl]], [st_ref[d, h]], [None], d == 1)
            o_ref[0, :, sl] = outs[0]
            st_ref[d, h] = new[0]


def _hgrn_call(p, lb):
    fq, bq = _chunk_maps(C_Q)
    fi, bi = _chunk_maps(C_I)
    ff, _ = _chunk_maps(C_FF)
    _, bf = _chunk_maps(C_FB)
    fo, bo = _chunk_maps(0)
    blk = lambda m: pl.BlockSpec((1, CH, 512), m)
    return pl.pallas_call(
        _hgrn_kernel,
        grid=(NB, T // CH),
        in_specs=[blk(fq), blk(fi), blk(ff), blk(bq), blk(bi), blk(bf),
                  pl.BlockSpec((2, 512), lambda b, s: (0, 0))],
        out_specs=[blk(fo), blk(bo)],
        out_shape=[jax.ShapeDtypeStruct((NB, T, 512), F32)] * 2,
        scratch_shapes=[pltpu.VMEM((2, 4, 128, 128), F32)],
        compiler_params=_cparams(("parallel", "arbitrary"), 32),
        name="hgrn2_scan",
    )(p, p, p, p, p, p, lb)


def _gla_kernel(qkf_ref, vf_ref, zf_ref, qkb_ref, vb_ref, zb_ref, w2_ref, b2_ref, of_ref, ob_ref, st_ref):
    @pl.when(pl.program_id(1) == 0)
    def _():
        st_ref[...] = jnp.zeros_like(st_ref)

    lane = lax.broadcasted_iota(jnp.int32, (1, 128), 1)
    masks = [lane < GLA_DK, lane >= GLA_DK]
    dirs = ((qkf_ref, vf_ref, zf_ref, of_ref), (qkb_ref, vb_ref, zb_ref, ob_ref))
    for d, (qk_ref, v_ref, z_ref, o_ref) in enumerate(dirs):
        logit = jnp.dot(z_ref[0].astype(BF16), w2_ref[d].astype(BF16), preferred_element_type=F32) + b2_ref[d]
        g = _log_sigmoid(logit) * (1.0 / GLA_TAU)
        for grp in range(2):
            sl = slice(128 * grp, 128 * grp + 128)
            q = qk_ref[0, :, sl] * (GLA_DK ** -0.5)
            k = qk_ref[0, :, 256 + 128 * grp:256 + 128 * grp + 128]
            heads = (2 * grp, 2 * grp + 1)
            vs = [v_ref[0, :, 128 * h:128 * h + 128] for h in heads]
            sts = [st_ref[d, h] for h in heads]
            outs, new = _gla_chunk(q, k, g[:, sl], vs, sts, masks, d == 1)
            for j, h in enumerate(heads):
                o_ref[0, :, 128 * h:128 * h + 128] = outs[j]
                st_ref[d, h] = new[j]


def _gla_call(p, w2p, b2):
    fqk, bqk = _chunk_maps(D_QK)
    fv, bv = _chunk_maps(D_V)
    fz, bz = _chunk_maps(D_Z)
    fo, bo = _chunk_maps(0)
    blk = lambda m: pl.BlockSpec((1, CH, 512), m)
    zblk = lambda m: pl.BlockSpec((1, CH, 128), m)
    return pl.pallas_call(
        _gla_kernel,
        grid=(NB, T // CH),
        in_specs=[blk(fqk), blk(fv), zblk(fz), blk(bqk), blk(bv), zblk(bz),
                  pl.BlockSpec((2, 128, 256), lambda b, s: (0, 0, 0)),
                  pl.BlockSpec((2, 1, 256), lambda b, s: (0, 0, 0))],
        out_specs=[blk(fo), blk(bo)],
        out_shape=[jax.ShapeDtypeStruct((NB, T, 512), F32)] * 2,
        scratch_shapes=[pltpu.VMEM((2, 4, 128, 128), F32)],
        compiler_params=_cparams(("parallel", "arbitrary"), 32),
        name="gla_scan",
    )(p, p, p, p, p, p, w2p, b2)


def _gated_norm(o, gate, g):
    parts = []
    for h in range(4):
        sl = slice(128 * h, 128 * h + 128)
        oh = o[:, sl]
        ms = jnp.mean(oh * oh, axis=-1, keepdims=True)
        parts.append((oh * lax.rsqrt(ms + EPS) * g * _silu(gate[:, sl])).astype(BF16))
    return parts


def _mixout_kernel(x_ref, a_ref, n_ref, cf_ref, cb_ref, cg_ref, df_ref, db_ref, dr_ref, gc_ref, gd_ref,
                   gl_ref, gx_ref, w_ref, o_ref, mix_ref, *, tm):
    i = pl.program_id(1)

    @pl.when(pl.program_id(2) == 0)
    def _():
        def body(r, carry):
            rows = pl.ds(pl.multiple_of(r * ROW_STEP, ROW_STEP), ROW_STEP)
            mix_ref[rows, 0:512] = a_ref[0, rows, :]
            mix_ref[rows, 512:1024] = n_ref[0, rows, :]
            oc = cf_ref[0, rows, :] + cb_ref[0, rows, :]
            for h, part in enumerate(_gated_norm(oc, cg_ref[0, rows, :], gc_ref[...])):
                mix_ref[rows, 1024 + 128 * h:1024 + 128 * h + 128] = part
            od = df_ref[0, rows, :] + db_ref[0, rows, :]
            for h, part in enumerate(_gated_norm(od, dr_ref[0, rows, :], gd_ref[...])):
                mix_ref[rows, 1536 + 128 * h:1536 + 128 * h + 128] = part
            return carry

        lax.fori_loop(0, tm // ROW_STEP, body, 0)

    acc = jnp.dot(mix_ref[...], w_ref[...], preferred_element_type=F32)
    gate = jnp.where(_row_is_ctx(i * tm, tm), gx_ref[0], gl_ref[0])
    o_ref[0] = x_ref[0] + gate * acc


def _mixout_call(xall, a, n, cf, cb, df, db, p, gc, gd, mods, w):
    tm, tn = 768, 1024
    row = lambda w_: pl.BlockSpec((1, tm, w_), lambda b, i, j: (b, i, 0))
    pcol = lambda blk: pl.BlockSpec((1, tm, 512), lambda b, i, j: (b, i, blk))
    return pl.pallas_call(
        functools.partial(_mixout_kernel, tm=tm),
        grid=(NB, T // tm, D // tn),
        in_specs=[
            pl.BlockSpec((1, tm, tn), lambda b, i, j: (b, i, j)),
            row(512), row(512), row(512), row(512), pcol(C_G), row(512), row(512), pcol(D_R),
            pl.BlockSpec((1, 128), lambda b, i, j: (0, 0)),
            pl.BlockSpec((1, 128), lambda b, i, j: (0, 0)),
            pl.BlockSpec((1, 1, tn), lambda b, i, j: (b * 6 + 2, 0, j)),
            pl.BlockSpec((1, 1, tn), lambda b, i, j: (NB * 6 + 2, 0, j)),
            pl.BlockSpec((D, tn), lambda b, i, j: (0, j)),
        ],
        out_specs=pl.BlockSpec((1, tm, tn), lambda b, i, j: (b, i, j)),
        out_shape=jax.ShapeDtypeStruct((NB, T, D), F32),
        scratch_shapes=[pltpu.VMEM((tm, D), BF16)],
        compiler_params=_cparams(("parallel", "parallel", "arbitrary"), 52),
        name="out_proj",
    )(xall, a, n, cf, cb, p, df, db, p, gc, gd, mods, mods, w)


def _ffn_kernel(x_ref, g_ref, shl_ref, scl_ref, gl_ref, shc_ref, scc_ref, gx_ref, w1_ref, w3_ref, w2_ref,
                o_ref, h_ref, acc_ref, *, tm, nf):
    i = pl.program_id(1)
    f = pl.program_id(2)

    @pl.when(f == 0)
    def _():
        _modnorm_rows(x_ref, h_ref, i * tm, tm, g_ref[...], shl_ref[0], scl_ref[0], shc_ref[0], scc_ref[0])
        acc_ref[...] = jnp.zeros_like(acc_ref)

    h = h_ref[...]
    u = jnp.dot(h, w1_ref[...], preferred_element_type=F32)
    v = jnp.dot(h, w3_ref[...], preferred_element_type=F32)
    acc_ref[...] += jnp.dot((_silu(u) * v).astype(BF16), w2_ref[...], preferred_element_type=F32)

    @pl.when(f == nf - 1)
    def _():
        gate = jnp.where(_row_is_ctx(i * tm, tm), gx_ref[0], gl_ref[0])
        o_ref[0] = x_ref[0] + gate * acc_ref[...]


def _ffn_call(xall, g, mods, w1, w3, w2):
    tm, tf = 768, 512
    nf = FFN_DENSE // tf
    return pl.pallas_call(
        functools.partial(_ffn_kernel, tm=tm, nf=nf),
        grid=(NB, T // tm, nf),
        in_specs=[
            pl.BlockSpec((1, tm, D), lambda b, i, f: (b, i, 0)),
            pl.BlockSpec((1, D), lambda b, i, f: (0, 0)),
            _mod_spec(3, False, 3), _mod_spec(4, False, 3), _mod_spec(5, False, 3),
            _mod_spec(3, True, 3), _mod_spec(4, True, 3), _mod_spec(5, True, 3),
            pl.BlockSpec((D, tf), lambda b, i, f: (0, f)),
            pl.BlockSpec((D, tf), lambda b, i, f: (0, f)),
            pl.BlockSpec((tf, D), lambda b, i, f: (f, 0)),
        ],
        out_specs=pl.BlockSpec((1, tm, D), lambda b, i, f: (b, i, 0)),
        out_shape=jax.ShapeDtypeStruct((NB, T, D), F32),
        scratch_shapes=[pltpu.VMEM((tm, D), BF16), pltpu.VMEM((tm, D), F32)],
        compiler_params=_cparams(("parallel", "parallel", "arbitrary"), 56),
        name="dense_ffn",
    )(xall, g, mods, mods, mods, mods, mods, mods, w1, w3, w2)


MOE_TM = 512
MOE_TF = 512
MOE_ROWS = 2 * NB * SEQ + N_EXPERTS * MOE_TM
MOE_TILES = MOE_ROWS // MOE_TM


def _route_kernel(x_ref, g_ref, sh_ref, sc_ref, wr_ref, h_ref, r_ref):
    x = x_ref[0]
    ms = jnp.mean(x * x, axis=-1, keepdims=True)
    h = x * lax.rsqrt(ms + EPS) * g_ref[...] * (1.0 + sc_ref[0]) + sh_ref[0]
    h_ref[0] = h
    logits = jnp.dot(h, wr_ref[...], preferred_element_type=F32, precision=lax.Precision.HIGHEST)
    lane = lax.broadcasted_iota(jnp.int32, logits.shape, 1)
    lanef = lane.astype(F32)
    lg = jnp.where(lane < N_EXPERTS, logits, -jnp.inf)
    m1 = jnp.max(lg, axis=-1, keepdims=True)
    i1 = jnp.min(jnp.where(lg == m1, lanef, 128.0), axis=-1, keepdims=True)
    lg2 = jnp.where(lanef == i1, -jnp.inf, lg)
    m2 = jnp.max(lg2, axis=-1, keepdims=True)
    i2 = jnp.min(jnp.where(lg2 == m2, lanef, 128.0), axis=-1, keepdims=True)
    e = jnp.exp(m2 - m1)
    w1 = 1.0 / (1.0 + e)
    w2 = e * w1
    r_ref[0] = jnp.where(lane == 0, i1, jnp.where(lane == 1, i2, jnp.where(lane == 2, w1,
                         jnp.where(lane == 3, w2, 0.0))))


def _route_call(xall, g, mods, wr):
    tm = 256
    return pl.pallas_call(
        _route_kernel,
        grid=(NB, SEQ // tm),
        in_specs=[
            pl.BlockSpec((1, tm, D), lambda b, i: (b, i + NCTX // tm, 0)),
            pl.BlockSpec((1, D), lambda b, i: (0, 0)),
            _mod_spec(3, False, 2), _mod_spec(4, False, 2),
            pl.BlockSpec((D, 128), lambda b, i: (0, 0)),
        ],
        out_specs=[pl.BlockSpec((1, tm, D), lambda b, i: (b, i, 0)),
                   pl.BlockSpec((1, tm, 128), lambda b, i: (b, i, 0))],
        out_shape=[jax.ShapeDtypeStruct((NB, SEQ, D), F32), jax.ShapeDtypeStruct((NB, SEQ, 128), F32)],
        compiler_params=_cparams(("parallel", "parallel"), 32),
        name="moe_route",
    )(xall, g, mods, mods, wr)


def _row_copy(src_hbm, row, dst_ref, r, sem):
    return pltpu.make_async_copy(src_hbm.at[pl.ds(row, 1), :], dst_ref.at[pl.ds(r, 1), :], sem)


def _gather_kernel(idx_ref, src_ref, o_ref, sem, *, tg):
    base = pl.program_id(0) * tg

    def issue(r, c):
        _row_copy(src_ref, idx_ref[base + r], o_ref, r, sem).start()
        return c

    lax.fori_loop(0, tg, issue, 0)

    def drain(r, c):
        _row_copy(src_ref, 0, o_ref, r, sem).wait()
        return c

    lax.fori_loop(0, tg, drain, 0)


def _gather_call(src_rows, h):
    tg = 512
    return pl.pallas_call(
        functools.partial(_gather_kernel, tg=tg),
        grid_spec=pltpu.PrefetchScalarGridSpec(
            num_scalar_prefetch=1,
            grid=(MOE_ROWS // tg,),
            in_specs=[pl.BlockSpec(memory_space=pl.ANY)],
            out_specs=pl.BlockSpec((tg, D), lambda i, idx: (i, 0)),
            scratch_shapes=[pltpu.SemaphoreType.DMA(())],
        ),
        out_shape=jax.ShapeDtypeStruct((MOE_ROWS, D), F32),
        compiler_params=_cparams(("arbitrary",), 32),
        name="moe_gather",
    )(src_rows, h)


def _moe_kernel(te_ref, nt_ref, x_ref, w1_ref, w3_ref, w2_ref, o_ref, xb_ref, acc_ref, *, nf):
    i = pl.program_id(0)
    f = pl.program_id(1)
    live = i < nt_ref[0]

    @pl.when(live & (f == 0))
    def _():
        xb_ref[...] = x_ref[...].astype(BF16)
        acc_ref[...] = jnp.zeros_like(acc_ref)

    @pl.when(live)
    def _():
        h = xb_ref[...]
        u = jnp.dot(h, w1_ref[0], preferred_element_type=F32)
        v = jnp.dot(h, w3_ref[0], preferred_element_type=F32)
        acc_ref[...] += jnp.dot((_silu(u) * v).astype(BF16), w2_ref[0], preferred_element_type=F32)

    @pl.when(live & (f == nf - 1))
    def _():
        o_ref[...] = acc_ref[...]

    @pl.when(jnp.logical_not(live) & (f == nf - 1))
    def _():
        o_ref[...] = jnp.zeros_like(o_ref)


def _moe_call(tile_expert, n_tiles, xs, w1, w3, w2):
    nf = FFN_EXPERT // MOE_TF

    def fidx(i, f, nt):
        return jnp.where(i < nt[0], f, nf - 1)

    return pl.pallas_call(
        functools.partial(_moe_kernel, nf=nf),
        grid_spec=pltpu.PrefetchScalarGridSpec(
            num_scalar_prefetch=2,
            grid=(MOE_TILES, nf),
            in_specs=[
                pl.BlockSpec((MOE_TM, D), lambda i, f, te, nt: (i, 0)),
                pl.BlockSpec((1, D, MOE_TF), lambda i, f, te, nt: (te[i], 0, fidx(i, f, nt))),
                pl.BlockSpec((1, D, MOE_TF), lambda i, f, te, nt: (te[i], 0, fidx(i, f, nt))),
                pl.BlockSpec((1, MOE_TF, D), lambda i, f, te, nt: (te[i], fidx(i, f, nt), 0)),
            ],
            out_specs=pl.BlockSpec((MOE_TM, D), lambda i, f, te, nt: (i, 0)),
            scratch_shapes=[pltpu.VMEM((MOE_TM, D), BF16), pltpu.VMEM((MOE_TM, D), F32)],
        ),
        out_shape=jax.ShapeDtypeStruct((MOE_ROWS, D), F32),
        compiler_params=_cparams(("arbitrary", "arbitrary"), 48),
        name="moe_experts",
    )(tile_expert, n_tiles, xs, w1, w3, w2)


def _combine_kernel(dest_ref, x_ref, r_ref, gm_ref, gf_ref, y_ref, o_ref, y0_ref, y1_ref, sem, *, tc):
    base = (pl.program_id(0) * (SEQ // tc) + pl.program_id(1)) * tc

    def issue(r, c):
        _row_copy(y_ref, dest_ref[2 * (base + r)], y0_ref, r, sem).start()
        _row_copy(y_ref, dest_ref[2 * (base + r) + 1], y1_ref, r, sem).start()
        return c

    lax.fori_loop(0, tc, issue, 0)

    def drain(r, c):
        _row_copy(y_ref, 0, y0_ref, r, sem).wait()
        _row_copy(y_ref, 0, y1_ref, r, sem).wait()
        return c

    lax.fori_loop(0, tc, drain, 0)
    rt = r_ref[0]
    moe = rt[:, 2:3] * y0_ref[...] + rt[:, 3:4] * y1_ref[...]
    x = x_ref[0] + gm_ref[0] * moe
    ms = jnp.mean(x * x, axis=-1, keepdims=True)
    o_ref[0] = x * lax.rsqrt(ms + EPS) * gf_ref[...]


def _combine_call(dest, xall, route, mods, gfinal, y):
    tc = 256
    return pl.pallas_call(
        functools.partial(_combine_kernel, tc=tc),
        grid_spec=pltpu.PrefetchScalarGridSpec(
            num_scalar_prefetch=1,
            grid=(NB, SEQ // tc),
            in_specs=[
                pl.BlockSpec((1, tc, D), lambda b, i, d: (b, i + NCTX // tc, 0)),
                pl.BlockSpec((1, tc, 128), lambda b, i, d: (b, i, 0)),
                pl.BlockSpec((1, 1, D), lambda b, i, d: (b * 6 + 5, 0, 0)),
                pl.BlockSpec((1, D), lambda b, i, d: (0, 0)),
                pl.BlockSpec(memory_space=pl.ANY),
            ],
            out_specs=pl.BlockSpec((1, tc, D), lambda b, i, d: (b, i, 0)),
            scratch_shapes=[pltpu.VMEM((tc, D), F32), pltpu.VMEM((tc, D), F32), pltpu.SemaphoreType.DMA(())],
        ),
        out_shape=jax.ShapeDtypeStruct((NB, SEQ, D), F32),
        compiler_params=_cparams(("arbitrary", "arbitrary"), 32),
        name="moe_combine",
    )(dest, xall, route, mods, gfinal, y)


def _moe_plan(route):
    n_pairs = 2 * NB * SEQ
    e = route[..., 0:2].astype(jnp.int32).reshape(n_pairs)
    onehot = (e[:, None] == jnp.arange(N_EXPERTS, dtype=jnp.int32)[None, :]).astype(jnp.int32)
    csum = jnp.cumsum(onehot, axis=0)
    rank = jnp.sum(onehot * csum, axis=1) - 1
    counts = csum[-1]
    padded = ((counts + MOE_TM - 1) // MOE_TM) * MOE_TM
    ends = jnp.cumsum(padded)
    dest = ((ends - padded)[e] + rank).astype(jnp.int32)
    src_rows = jnp.zeros((MOE_ROWS,), jnp.int32).at[dest].set(jnp.arange(n_pairs, dtype=jnp.int32) // 2)
    tile_start = jnp.arange(MOE_TILES, dtype=jnp.int32) * MOE_TM
    n_tiles = (ends[-1] // MOE_TM).astype(jnp.int32).reshape(1)
    last_live = jnp.minimum(tile_start, ends[-1] - MOE_TM)
    tile_expert = jnp.searchsorted(ends, last_live, side="right").astype(jnp.int32)
    return dest, src_rows, tile_expert, n_tiles


def _rope_tables():
    per_axis = DIFF_QK // 4
    t = np.arange(SEQ)
    inv = ROPE_BASE ** (-np.arange(per_axis, dtype=np.float32) / per_axis)
    ang = np.concatenate([(t // GRID_W).astype(np.float32)[:, None] * inv,
                          (t % GRID_W).astype(np.float32)[:, None] * inv], axis=-1).astype(np.float32)
    cos = np.concatenate([np.ones((NCTX, 2 * per_axis), np.float32), np.cos(ang)], axis=0)
    sin = np.concatenate([np.zeros((NCTX, 2 * per_axis), np.float32), np.sin(ang)], axis=0)
    reps = 128 // (2 * per_axis)
    return jnp.asarray(np.tile(cos, (1, reps))), jnp.asarray(np.tile(sin, (1, reps)))


def _in_proj_weight(w):
    z0 = IN_WIDTH_Z0
    parts = [w[:, :z0], w[:, z0 + 2 * GLA_RANK:], w[:, z0:z0 + 2 * GLA_RANK],
             jnp.zeros((D, IN_PAD - w.shape[1]), w.dtype)]
    return jnp.concatenate(parts, axis=1).astype(BF16)


IN_WIDTH_Z0 = 6656


def kernel(x, c, ctx, c_ctx, ada_w, ada_b, norm_mix, norm_ffn, w_in, w_out, diff_lambda, diff_norm,
           na_rel_bias, hgrn_lower_bounds, hgrn_norm, gla_gate_w2, gla_gate_b, gla_norm,
           ffn_w1, ffn_w3, ffn_w2, moe_router, moe_w1, moe_w3, moe_w2, final_norm):
    lb_soft = jax.nn.softmax(hgrn_lower_bounds.astype(F32), axis=1)
    lower_bounds = jnp.clip(jnp.cumsum(lb_soft, axis=1) - lb_soft[:, :1], 0.0, 1.0 - 1e-6)
    cond8 = jnp.concatenate([c, c_ctx[None, :], jnp.zeros((8 - NB - 1, D), F32)], axis=0)
    mods_all = _ada_call(cond8, ada_w, ada_b).reshape(DEPTH, 8 * 6, 1, D)
    cos, sin = _rope_tables()
    xall = jnp.concatenate([ctx, x], axis=1)

    out = None
    for l in range(DEPTH):
        mods = mods_all[l]
        p = _proj_call(xall, norm_mix[l][None, :], mods, _in_proj_weight(w_in[l]))

        lambda_init = 0.8 - 0.6 * math.exp(-0.3 * l)
        lp = diff_lambda[l].astype(F32)
        lam = (jnp.exp(jnp.sum(lp[0] * lp[1])) - jnp.exp(jnp.sum(lp[2] * lp[3])) + lambda_init).reshape(1)
        a = _attn_a_call(p, lam, cos, sin, jnp.tile(diff_norm[l], 2)[None, :], 1.0 - lambda_init)
        n = _attn_na_call(p, _na_bias_table(na_rel_bias[l]))
        cf, cb = _hgrn_call(p, lower_bounds[:, l])
        w2p = jnp.zeros((2, 128, 4 * GLA_DK), F32)
        w2p = w2p.at[0, 0:GLA_RANK].set(gla_gate_w2[l, 0]).at[1, GLA_RANK:2 * GLA_RANK].set(gla_gate_w2[l, 1])
        df, db = _gla_call(p, w2p, gla_gate_b[l][:, None, :])
        xall = _mixout_call(xall, a, n, cf, cb, df, db, p, hgrn_norm[l][None, :], gla_norm[l][None, :],
                            mods, w_out[l].astype(BF16))

        if l % 2 == 0:
            xall = _ffn_call(xall, norm_ffn[l][None, :], mods, ffn_w1[l // 2].astype(BF16),
                             ffn_w3[l // 2].astype(BF16), ffn_w2[l // 2].astype(BF16))
        else:
            assert l == DEPTH - 1
            wr = jnp.zeros((D, 128), F32).at[:, :N_EXPERTS].set(moe_router[l // 2])
            h, route = _route_call(xall, norm_ffn[l][None, :], mods, wr)
            dest, src_rows, tile_expert, n_tiles = _moe_plan(route)
            xs = _gather_call(src_rows, h.reshape(NB * SEQ, D))
            y = _moe_call(tile_expert, n_tiles, xs, moe_w1[l // 2].astype(BF16), moe_w3[l // 2].astype(BF16),
                          moe_w2[l // 2].astype(BF16))
            out = _combine_call(dest, xall, route, mods, final_norm[None, :], y)
    return out
```

```python
import functools
import math

import numpy as np
import jax
import jax.numpy as jnp
from jax import lax
from jax.experimental import pallas as pl
from jax.experimental.pallas import tpu as pltpu

F32 = jnp.float32
BF16 = jnp.bfloat16
U32 = jnp.uint32

D = 2048
NB = 4
SEQ = 2048
NCTX = 256
T = NCTX + SEQ
DEPTH = 2
GRID_W = 64
ROPE_BASE = 10000.0
EPS = 1e-6
LOG2E = math.log2(math.e)

DIFF_QK = 32
NA_DIM = 64
NA_WIN_R = 8
NA_WIN_C = 16
NA_GROUP = 4
NA_KROWS = NA_GROUP + NA_WIN_R - 1
GLA_DK = 64
GLA_TAU = 16.0
GLA_RANK = 16
CH = 64
SUB = 8

FFN_DENSE = 5632
N_EXPERTS = 8
FFN_EXPERT = 7168

IN_MAIN = 6656
A_Q, A_K, A_V = 0, 4, 8
B_Q, B_K, B_V = 12, 16, 20
C_Q, C_I, C_FF, C_FB, C_G = 6, 7, 8, 9, 10
D_QK, D_V = 11, 12
TAIL_W = 640
TAIL_R, TAIL_Z = 0, 4

V7X_VMEM_BYTES = 64 * 1024 * 1024


def _cparams(sem, vmem_mb):
    assert vmem_mb * 1024 * 1024 < V7X_VMEM_BYTES
    return pltpu.CompilerParams(dimension_semantics=sem, vmem_limit_bytes=vmem_mb * 1024 * 1024)


def _sigmoid(x):
    return 1.0 / (1.0 + jnp.exp(-x))


def _silu(x):
    return x * _sigmoid(x)


def _log_sigmoid(x):
    return jnp.minimum(x, 0.0) - jnp.log(1.0 + jnp.exp(-jnp.abs(x)))


def _nt(a, b):
    return lax.dot_general(a, b, (((1,), (1,)), ((), ())), preferred_element_type=F32)


def _tn(a, b):
    return lax.dot_general(a, b, (((0,), (0,)), ((), ())), preferred_element_type=F32)


def _ada_kernel(c_ref, w_ref, b_ref, o_ref):
    s = _silu(c_ref[...]).astype(BF16)
    o_ref[0] = jnp.dot(s, w_ref[0].astype(BF16), preferred_element_type=F32) + b_ref[0]


def _ada_call(cond8, ada_w, ada_b):
    tn = 1536
    n = 6 * D
    return pl.pallas_call(
        _ada_kernel,
        grid=(DEPTH, n // tn),
        in_specs=[
            pl.BlockSpec((8, D), lambda l, j: (0, 0)),
            pl.BlockSpec((1, D, tn), lambda l, j: (l, 0, j)),
            pl.BlockSpec((1, 1, tn), lambda l, j: (l, 0, j)),
        ],
        out_specs=pl.BlockSpec((1, 8, tn), lambda l, j: (l, 0, j)),
        out_shape=jax.ShapeDtypeStruct((DEPTH, 8, n), F32),
        compiler_params=_cparams(("parallel", "parallel"), 40),
        name="ada_mod",
    )(cond8, ada_w, ada_b.reshape(DEPTH, 1, n))


def _mod_spec(k, ctx):
    if ctx:
        return pl.BlockSpec((1, 1, D), lambda b, i: (NB * 6 + k, 0, 0))
    return pl.BlockSpec((1, 1, D), lambda b, i: (b * 6 + k, 0, 0))


def _row_is_ctx(row0, tm):
    return (row0 + lax.broadcasted_iota(jnp.int32, (tm, 1), 0)) < NCTX


def _modnorm(x, g, is_c, shl, scl, shc, scc):
    ms = jnp.mean(x * x, axis=-1, keepdims=True)
    y = x * lax.rsqrt(ms + EPS) * g
    sc = jnp.where(is_c, scc, scl)
    sh = jnp.where(is_c, shc, shl)
    return y * (1.0 + sc) + sh


def _norm_kernel(*refs, tm, resid):
    if resid:
        x_ref, y_ref, gl_ref, gx_ref, g_ref, shl_ref, scl_ref, shc_ref, scc_ref, xo_ref, h_ref = refs
    else:
        x_ref, g_ref, shl_ref, scl_ref, shc_ref, scc_ref, h_ref = refs
    is_c = _row_is_ctx(pl.program_id(1) * tm, tm)
    x = x_ref[0]
    if resid:
        x = x + jnp.where(is_c, gx_ref[0], gl_ref[0]) * y_ref[0]
        xo_ref[0] = x
    h_ref[0] = _modnorm(x, g_ref[...], is_c, shl_ref[0], scl_ref[0], shc_ref[0], scc_ref[0]).astype(BF16)


def _norm_call(xall, g, mods, k_shift, k_scale, resid=None):
    tm = 256
    row = pl.BlockSpec((1, tm, D), lambda b, i: (b, i, 0))
    in_specs, args = [row], [xall]
    if resid is not None:
        y, mods_prev, k_gate = resid
        in_specs += [row, _mod_spec(k_gate, False), _mod_spec(k_gate, True)]
        args += [y, mods_prev, mods_prev]
    in_specs += [pl.BlockSpec((1, D), lambda b, i: (0, 0)),
                 _mod_spec(k_shift, False), _mod_spec(k_scale, False),
                 _mod_spec(k_shift, True), _mod_spec(k_scale, True)]
    args += [g, mods, mods, mods, mods]
    h_shape = jax.ShapeDtypeStruct((NB, T, D), BF16)
    out_shape, out_specs = h_shape, row
    if resid is not None:
        out_shape, out_specs = [jax.ShapeDtypeStruct((NB, T, D), F32), h_shape], [row, row]
    return pl.pallas_call(
        functools.partial(_norm_kernel, tm=tm, resid=resid is not None),
        grid=(NB, T // tm),
        in_specs=in_specs, out_specs=out_specs, out_shape=out_shape,
        compiler_params=_cparams(("parallel", "parallel"), 32),
        name="mod_norm",
    )(*args)


def _wsmm_kernel(*refs, ksizes, tm, epilogue):
    n_in = len(ksizes)
    ins, w_ref = refs[:n_in], refs[n_in]
    o_ref, wb_ref = refs[-2], refs[-1]

    @pl.when((pl.program_id(1) == 0) & (pl.program_id(2) == 0))
    def _():
        wb_ref[...] = w_ref[...].astype(BF16)

    acc, k0 = None, 0
    for r, ks in zip(ins, ksizes):
        part = jnp.dot(r[0], wb_ref[k0:k0 + ks, :], preferred_element_type=F32)
        acc = part if acc is None else acc + part
        k0 += ks
    if epilogue:
        x_ref, gl_ref, gx_ref = refs[n_in + 1:n_in + 4]
        gate = jnp.where(_row_is_ctx(pl.program_id(2) * tm, tm), gx_ref[0], gl_ref[0])
        acc = x_ref[0] + gate * acc
    o_ref[0] = acc


def _wsmm_call(acts, w, tn, n_j, resid=None, vmem_mb=48, name="matmul"):
    tm = 768
    ksizes = tuple(a.shape[-1] for a in acts)
    kdim = sum(ksizes)
    in_specs = [pl.BlockSpec((1, tm, ks), lambda j, b, i: (b, i, 0)) for ks in ksizes]
    in_specs.append(pl.BlockSpec((kdim, tn), lambda j, b, i: (0, j)))
    args = list(acts) + [w]
    if resid is not None:
        x, mods, k_gate = resid
        in_specs += [pl.BlockSpec((1, tm, tn), lambda j, b, i: (b, i, j)),
                     pl.BlockSpec((1, 1, tn), lambda j, b, i: (b * 6 + k_gate, 0, j)),
                     pl.BlockSpec((1, 1, tn), lambda j, b, i: (NB * 6 + k_gate, 0, j))]
        args += [x, mods, mods]
    return pl.pallas_call(
        functools.partial(_wsmm_kernel, ksizes=ksizes, tm=tm, epilogue=resid is not None),
        grid=(n_j, NB, T // tm),
        in_specs=in_specs,
        out_specs=pl.BlockSpec((1, tm, tn), lambda j, b, i: (b, i, j)),
        out_shape=jax.ShapeDtypeStruct((NB, T, n_j * tn), F32),
        scratch_shapes=[pltpu.VMEM((kdim, tn), BF16)],
        compiler_params=_cparams(("arbitrary", "arbitrary", "arbitrary"), vmem_mb),
        name=name,
    )(*args)


def _rope(x, cos, sin):
    lane = lax.broadcasted_iota(jnp.int32, x.shape, 1)
    first = (lane & (DIFF_QK - 1)) < (DIFF_QK // 2)
    rot = jnp.where(first, -pltpu.roll(x, 128 - DIFF_QK // 2, 1), pltpu.roll(x, DIFF_QK // 2, 1))
    return x * cos + rot * sin


def _attn_a_kernel(lam_ref, q_ref, k_ref, v_ref, cq_ref, sq_ref, ck_ref, sk_ref, g_ref, o_ref,
                   ks_ref, vs_ref, *, post_scale):
    t = pl.program_id(2)

    @pl.when(t == 0)
    def _():
        ks_ref[...] = _rope(k_ref[0], ck_ref[...], sk_ref[...]).astype(BF16)
        vs_ref[...] = v_ref[0].astype(BF16)

    q = _rope(q_ref[0], cq_ref[...], sq_ref[...]) * (DIFF_QK ** -0.5)
    lam = lam_ref[0]
    lane = lax.broadcasted_iota(jnp.int32, (1, 128), 1)

    def attend(nk):
        kk = ks_ref[0:nk, :]
        vv = vs_ref[0:nk, :]
        outs = []
        for hh in range(2):
            w = None
            for m in range(2):
                lo = 64 * hh + DIFF_QK * m
                qm = jnp.where((lane >= lo) & (lane < lo + DIFF_QK), q, 0.0).astype(BF16)
                s = _nt(qm, kk)
                e = jnp.exp(s - jnp.max(s, axis=-1, keepdims=True))
                r = 1.0 / jnp.sum(e, axis=-1, keepdims=True)
                w = e * r if m == 0 else w - e * (lam * r)
            outs.append(jnp.dot(w.astype(BF16), vv, preferred_element_type=F32))
        o = jnp.where(lane < 64, outs[0], outs[1])
        sq = o * o
        s0 = jnp.sum(jnp.where(lane < 64, sq, 0.0), axis=-1, keepdims=True)
        s1 = jnp.sum(jnp.where(lane >= 64, sq, 0.0), axis=-1, keepdims=True)
        ms = jnp.where(lane < 64, s0, s1) * (1.0 / 64.0)
        o_ref[0] = (o * lax.rsqrt(ms + EPS) * (g_ref[...] * post_scale)).astype(o_ref.dtype)

    @pl.when(t == 0)
    def _():
        attend(NCTX)

    @pl.when(t > 0)
    def _():
        attend(T)


def _attn_a_call(p, lam, cos, sin, g2, post_scale):
    tq = 256
    return pl.pallas_call(
        functools.partial(_attn_a_kernel, post_scale=post_scale),
        grid=(NB, 4, T // tq),
        in_specs=[
            pl.BlockSpec(memory_space=pltpu.SMEM),
            pl.BlockSpec((1, tq, 128), lambda b, h, t: (b, t, A_Q + h)),
            pl.BlockSpec((1, T, 128), lambda b, h, t: (b, 0, A_K + h)),
            pl.BlockSpec((1, T, 128), lambda b, h, t: (b, 0, A_V + h)),
            pl.BlockSpec((tq, 128), lambda b, h, t: (t, 0)),
            pl.BlockSpec((tq, 128), lambda b, h, t: (t, 0)),
            pl.BlockSpec((T, 128), lambda b, h, t: (0, 0)),
            pl.BlockSpec((T, 128), lambda b, h, t: (0, 0)),
            pl.BlockSpec((1, 128), lambda b, h, t: (0, 0)),
        ],
        out_specs=pl.BlockSpec((1, tq, 128), lambda b, h, t: (b, t, h)),
        out_shape=jax.ShapeDtypeStruct((NB, T, 512), BF16),
        scratch_shapes=[pltpu.VMEM((T, 128), BF16), pltpu.VMEM((T, 128), BF16)],
        compiler_params=_cparams(("parallel", "parallel", "arbitrary"), 48),
        name="diff_attn",
    )(lam, p, p, p, cos, sin, cos, sin, g2)


def _attn_na_kernel(q_ref, k_ref, v_ref, bias_ref, o_ref):
    g = pl.program_id(2)
    q = q_ref[0] * (NA_DIM ** -0.5)
    lane = lax.broadcasted_iota(jnp.int32, (1, 128), 1)
    kc = k_ref[0, 0:NCTX, :].astype(BF16)
    vc = v_ref[0, 0:NCTX, :].astype(BF16)

    def qhead(hh):
        return jnp.where((lane >= 64 * hh) & (lane < 64 * hh + 64), q, 0.0).astype(BF16)

    def store(outs):
        o_ref[0] = jnp.where(lane < 64, outs[0], outs[1]).astype(o_ref.dtype)

    @pl.when(g == 0)
    def _():
        outs = []
        for hh in range(2):
            s = _nt(qhead(hh), kc)
            e = jnp.exp(s - jnp.max(s, axis=-1, keepdims=True))
            r = 1.0 / jnp.sum(e, axis=-1, keepdims=True)
            outs.append(jnp.dot(e.astype(BF16), vc, preferred_element_type=F32) * r)
        store(outs)

    @pl.when(g > 0)
    def _():
        u0 = jnp.clip(NA_GROUP * (g - 1) - NA_WIN_R // 2, 0, SEQ // GRID_W - NA_KROWS)
        start = pl.multiple_of(NCTX + GRID_W * u0, GRID_W)
        kw = k_ref[0, pl.ds(start, NA_KROWS * GRID_W), :].astype(BF16)
        vw = v_ref[0, pl.ds(start, NA_KROWS * GRID_W), :].astype(BF16)
        outs = []
        for hh in range(2):
            qm = qhead(hh)
            sl = _nt(qm, kw) + bias_ref[hh, 0]
            sc = _nt(qm, kc)
            mx = jnp.maximum(jnp.max(sl, axis=-1, keepdims=True), jnp.max(sc, axis=-1, keepdims=True))
            el = jnp.exp(sl - mx)
            ec = jnp.exp(sc - mx)
            r = 1.0 / (jnp.sum(el, axis=-1, keepdims=True) + jnp.sum(ec, axis=-1, keepdims=True))
            o = (jnp.dot(el.astype(BF16), vw, preferred_element_type=F32)
                 + jnp.dot(ec.astype(BF16), vc, preferred_element_type=F32))
            outs.append(o * r)
        store(outs)


def _attn_na_call(p, bias):
    tq = NA_GROUP * GRID_W
    nk = NA_KROWS * GRID_W
    n_grp = SEQ // tq

    def bias_map(b, h, g):
        grp = jnp.maximum(g - 1, 0)
        return (h, jnp.where(grp == 0, 0, jnp.where(grp == n_grp - 1, 2, 1)), 0, 0)

    return pl.pallas_call(
        _attn_na_kernel,
        grid=(NB, 4, T // tq),
        in_specs=[
            pl.BlockSpec((1, tq, 128), lambda b, h, g: (b, g, B_Q + h)),
            pl.BlockSpec((1, T, 128), lambda b, h, g: (b, 0, B_K + h)),
            pl.BlockSpec((1, T, 128), lambda b, h, g: (b, 0, B_V + h)),
            pl.BlockSpec((2, 1, tq, nk), bias_map),
        ],
        out_specs=pl.BlockSpec((1, tq, 128), lambda b, h, g: (b, g, h)),
        out_shape=jax.ShapeDtypeStruct((NB, T, 512), BF16),
        compiler_params=_cparams(("parallel", "parallel", "arbitrary"), 48),
        name="nbr_attn",
    )(p, p, p, bias)


def _na_bias_table(rel_bias):
    rows = SEQ // GRID_W
    n_grp = rows // NA_GROUP
    cq = np.arange(GRID_W)
    col_start = np.clip(cq - NA_WIN_C // 2, 0, GRID_W - NA_WIN_C)
    col_ok = (cq[None, :] >= col_start[:, None]) & (cq[None, :] < col_start[:, None] + NA_WIN_C)
    dc = np.clip(cq[None, :] - cq[:, None], -(NA_WIN_C - 1), NA_WIN_C - 1) + (NA_WIN_C - 1)
    gi = np.arange(n_grp)[:, None, None]
    qr = np.arange(NA_GROUP)[None, :, None]
    ku = np.arange(NA_KROWS)[None, None, :]
    r = NA_GROUP * gi + qr
    u = np.clip(NA_GROUP * gi - NA_WIN_R // 2, 0, rows - NA_KROWS) + ku
    row_start = np.clip(r - NA_WIN_R // 2, 0, rows - NA_WIN_R)
    row_ok = (u >= row_start) & (u < row_start + NA_WIN_R)
    dr = np.where(row_ok, u - r + (NA_WIN_R - 1), -1)
    for g in range(2, n_grp - 1):
        assert np.array_equal(dr[g], dr[1])
    rbm = jnp.where(col_ok[None, None], rel_bias.astype(F32)[:, :, dc], -jnp.inf)
    ninf = jnp.full((rel_bias.shape[0], GRID_W, GRID_W), -jnp.inf, F32)
    pats = []
    for g in (0, 1, n_grp - 1):
        qrows = []
        for a in range(NA_GROUP):
            blocks = [rbm[:, dr[g, a, b]] if dr[g, a, b] >= 0 else ninf for b in range(NA_KROWS)]
            qrows.append(jnp.concatenate(blocks, axis=-1))
        pats.append(jnp.concatenate(qrows, axis=-2))
    return jnp.stack(pats, axis=1)


def _cumsum_rows(g, rev):
    r = lax.broadcasted_iota(jnp.int32, (CH, CH), 0)
    c = lax.broadcasted_iota(jnp.int32, (CH, CH), 1)
    tri = jnp.where((c >= r) if rev else (c <= r), 1.0, 0.0).astype(BF16)
    g1 = g.astype(BF16)
    r1 = g - g1.astype(F32)
    g2 = r1.astype(BF16)
    g3 = (r1 - g2.astype(F32)).astype(BF16)
    dot = lambda a: jnp.dot(tri, a, preferred_element_type=F32)
    return dot(g1) + dot(g2) + dot(g3)


def _block_rows(a, first, period):
    return jnp.concatenate(
        [jnp.broadcast_to(a[r:r + 1, :], (period, 128)) for r in range(first, CH, period)], axis=0)


def _gla_chunk(q, k, g, vs, sts, masks, rev):
    nh = len(vs)
    b = _cumsum_rows(g * LOG2E, rev)
    b_end = b[0:1] if rev else b[CH - 1:CH]
    r2 = lax.broadcasted_iota(jnp.int32, (CH, CH), 0)
    c2 = lax.broadcasted_iota(jnp.int32, (CH, CH), 1)

    def headq(a, h):
        return a if masks[h] is None else jnp.where(masks[h], a, 0.0)

    atts = [jnp.zeros((CH, CH), F32) for _ in range(nh)]
    n = CH // 2
    while n >= SUB:
        ref = _block_rows(b, n if rev else n - 1, 2 * n) if 2 * n < CH else b[(n if rev else n - 1):(n if rev else n - 1) + 1]
        qn = (q * jnp.exp2(b - ref)).astype(BF16)
        kn = (k * jnp.exp2(ref - b)).astype(BF16)
        sh = (2 * n).bit_length() - 1
        same = (r2 >> sh) == (c2 >> sh)
        rin = r2 & (2 * n - 1)
        cin = c2 & (2 * n - 1)
        valid = (same & (rin < n) & (cin >= n)) if rev else (same & (rin >= n) & (cin < n))
        for h in range(nh):
            a = _nt(headq(qn, h), kn)
            atts[h] = jnp.where(valid, a, atts[h])
        n //= 2

    sub_sh = SUB.bit_length() - 1
    dvalid = ((r2 >> sub_sh) == (c2 >> sub_sh)) & ((c2 >= r2) if rev else (c2 <= r2))
    diag = [jnp.zeros((CH, CH), F32) for _ in range(nh)]
    for s in range(SUB):
        e = q * _block_rows(k, s, SUB) * jnp.exp2(b - _block_rows(b, s, SUB))
        hit = (c2 & (SUB - 1)) == s
        for h in range(nh):
            rs = jnp.sum(headq(e, h), axis=-1, keepdims=True)
            diag[h] = jnp.where(hit, rs, diag[h])

    qe = (q * jnp.exp2(b)).astype(BF16)
    kd = k * jnp.exp2(b_end - b)
    dec = jnp.exp2(b_end)
    outs, new = [], []
    for h in range(nh):
        att = jnp.where(dvalid, diag[h], atts[h])
        vb = vs[h].astype(BF16)
        o = jnp.dot(att.astype(BF16), vb, preferred_element_type=F32) + _nt(qe, sts[h].astype(BF16))
        outs.append(o)
        new.append(sts[h] * dec + _tn(vb, headq(kd, h).astype(BF16)))
    return outs, new


def _chunk_maps(col_block):
    n_c = NCTX // CH
    n_all = T // CH

    def fwd(b, s):
        return (b, s, col_block)

    def bwd(b, s):
        return (b, jnp.where(s < n_c, n_c - 1 - s, n_all + n_c - 1 - s), col_block)

    return fwd, bwd


def _hgrn_kernel(qf_ref, if_ref, ff_ref, qb_ref, ib_ref, fb_ref, lb_ref, of_ref, ob_ref, st_ref):
    @pl.when(pl.program_id(1) == 0)
    def _():
        st_ref[...] = jnp.zeros_like(st_ref)

    dirs = ((qf_ref, if_ref, ff_ref, of_ref), (qb_ref, ib_ref, fb_ref, ob_ref))
    for d, (q_ref, i_ref, f_ref, o_ref) in enumerate(dirs):
        for h in range(4):
            sl = slice(128 * h, 128 * h + 128)
            lb = lb_ref[d:d + 1, sl]
            q = _silu(q_ref[0, :, sl])
            f = f_ref[0, :, sl]
            e = jnp.exp(-jnp.abs(f))
            inv = 1.0 / (1.0 + e)
            k = (1.0 - lb) * jnp.where(f >= 0.0, e * inv, inv)
            a = jnp.log(lb)
            c = jnp.log(1.0 - lb) + (jnp.minimum(f, 0.0) - jnp.log(1.0 + e))
            g = jnp.maximum(a, c) + jnp.log(1.0 + jnp.exp(-jnp.abs(a - c)))
            outs, new = _gla_chunk(q, k, g, [i_ref[0, :, sl]], [st_ref[d, h]], [None], d == 1)
            o_ref[0, :, sl] = outs[0]
            st_ref[d, h] = new[0]


def _hgrn_call(p, lb):
    fq, bq = _chunk_maps(C_Q)
    fi, bi = _chunk_maps(C_I)
    ff, _ = _chunk_maps(C_FF)
    _, bf = _chunk_maps(C_FB)
    fo, bo = _chunk_maps(0)
    blk = lambda m: pl.BlockSpec((1, CH, 512), m)
    return pl.pallas_call(
        _hgrn_kernel,
        grid=(NB, T // CH),
        in_specs=[blk(fq), blk(fi), blk(ff), blk(bq), blk(bi), blk(bf),
                  pl.BlockSpec((2, 512), lambda b, s: (0, 0))],
        out_specs=[blk(fo), blk(bo)],
        out_shape=[jax.ShapeDtypeStruct((NB, T, 512), F32)] * 2,
        scratch_shapes=[pltpu.VMEM((2, 4, 128, 128), F32)],
        compiler_params=_cparams(("parallel", "arbitrary"), 32),
        name="hgrn2_scan",
    )(p, p, p, p, p, p, lb)


def _gla_kernel(qkf_ref, vf_ref, zf_ref, qkb_ref, vb_ref, zb_ref, w2_ref, b2_ref, of_ref, ob_ref, st_ref):
    @pl.when(pl.program_id(1) == 0)
    def _():
        st_ref[...] = jnp.zeros_like(st_ref)

    lane = lax.broadcasted_iota(jnp.int32, (1, 128), 1)
    masks = [lane < GLA_DK, lane >= GLA_DK]
    dirs = ((qkf_ref, vf_ref, zf_ref, of_ref), (qkb_ref, vb_ref, zb_ref, ob_ref))
    for d, (qk_ref, v_ref, z_ref, o_ref) in enumerate(dirs):
        logit = jnp.dot(z_ref[0].astype(BF16), w2_ref[d].astype(BF16), preferred_element_type=F32) + b2_ref[d]
        g = _log_sigmoid(logit) * (1.0 / GLA_TAU)
        for grp in range(2):
            sl = slice(128 * grp, 128 * grp + 128)
            q = qk_ref[0, :, sl] * (GLA_DK ** -0.5)
            k = qk_ref[0, :, 256 + 128 * grp:256 + 128 * grp + 128]
            heads = (2 * grp, 2 * grp + 1)
            vs = [v_ref[0, :, 128 * h:128 * h + 128] for h in heads]
            sts = [st_ref[d, h] for h in heads]
            outs, new = _gla_chunk(q, k, g[:, sl], vs, sts, masks, d == 1)
            for j, h in enumerate(heads):
                o_ref[0, :, 128 * h:128 * h + 128] = outs[j]
                st_ref[d, h] = new[j]


def _gla_call(p, pt, w2p, b2):
    fqk, bqk = _chunk_maps(D_QK)
    fv, bv = _chunk_maps(D_V)
    fz, bz = _chunk_maps(TAIL_Z)
    fo, bo = _chunk_maps(0)
    blk = lambda m: pl.BlockSpec((1, CH, 512), m)
    zblk = lambda m: pl.BlockSpec((1, CH, 128), m)
    return pl.pallas_call(
        _gla_kernel,
        grid=(NB, T // CH),
        in_specs=[blk(fqk), blk(fv), zblk(fz), blk(bqk), blk(bv), zblk(bz),
                  pl.BlockSpec((2, 128, 256), lambda b, s: (0, 0, 0)),
                  pl.BlockSpec((2, 1, 256), lambda b, s: (0, 0, 0))],
        out_specs=[blk(fo), blk(bo)],
        out_shape=[jax.ShapeDtypeStruct((NB, T, 512), F32)] * 2,
        scratch_shapes=[pltpu.VMEM((2, 4, 128, 128), F32)],
        compiler_params=_cparams(("parallel", "arbitrary"), 32),
        name="gla_scan",
    )(p, p, pt, p, p, pt, w2p, b2)


def _gated_norm(o, gate, g):
    parts = []
    for h in range(4):
        sl = slice(128 * h, 128 * h + 128)
        oh = o[:, sl]
        ms = jnp.mean(oh * oh, axis=-1, keepdims=True)
        parts.append((oh * lax.rsqrt(ms + EPS) * g * _silu(gate[:, sl])).astype(BF16))
    return parts


def _cdpost_kernel(cf_ref, cb_ref, cg_ref, df_ref, db_ref, dr_ref, gc_ref, gd_ref, o_ref):
    for h, part in enumerate(_gated_norm(cf_ref[0] + cb_ref[0], cg_ref[0], gc_ref[...])):
        o_ref[0, :, 128 * h:128 * h + 128] = part
    for h, part in enumerate(_gated_norm(df_ref[0] + db_ref[0], dr_ref[0], gd_ref[...])):
        o_ref[0, :, 512 + 128 * h:512 + 128 * h + 128] = part


def _cdpost_call(cf, cb, df, db, p, pt, gc, gd):
    tm = 256
    row = lambda col: pl.BlockSpec((1, tm, 512), lambda b, i: (b, i, col))
    vec = pl.BlockSpec((1, 128), lambda b, i: (0, 0))
    return pl.pallas_call(
        _cdpost_kernel,
        grid=(NB, T // tm),
        in_specs=[row(0), row(0), row(C_G), row(0), row(0), row(TAIL_R), vec, vec],
        out_specs=pl.BlockSpec((1, tm, 1024), lambda b, i: (b, i, 0)),
        out_shape=jax.ShapeDtypeStruct((NB, T, 1024), BF16),
        compiler_params=_cparams(("parallel", "parallel"), 32),
        name="scan_post",
    )(cf, cb, p, df, db, pt, gc, gd)


def _ffn_kernel(h_ref, w1_ref, w3_ref, w2_ref, o_ref):
    @pl.when(pl.program_id(2) == 0)
    def _():
        o_ref[...] = jnp.zeros_like(o_ref)

    h = h_ref[0]
    u = jnp.dot(h, w1_ref[...].astype(BF16), preferred_element_type=F32)
    v = jnp.dot(h, w3_ref[...].astype(BF16), preferred_element_type=F32)
    o_ref[0] += jnp.dot((_silu(u) * v).astype(BF16), w2_ref[...].astype(BF16), preferred_element_type=F32)


def _ffn_call(h, w1, w3, w2):
    tm, tf = 768, 512
    return pl.pallas_call(
        _ffn_kernel,
        grid=(NB, T // tm, FFN_DENSE // tf),
        in_specs=[
            pl.BlockSpec((1, tm, D), lambda b, i, f: (b, i, 0)),
            pl.BlockSpec((D, tf), lambda b, i, f: (0, f)),
            pl.BlockSpec((D, tf), lambda b, i, f: (0, f)),
            pl.BlockSpec((tf, D), lambda b, i, f: (f, 0)),
        ],
        out_specs=pl.BlockSpec((1, tm, D), lambda b, i, f: (b, i, 0)),
        out_shape=jax.ShapeDtypeStruct((NB, T, D), F32),
        compiler_params=_cparams(("parallel", "parallel", "arbitrary"), 56),
        name="dense_ffn",
    )(h, w1, w3, w2)


MOE_TM = 1024
MOE_SUB = 256
MOE_TF = 256
MOE_ROWS = 2 * NB * SEQ + N_EXPERTS * MOE_TM
MOE_TILES = MOE_ROWS // MOE_TM
MOE_PREFETCH_STEPS = 16
MOE_PREFETCH_ROWS = MOE_TM // MOE_PREFETCH_STEPS


def _pack_bf16_halves(h):
    u = lax.bitcast_convert_type(h.astype(BF16).astype(F32), U32)
    return (u[:, :D // 2] >> 16) | (u[:, D // 2:] & jnp.uint32(0xFFFF0000))


def _unpack_bf16_halves(u):
    lo = lax.bitcast_convert_type(u << 16, F32).astype(BF16)
    hi = lax.bitcast_convert_type(u & jnp.uint32(0xFFFF0000), F32).astype(BF16)
    return lo, hi


def _route_kernel(x_ref, g_ref, sh_ref, sc_ref, wr_ref, h_ref, r_ref):
    x = x_ref[0]
    ms = jnp.mean(x * x, axis=-1, keepdims=True)
    h = x * lax.rsqrt(ms + EPS) * g_ref[...] * (1.0 + sc_ref[0]) + sh_ref[0]
    h_ref[0] = _pack_bf16_halves(h)
    logits = jnp.dot(h, wr_ref[...], preferred_element_type=F32, precision=lax.Precision.HIGHEST)
    lane = lax.broadcasted_iota(jnp.int32, logits.shape, 1)
    lanef = lane.astype(F32)
    lg = jnp.where(lane < N_EXPERTS, logits, -jnp.inf)
    m1 = jnp.max(lg, axis=-1, keepdims=True)
    i1 = jnp.min(jnp.where(lg == m1, lanef, 128.0), axis=-1, keepdims=True)
    lg2 = jnp.where(lanef == i1, -jnp.inf, lg)
    m2 = jnp.max(lg2, axis=-1, keepdims=True)
    i2 = jnp.min(jnp.where(lg2 == m2, lanef, 128.0), axis=-1, keepdims=True)
    e = jnp.exp(m2 - m1)
    w1 = 1.0 / (1.0 + e)
    w2 = e * w1
    r_ref[0] = jnp.where(lane == 0, i1, jnp.where(lane == 1, i2, jnp.where(lane == 2, w1,
                         jnp.where(lane == 3, w2, 0.0))))


def _route_call(xall, g, mods, wr):
    tm = 256
    return pl.pallas_call(
        _route_kernel,
        grid=(NB, SEQ // tm),
        in_specs=[
            pl.BlockSpec((1, tm, D), lambda b, i: (b, i + NCTX // tm, 0)),
            pl.BlockSpec((1, D), lambda b, i: (0, 0)),
            _mod_spec(3, False), _mod_spec(4, False),
            pl.BlockSpec((D, 128), lambda b, i: (0, 0)),
        ],
        out_specs=[pl.BlockSpec((1, tm, D // 2), lambda b, i: (b, i, 0)),
                   pl.BlockSpec((1, tm, 128), lambda b, i: (b, i, 0))],
        out_shape=[jax.ShapeDtypeStruct((NB, SEQ, D // 2), U32), jax.ShapeDtypeStruct((NB, SEQ, 128), F32)],
        compiler_params=_cparams(("parallel", "parallel"), 32),
        name="moe_route",
    )(xall, g, mods, mods, wr)


def _moe_kernel(te_ref, nl_ref, src_ref, hp_ref, w1_ref, w3_ref, w2_ref, o_ref, xg_ref, xb_ref, sem):
    i = pl.program_id(0)
    f = pl.program_id(1)
    slot = lax.rem(i, 2)
    n_live = nl_ref[i]

    def start_rows(tile, slot_, r0, n):
        def body(r, c):
            row = src_ref[tile * MOE_TM + r]
            pltpu.make_async_copy(hp_ref.at[pl.ds(row, 1), :], xg_ref.at[slot_, pl.ds(r, 1), :],
                                  sem.at[slot_]).start()
            return c

        lax.fori_loop(r0, r0 + n, body, 0)

    @pl.when((i == 0) & (f == 0) & (n_live > 0))
    def _():
        start_rows(0, 0, 0, MOE_TM)

    nxt = jnp.minimum(i + 1, MOE_TILES - 1)

    @pl.when((i + 1 < MOE_TILES) & (nl_ref[nxt] > 0) & (f < MOE_PREFETCH_STEPS))
    def _():
        start_rows(i + 1, 1 - slot, f * MOE_PREFETCH_ROWS, MOE_PREFETCH_ROWS)

    @pl.when(f == 0)
    def _():
        o_ref[...] = jnp.zeros_like(o_ref)

    @pl.when((f == 0) & (n_live > 0))
    def _():
        pltpu.make_async_copy(hp_ref.at[pl.ds(0, MOE_TM), :], xg_ref.at[slot], sem.at[slot]).wait()
        lo, hi = _unpack_bf16_halves(xg_ref[slot])
        xb_ref[:, :D // 2] = lo
        xb_ref[:, D // 2:] = hi

    @pl.when(n_live > 0)
    def _():
        w1 = w1_ref[0].astype(BF16)
        w3 = w3_ref[0].astype(BF16)
        w2 = w2_ref[0].astype(BF16)
        for k in range(MOE_TM // MOE_SUB):
            @pl.when(k < n_live)
            def _():
                rows = slice(k * MOE_SUB, (k + 1) * MOE_SUB)
                h = xb_ref[rows, :]
                u = jnp.dot(h, w1, preferred_element_type=F32)
                v = jnp.dot(h, w3, preferred_element_type=F32)
                o_ref[rows, :] += jnp.dot((_silu(u) * v).astype(BF16), w2, preferred_element_type=F32)


def _moe_call(tile_expert, tile_live, src_rows, hp, w1, w3, w2):
    nf = FFN_EXPERT // MOE_TF

    def fidx(i, f, nl):
        return jnp.where(nl[i] > 0, f, nf - 1)

    return pl.pallas_call(
        _moe_kernel,
        grid_spec=pltpu.PrefetchScalarGridSpec(
            num_scalar_prefetch=3,
            grid=(MOE_TILES, nf),
            in_specs=[
                pl.BlockSpec(memory_space=pl.ANY),
                pl.BlockSpec((1, D, MOE_TF), lambda i, f, te, nl, src: (te[i], 0, fidx(i, f, nl))),
                pl.BlockSpec((1, D, MOE_TF), lambda i, f, te, nl, src: (te[i], 0, fidx(i, f, nl))),
                pl.BlockSpec((1, MOE_TF, D), lambda i, f, te, nl, src: (te[i], fidx(i, f, nl), 0)),
            ],
            out_specs=pl.BlockSpec((MOE_TM, D), lambda i, f, te, nl, src: (i, 0)),
            scratch_shapes=[pltpu.VMEM((2, MOE_TM, D // 2), U32), pltpu.VMEM((MOE_TM, D), BF16),
                            pltpu.SemaphoreType.DMA((2,))],
        ),
        out_shape=jax.ShapeDtypeStruct((MOE_ROWS, D), F32),
        compiler_params=_cparams(("arbitrary", "arbitrary"), 52),
        name="moe_experts",
    )(tile_expert, tile_live, src_rows, hp, w1, w3, w2)


def _row_copy(src_hbm, row, dst_ref, r, sem):
    return pltpu.make_async_copy(src_hbm.at[pl.ds(row, 1), :], dst_ref.at[pl.ds(r, 1), :], sem)


def _combine_kernel(dest_ref, x_ref, r_ref, gm_ref, gf_ref, y_ref, o_ref, y0_ref, y1_ref, sem, *, tc):
    base = (pl.program_id(0) * (SEQ // tc) + pl.program_id(1)) * tc

    def issue(r, c):
        _row_copy(y_ref, dest_ref[2 * (base + r)], y0_ref, r, sem).start()
        _row_copy(y_ref, dest_ref[2 * (base + r) + 1], y1_ref, r, sem).start()
        return c

    lax.fori_loop(0, tc, issue, 0)

    def drain(r, c):
        _row_copy(y_ref, 0, y0_ref, r, sem).wait()
        _row_copy(y_ref, 0, y1_ref, r, sem).wait()
        return c

    lax.fori_loop(0, tc, drain, 0)
    rt = r_ref[0]
    moe = rt[:, 2:3] * y0_ref[...] + rt[:, 3:4] * y1_ref[...]
    x = x_ref[0] + gm_ref[0] * moe
    ms = jnp.mean(x * x, axis=-1, keepdims=True)
    o_ref[0] = x * lax.rsqrt(ms + EPS) * gf_ref[...]


def _combine_call(dest, xall, route, mods, gfinal, y):
    tc = 256
    return pl.pallas_call(
        functools.partial(_combine_kernel, tc=tc),
        grid_spec=pltpu.PrefetchScalarGridSpec(
            num_scalar_prefetch=1,
            grid=(NB, SEQ // tc),
            in_specs=[
                pl.BlockSpec((1, tc, D), lambda b, i, d: (b, i + NCTX // tc, 0)),
                pl.BlockSpec((1, tc, 128), lambda b, i, d: (b, i, 0)),
                pl.BlockSpec((1, 1, D), lambda b, i, d: (b * 6 + 5, 0, 0)),
                pl.BlockSpec((1, D), lambda b, i, d: (0, 0)),
                pl.BlockSpec(memory_space=pl.ANY),
            ],
            out_specs=pl.BlockSpec((1, tc, D), lambda b, i, d: (b, i, 0)),
            scratch_shapes=[pltpu.VMEM((tc, D), F32), pltpu.VMEM((tc, D), F32), pltpu.SemaphoreType.DMA(())],
        ),
        out_shape=jax.ShapeDtypeStruct((NB, SEQ, D), F32),
        compiler_params=_cparams(("arbitrary", "arbitrary"), 32),
        name="moe_combine",
    )(dest, xall, route, mods, gfinal, y)


def _moe_plan(route):
    n_pairs = 2 * NB * SEQ
    e = route[..., 0:2].astype(jnp.int32).reshape(n_pairs)
    onehot = (e[:, None] == jnp.arange(N_EXPERTS, dtype=jnp.int32)[None, :]).astype(jnp.int32)
    csum = jnp.cumsum(onehot, axis=0)
    rank = jnp.sum(onehot * csum, axis=1) - 1
    counts = csum[-1]
    padded = ((counts + MOE_TM - 1) // MOE_TM) * MOE_TM
    ends = jnp.cumsum(padded)
    starts = ends - padded
    dest = (starts[e] + rank).astype(jnp.int32)
    src_rows = jnp.zeros((MOE_ROWS,), jnp.int32).at[dest].set(jnp.arange(n_pairs, dtype=jnp.int32) // 2)
    tile_start = jnp.arange(MOE_TILES, dtype=jnp.int32) * MOE_TM
    t_eff = jnp.minimum(tile_start, ends[-1] - MOE_TM)
    tile_expert = jnp.sum((ends[None, :] <= t_eff[:, None]).astype(jnp.int32), axis=1)
    rows_left = (starts + counts)[tile_expert] - tile_start
    tile_live = jnp.clip((rows_left + MOE_SUB - 1) // MOE_SUB, 0, MOE_TM // MOE_SUB).astype(jnp.int32)
    return dest, src_rows, tile_expert.astype(jnp.int32), tile_live


def _rope_tables():
    per_axis = DIFF_QK // 4
    t = np.arange(SEQ)
    inv = ROPE_BASE ** (-np.arange(per_axis, dtype=np.float32) / per_axis)
    ang = np.concatenate([(t // GRID_W).astype(np.float32)[:, None] * inv,
                          (t % GRID_W).astype(np.float32)[:, None] * inv], axis=-1).astype(np.float32)
    cos = np.concatenate([np.ones((NCTX, 2 * per_axis), np.float32), np.cos(ang)], axis=0)
    sin = np.concatenate([np.zeros((NCTX, 2 * per_axis), np.float32), np.sin(ang)], axis=0)
    reps = 128 // (2 * per_axis)
    return jnp.asarray(np.tile(cos, (1, reps))), jnp.asarray(np.tile(sin, (1, reps)))


def _in_proj_tail(w):
    z = w[:, IN_MAIN:IN_MAIN + 2 * GLA_RANK]
    r = w[:, IN_MAIN + 2 * GLA_RANK:]
    return jnp.concatenate([r, z, jnp.zeros((D, TAIL_W - r.shape[1] - z.shape[1]), w.dtype)], axis=1)


def kernel(x, c, ctx, c_ctx, ada_w, ada_b, norm_mix, norm_ffn, w_in, w_out, diff_lambda, diff_norm,
           na_rel_bias, hgrn_lower_bounds, hgrn_norm, gla_gate_w2, gla_gate_b, gla_norm,
           ffn_w1, ffn_w3, ffn_w2, moe_router, moe_w1, moe_w3, moe_w2, final_norm):
    lb_soft = jax.nn.softmax(hgrn_lower_bounds.astype(F32), axis=1)
    lower_bounds = jnp.clip(jnp.cumsum(lb_soft, axis=1) - lb_soft[:, :1], 0.0, 1.0 - 1e-6)
    cond8 = jnp.concatenate([c, c_ctx[None, :], jnp.zeros((8 - NB - 1, D), F32)], axis=0)
    mods_all = _ada_call(cond8, ada_w, ada_b).reshape(DEPTH, 8 * 6, 1, D)
    cos, sin = _rope_tables()
    xall = jnp.concatenate([ctx, x], axis=1)

    out = None
    pending = None
    for l in range(DEPTH):
        mods = mods_all[l]
        if pending is None:
            h = _norm_call(xall, norm_mix[l][None, :], mods, 0, 1)
        else:
            xall, h = _norm_call(xall, norm_mix[l][None, :], mods, 0, 1, resid=pending)
            pending = None
        p = _wsmm_call([h], w_in[l], 1664, IN_MAIN // 1664, vmem_mb=56, name="in_proj")
        pt = _wsmm_call([h], _in_proj_tail(w_in[l]), TAIL_W, 1, vmem_mb=32, name="in_proj_tail")

        lambda_init = 0.8 - 0.6 * math.exp(-0.3 * l)
        lp = diff_lambda[l].astype(F32)
        lam = (jnp.exp(jnp.sum(lp[0] * lp[1])) - jnp.exp(jnp.sum(lp[2] * lp[3])) + lambda_init).reshape(1)
        a = _attn_a_call(p, lam, cos, sin, jnp.tile(diff_norm[l], 2)[None, :], 1.0 - lambda_init)
        n = _attn_na_call(p, _na_bias_table(na_rel_bias[l]))
        cf, cb = _hgrn_call(p, lower_bounds[:, l])
        w2p = jnp.zeros((2, 128, 4 * GLA_DK), F32)
        w2p = w2p.at[0, 0:GLA_RANK].set(gla_gate_w2[l, 0]).at[1, GLA_RANK:2 * GLA_RANK].set(gla_gate_w2[l, 1])
        df, db = _gla_call(p, pt, w2p, gla_gate_b[l][:, None, :])
        cd = _cdpost_call(cf, cb, df, db, p, pt, hgrn_norm[l][None, :], gla_norm[l][None, :])
        xall = _wsmm_call([a, n, cd], w_out[l], 1024, D // 1024, resid=(xall, mods, 2), name="out_proj")

        if l % 2 == 0:
            h2 = _norm_call(xall, norm_ffn[l][None, :], mods, 3, 4)
            y = _ffn_call(h2, ffn_w1[l // 2], ffn_w3[l // 2], ffn_w2[l // 2])
            pending = (y, mods, 5)
        else:
            assert l == DEPTH - 1
            wr = jnp.zeros((D, 128), F32).at[:, :N_EXPERTS].set(moe_router[l // 2])
            hp, route = _route_call(xall, norm_ffn[l][None, :], mods, wr)
            dest, src_rows, tile_expert, tile_live = _moe_plan(route)
            y = _moe_call(tile_expert, tile_live, src_rows, hp.reshape(NB * SEQ, D // 2),
                          moe_w1[l // 2], moe_w3[l // 2], moe_w2[l // 2])
            out = _combine_call(dest, xall, route, mods, final_norm[None, :], y)
    return out
```

```python
import functools
import math

import numpy as np
import jax
import jax.numpy as jnp
from jax import lax
from jax.experimental import pallas as pl
from jax.experimental.pallas import tpu as pltpu

F32 = jnp.float32
BF16 = jnp.bfloat16

D = 2048
NB = 4
SEQ = 2048
NCTX = 256
T = NCTX + SEQ
DEPTH = 2
GRID_W = 64
ROPE_BASE = 10000.0
EPS = 1e-6
LOG2E = math.log2(math.e)

DIFF_QK = 32
NA_DIM = 64
NA_WIN_R = 8
NA_WIN_C = 16
NA_GROUP = 4
NA_KROWS = NA_GROUP + NA_WIN_R - 1
GLA_DK = 64
GLA_TAU = 16.0
GLA_RANK = 16
CH = 64
SUB = 8

FFN_DENSE = 5632
N_EXPERTS = 8
FFN_EXPERT = 7168

IN_MAIN = 6656
A_Q, A_K, A_V = 0, 4, 8
B_Q, B_K, B_V = 12, 16, 20
C_Q, C_I, C_FF, C_FB, C_G = 6, 7, 8, 9, 10
D_QK, D_V = 11, 12
TAIL_W = 640
TAIL_R, TAIL_Z = 0, 4

V7X_VMEM_BYTES = 64 * 1024 * 1024


def _cparams(sem, vmem_mb):
    assert vmem_mb * 1024 * 1024 < V7X_VMEM_BYTES
    return pltpu.CompilerParams(dimension_semantics=sem, vmem_limit_bytes=vmem_mb * 1024 * 1024)


def _sigmoid(x):
    return 1.0 / (1.0 + jnp.exp(-x))


def _silu(x):
    return x * _sigmoid(x)


def _log_sigmoid(x):
    return jnp.minimum(x, 0.0) - jnp.log(1.0 + jnp.exp(-jnp.abs(x)))


def _nt(a, b):
    return lax.dot_general(a, b, (((1,), (1,)), ((), ())), preferred_element_type=F32)


def _tn(a, b):
    return lax.dot_general(a, b, (((0,), (0,)), ((), ())), preferred_element_type=F32)


def _ada_kernel(c_ref, w_ref, b_ref, o_ref):
    s = _silu(c_ref[...]).astype(BF16)
    o_ref[0] = jnp.dot(s, w_ref[0].astype(BF16), preferred_element_type=F32) + b_ref[0]


def _ada_call(cond8, ada_w, ada_b):
    tn = 1536
    n = 6 * D
    return pl.pallas_call(
        _ada_kernel,
        grid=(DEPTH, n // tn),
        in_specs=[
            pl.BlockSpec((8, D), lambda l, j: (0, 0)),
            pl.BlockSpec((1, D, tn), lambda l, j: (l, 0, j)),
            pl.BlockSpec((1, 1, tn), lambda l, j: (l, 0, j)),
        ],
        out_specs=pl.BlockSpec((1, 8, tn), lambda l, j: (l, 0, j)),
        out_shape=jax.ShapeDtypeStruct((DEPTH, 8, n), F32),
        compiler_params=_cparams(("parallel", "parallel"), 40),
        name="ada_mod",
    )(cond8, ada_w, ada_b.reshape(DEPTH, 1, n))


def _mod_spec(k, ctx):
    if ctx:
        return pl.BlockSpec((1, 1, D), lambda b, i: (NB * 6 + k, 0, 0))
    return pl.BlockSpec((1, 1, D), lambda b, i: (b * 6 + k, 0, 0))


def _row_is_ctx(row0, tm):
    return (row0 + lax.broadcasted_iota(jnp.int32, (tm, 1), 0)) < NCTX


def _modnorm(x, g, is_c, shl, scl, shc, scc):
    ms = jnp.mean(x * x, axis=-1, keepdims=True)
    y = x * lax.rsqrt(ms + EPS) * g
    sc = jnp.where(is_c, scc, scl)
    sh = jnp.where(is_c, shc, shl)
    return y * (1.0 + sc) + sh


def _norm_kernel(*refs, tm, resid):
    if resid:
        x_ref, y_ref, gl_ref, gx_ref, g_ref, shl_ref, scl_ref, shc_ref, scc_ref, xo_ref, h_ref = refs
    else:
        x_ref, g_ref, shl_ref, scl_ref, shc_ref, scc_ref, h_ref = refs
    is_c = _row_is_ctx(pl.program_id(1) * tm, tm)
    x = x_ref[0]
    if resid:
        x = x + jnp.where(is_c, gx_ref[0], gl_ref[0]) * y_ref[0]
        xo_ref[0] = x
    h_ref[0] = _modnorm(x, g_ref[...], is_c, shl_ref[0], scl_ref[0], shc_ref[0], scc_ref[0]).astype(BF16)


def _norm_call(xall, g, mods, k_shift, k_scale, resid=None):
    tm = 384
    row = pl.BlockSpec((1, tm, D), lambda b, i: (b, i, 0))
    in_specs, args = [row], [xall]
    if resid is not None:
        y, mods_prev, k_gate = resid
        in_specs += [row, _mod_spec(k_gate, False), _mod_spec(k_gate, True)]
        args += [y, mods_prev, mods_prev]
    in_specs += [pl.BlockSpec((1, D), lambda b, i: (0, 0)),
                 _mod_spec(k_shift, False), _mod_spec(k_scale, False),
                 _mod_spec(k_shift, True), _mod_spec(k_scale, True)]
    args += [g, mods, mods, mods, mods]
    h_shape = jax.ShapeDtypeStruct((NB, T, D), BF16)
    out_shape, out_specs = h_shape, row
    if resid is not None:
        out_shape, out_specs = [jax.ShapeDtypeStruct((NB, T, D), F32), h_shape], [row, row]
    return pl.pallas_call(
        functools.partial(_norm_kernel, tm=tm, resid=resid is not None),
        grid=(NB, T // tm),
        in_specs=in_specs, out_specs=out_specs, out_shape=out_shape,
        compiler_params=_cparams(("parallel", "parallel"), 32),
        name="mod_norm",
    )(*args)


def _wsmm_kernel(*refs, ksizes, tm, epilogue):
    n_in = len(ksizes)
    ins, w_ref = refs[:n_in], refs[n_in]
    o_ref, wb_ref = refs[-2], refs[-1]

    @pl.when((pl.program_id(1) == 0) & (pl.program_id(2) == 0))
    def _():
        wb_ref[...] = w_ref[0].astype(BF16)

    acc, k0 = None, 0
    for r, ks in zip(ins, ksizes):
        part = jnp.dot(r[0], wb_ref[k0:k0 + ks, :], preferred_element_type=F32)
        acc = part if acc is None else acc + part
        k0 += ks
    if epilogue:
        x_ref, gl_ref, gx_ref = refs[n_in + 1:n_in + 4]
        gate = jnp.where(_row_is_ctx(pl.program_id(2) * tm, tm), gx_ref[0], gl_ref[0])
        acc = x_ref[0] + gate * acc
    o_ref[0] = acc


def _wsmm_call(acts, w, layer, tn, n_j, resid=None, vmem_mb=48, name="matmul"):
    tm = 768
    ksizes = tuple(a.shape[-1] for a in acts)
    kdim = sum(ksizes)
    in_specs = [pl.BlockSpec((1, tm, ks), lambda j, b, i: (b, i, 0)) for ks in ksizes]
    in_specs.append(pl.BlockSpec((1, kdim, tn), lambda j, b, i: (layer, 0, j)))
    args = list(acts) + [w]
    if resid is not None:
        x, mods, k_gate = resid
        in_specs += [pl.BlockSpec((1, tm, tn), lambda j, b, i: (b, i, j)),
                     pl.BlockSpec((1, 1, tn), lambda j, b, i: (b * 6 + k_gate, 0, j)),
                     pl.BlockSpec((1, 1, tn), lambda j, b, i: (NB * 6 + k_gate, 0, j))]
        args += [x, mods, mods]
    return pl.pallas_call(
        functools.partial(_wsmm_kernel, ksizes=ksizes, tm=tm, epilogue=resid is not None),
        grid=(n_j, NB, T // tm),
        in_specs=in_specs,
        out_specs=pl.BlockSpec((1, tm, tn), lambda j, b, i: (b, i, j)),
        out_shape=jax.ShapeDtypeStruct((NB, T, n_j * tn), F32),
        scratch_shapes=[pltpu.VMEM((kdim, tn), BF16)],
        compiler_params=_cparams(("arbitrary", "arbitrary", "arbitrary"), vmem_mb),
        name=name,
    )(*args)


def _rope(x, cos, sin):
    lane = lax.broadcasted_iota(jnp.int32, x.shape, 1)
    first = (lane & (DIFF_QK - 1)) < (DIFF_QK // 2)
    rot = jnp.where(first, -pltpu.roll(x, 128 - DIFF_QK // 2, 1), pltpu.roll(x, DIFF_QK // 2, 1))
    return x * cos + rot * sin


def _attn_a_kernel(lam_ref, q_ref, k_ref, v_ref, cos_ref, sin_ref, g_ref, o_ref, ks_ref, va_ref, *, post_scale):
    t = pl.program_id(2)
    lane = lax.broadcasted_iota(jnp.int32, (1, 128), 1)
    ones_lane = (64, 0)

    @pl.when(t == 0)
    def _():
        ks_ref[...] = _rope(k_ref[0], cos_ref[...], sin_ref[...]).astype(BF16)
        v = v_ref[0]
        va_ref[0] = jnp.where(lane < 64, v, jnp.where(lane == ones_lane[0], 1.0, 0.0)).astype(BF16)
        va_ref[1] = jnp.where(lane >= 64, v, jnp.where(lane == ones_lane[1], 1.0, 0.0)).astype(BF16)

    lam = lam_ref[0]

    def attend(q0, nq, nk):
        rows = pl.ds(q0, nq)
        q = _rope(q_ref[0, rows, :], cos_ref[rows, :], sin_ref[rows, :]) * (DIFF_QK ** -0.5 * LOG2E)
        kk = ks_ref[0:nk, :]
        outs = []
        for hh in range(2):
            acc = None
            for m in range(2):
                lo = 64 * hh + DIFF_QK * m
                qm = jnp.where((lane >= lo) & (lane < lo + DIFF_QK), q, 0.0).astype(BF16)
                s = _nt(qm, kk)
                e = jnp.exp2((s - jnp.max(s, axis=-1, keepdims=True)).astype(BF16))
                num = jnp.dot(e, va_ref[hh, 0:nk, :], preferred_element_type=F32)
                den = jnp.sum(jnp.where(lane == ones_lane[hh], num, 0.0), axis=-1, keepdims=True)
                acc = num * (1.0 / den) if m == 0 else acc - num * (lam / den)
            outs.append(acc)
        o = jnp.where(lane < 64, outs[0], outs[1])
        sq = o * o
        s0 = jnp.sum(jnp.where(lane < 64, sq, 0.0), axis=-1, keepdims=True)
        s1 = jnp.sum(jnp.where(lane >= 64, sq, 0.0), axis=-1, keepdims=True)
        ms = jnp.where(lane < 64, s0, s1) * (1.0 / 64.0)
        o_ref[0, rows, :] = (o * lax.rsqrt(ms + EPS) * (g_ref[...] * post_scale)).astype(o_ref.dtype)

    @pl.when(t == 0)
    def _():
        attend(0, NCTX, NCTX)

    @pl.when(t > 0)
    def _():
        attend(pl.multiple_of(NCTX + (t - 1) * ATTN_TQ, ATTN_TQ // 2), ATTN_TQ, T)


ATTN_TQ = 512


def _attn_a_call(p, lam, cos, sin, g2, post_scale):
    full = lambda col: pl.BlockSpec((1, T, 128), lambda b, h, t: (b, 0, col + h))
    tab = pl.BlockSpec((T, 128), lambda b, h, t: (0, 0))
    return pl.pallas_call(
        functools.partial(_attn_a_kernel, post_scale=post_scale),
        grid=(NB, 4, 1 + SEQ // ATTN_TQ),
        in_specs=[pl.BlockSpec(memory_space=pltpu.SMEM), full(A_Q), full(A_K), full(A_V), tab, tab,
                  pl.BlockSpec((1, 128), lambda b, h, t: (0, 0))],
        out_specs=pl.BlockSpec((1, T, 128), lambda b, h, t: (b, 0, h)),
        out_shape=jax.ShapeDtypeStruct((NB, T, 512), BF16),
        scratch_shapes=[pltpu.VMEM((T, 128), BF16), pltpu.VMEM((2, T, 128), BF16)],
        compiler_params=_cparams(("parallel", "parallel", "arbitrary"), 48),
        name="diff_attn",
    )(lam, p, p, p, cos, sin, g2)


def _attn_na_kernel(q_ref, k_ref, v_ref, bias_ref, o_ref):
    g = pl.program_id(1)
    lane = lax.broadcasted_iota(jnp.int32, (1, 128), 1)

    def heads(p, fn):
        sl = slice(128 * p, 128 * p + 128)
        q = q_ref[0, :, sl] * (NA_DIM ** -0.5 * LOG2E)
        outs = [fn(2 * p + hh, jnp.where((lane >= 64 * hh) & (lane < 64 * hh + 64), q, 0.0).astype(BF16))
                for hh in range(2)]
        o_ref[0, :, sl] = jnp.where(lane < 64, outs[0], outs[1]).astype(o_ref.dtype)

    @pl.when(g == 0)
    def _():
        for p in range(4):
            sl = slice(128 * p, 128 * p + 128)
            kc = k_ref[0, 0:NCTX, sl].astype(BF16)
            vc = v_ref[0, 0:NCTX, sl].astype(BF16)

            def ctx_head(h, qm):
                s = _nt(qm, kc)
                e = jnp.exp2(s - jnp.max(s, axis=-1, keepdims=True))
                r = 1.0 / jnp.sum(e, axis=-1, keepdims=True)
                return jnp.dot(e.astype(BF16), vc, preferred_element_type=F32) * r

            heads(p, ctx_head)

    @pl.when(g > 0)
    def _():
        u0 = jnp.clip(NA_GROUP * (g - 1) - NA_WIN_R // 2, 0, SEQ // GRID_W - NA_KROWS)
        win = pl.ds(pl.multiple_of(NCTX + GRID_W * u0, GRID_W), NA_KROWS * GRID_W)
        for p in range(4):
            sl = slice(128 * p, 128 * p + 128)
            kc = k_ref[0, 0:NCTX, sl].astype(BF16)
            vc = v_ref[0, 0:NCTX, sl].astype(BF16)
            kw = k_ref[0, win, sl].astype(BF16)
            vw = v_ref[0, win, sl].astype(BF16)

            def lat_head(h, qm):
                sl_ = _nt(qm, kw) + bias_ref[h, 0]
                sc = _nt(qm, kc)
                mx = jnp.maximum(jnp.max(sl_, axis=-1, keepdims=True), jnp.max(sc, axis=-1, keepdims=True))
                el = jnp.exp2(sl_ - mx)
                ec = jnp.exp2(sc - mx)
                r = 1.0 / (jnp.sum(el, axis=-1, keepdims=True) + jnp.sum(ec, axis=-1, keepdims=True))
                o = (jnp.dot(el.astype(BF16), vw, preferred_element_type=F32)
                     + jnp.dot(ec.astype(BF16), vc, preferred_element_type=F32))
                return o * r

            heads(p, lat_head)


def _attn_na_call(p, bias):
    tq = NA_GROUP * GRID_W
    nk = NA_KROWS * GRID_W
    n_grp = SEQ // tq

    def bias_map(b, g):
        grp = jnp.maximum(g - 1, 0)
        return (0, jnp.where(grp == 0, 0, jnp.where(grp == n_grp - 1, 2, 1)), 0, 0)

    return pl.pallas_call(
        _attn_na_kernel,
        grid=(NB, T // tq),
        in_specs=[
            pl.BlockSpec((1, tq, 512), lambda b, g: (b, g, B_Q // 4)),
            pl.BlockSpec((1, T, 512), lambda b, g: (b, 0, B_K // 4)),
            pl.BlockSpec((1, T, 512), lambda b, g: (b, 0, B_V // 4)),
            pl.BlockSpec((8, 1, tq, nk), bias_map),
        ],
        out_specs=pl.BlockSpec((1, tq, 512), lambda b, g: (b, g, 0)),
        out_shape=jax.ShapeDtypeStruct((NB, T, 512), BF16),
        compiler_params=_cparams(("parallel", "arbitrary"), 48),
        name="nbr_attn",
    )(p, p, p, bias)


def _na_bias_table(rel_bias):
    rows = SEQ // GRID_W
    n_grp = rows // NA_GROUP
    cq = np.arange(GRID_W)
    col_start = np.clip(cq - NA_WIN_C // 2, 0, GRID_W - NA_WIN_C)
    col_ok = (cq[None, :] >= col_start[:, None]) & (cq[None, :] < col_start[:, None] + NA_WIN_C)
    dc = np.clip(cq[None, :] - cq[:, None], -(NA_WIN_C - 1), NA_WIN_C - 1) + (NA_WIN_C - 1)
    gi = np.arange(n_grp)[:, None, None]
    qr = np.arange(NA_GROUP)[None, :, None]
    ku = np.arange(NA_KROWS)[None, None, :]
    r = NA_GROUP * gi + qr
    u = np.clip(NA_GROUP * gi - NA_WIN_R // 2, 0, rows - NA_KROWS) + ku
    row_start = np.clip(r - NA_WIN_R // 2, 0, rows - NA_WIN_R)
    row_ok = (u >= row_start) & (u < row_start + NA_WIN_R)
    dr = np.where(row_ok, u - r + (NA_WIN_R - 1), -1)
    for g in range(2, n_grp - 1):
        assert np.array_equal(dr[g], dr[1])
    assert np.all((dc == cq[None, :] - cq[:, None] + NA_WIN_C - 1)[col_ok])
    rbp = jnp.pad(rel_bias.astype(F32) * LOG2E, ((0, 0), (0, 0), (GRID_W, GRID_W)))
    rbt = jnp.stack([rbp[:, :, GRID_W + NA_WIN_C - 1 - q:2 * GRID_W + NA_WIN_C - 1 - q] for q in range(GRID_W)],
                    axis=2)
    rbm = jnp.where(col_ok[None, None], rbt, -jnp.inf)
    ninf = jnp.full((rel_bias.shape[0], GRID_W, GRID_W), -jnp.inf, F32)
    pats = []
    for g in (0, 1, n_grp - 1):
        qrows = []
        for a in range(NA_GROUP):
            blocks = [rbm[:, dr[g, a, b]] if dr[g, a, b] >= 0 else ninf for b in range(NA_KROWS)]
            qrows.append(jnp.concatenate(blocks, axis=-1))
        pats.append(jnp.concatenate(qrows, axis=-2))
    return jnp.stack(pats, axis=1)


def _cumsum_rows(g, rev):
    r = lax.broadcasted_iota(jnp.int32, (CH, CH), 0)
    c = lax.broadcasted_iota(jnp.int32, (CH, CH), 1)
    tri = jnp.where((c >= r) if rev else (c <= r), 1.0, 0.0).astype(BF16)
    g1 = g.astype(BF16)
    r1 = g - g1.astype(F32)
    g2 = r1.astype(BF16)
    g3 = (r1 - g2.astype(F32)).astype(BF16)
    dot = lambda a: jnp.dot(tri, a, preferred_element_type=F32)
    return dot(g1) + dot(g2) + dot(g3)


def _block_rows(a, first, period):
    return jnp.concatenate(
        [jnp.broadcast_to(a[r:r + 1, :], (period, 128)) for r in range(first, CH, period)], axis=0)


def _gla_chunk(q, k, g, vs, sts, masks, rev):
    nh = len(vs)
    b = _cumsum_rows(g * LOG2E, rev)
    b_end = b[0:1] if rev else b[CH - 1:CH]
    r2 = lax.broadcasted_iota(jnp.int32, (CH, CH), 0)
    c2 = lax.broadcasted_iota(jnp.int32, (CH, CH), 1)

    def headq(a, h):
        return a if masks[h] is None else jnp.where(masks[h], a, 0.0)

    atts = [jnp.zeros((CH, CH), F32) for _ in range(nh)]
    n = CH // 2
    while n >= SUB:
        first = n if rev else n - 1
        ref = _block_rows(b, first, 2 * n) if 2 * n < CH else b[first:first + 1]
        qn = (q * jnp.exp2(b - ref)).astype(BF16)
        kn = (k * jnp.exp2(ref - b)).astype(BF16)
        sh = (2 * n).bit_length() - 1
        same = (r2 >> sh) == (c2 >> sh)
        rin = r2 & (2 * n - 1)
        cin = c2 & (2 * n - 1)
        valid = (same & (rin < n) & (cin >= n)) if rev else (same & (rin >= n) & (cin < n))
        for h in range(nh):
            a = _nt(headq(qn, h), kn)
            atts[h] = jnp.where(valid, a, atts[h])
        n //= 2

    sub_sh = SUB.bit_length() - 1
    dvalid = ((r2 >> sub_sh) == (c2 >> sub_sh)) & ((c2 >= r2) if rev else (c2 <= r2))
    diag = [jnp.zeros((CH, CH), F32) for _ in range(nh)]
    for s in range(SUB):
        e = q * _block_rows(k, s, SUB) * jnp.exp2(b - _block_rows(b, s, SUB))
        hit = (c2 & (SUB - 1)) == s
        for h in range(nh):
            rs = jnp.sum(headq(e, h), axis=-1, keepdims=True)
            diag[h] = jnp.where(hit, rs, diag[h])

    qe = (q * jnp.exp2(b)).astype(BF16)
    kd = k * jnp.exp2(b_end - b)
    dec = jnp.exp2(b_end)
    outs, new = [], []
    for h in range(nh):
        att = jnp.where(dvalid, diag[h], atts[h])
        vb = vs[h].astype(BF16)
        o = jnp.dot(att.astype(BF16), vb, preferred_element_type=F32) + _nt(qe, sts[h].astype(BF16))
        outs.append(o)
        new.append(sts[h] * dec + _tn(vb, headq(kd, h).astype(BF16)))
    return outs, new


def _chunk_maps(col_block):
    n_c = NCTX // CH
    n_all = T // CH

    def fwd(b, s):
        return (b, s, col_block)

    def bwd(b, s):
        return (b, jnp.where(s < n_c, n_c - 1 - s, n_all + n_c - 1 - s), col_block)

    return fwd, bwd


def _hgrn_kernel(qf_ref, if_ref, ff_ref, qb_ref, ib_ref, fb_ref, lb_ref, of_ref, ob_ref, st_ref):
    @pl.when(pl.program_id(1) == 0)
    def _():
        st_ref[...] = jnp.zeros_like(st_ref)

    dirs = ((qf_ref, if_ref, ff_ref, of_ref), (qb_ref, ib_ref, fb_ref, ob_ref))
    for d, (q_ref, i_ref, f_ref, o_ref) in enumerate(dirs):
        for h in range(4):
            sl = slice(128 * h, 128 * h + 128)
            lb = lb_ref[d:d + 1, sl]
            q = _silu(q_ref[0, :, sl])
            f = f_ref[0, :, sl]
            e = jnp.exp(-jnp.abs(f))
            inv = 1.0 / (1.0 + e)
            k = (1.0 - lb) * jnp.where(f >= 0.0, e * inv, inv)
            a = jnp.log(lb)
            c = jnp.log(1.0 - lb) + (jnp.minimum(f, 0.0) - jnp.log(1.0 + e))
            g = jnp.maximum(a, c) + jnp.log(1.0 + jnp.exp(-jnp.abs(a - c)))
            outs, new = _gla_chunk(q, k, g, [i_ref[0, :, sl]], [st_ref[d, h]], [None], d == 1)
            o_ref[0, :, sl] = outs[0]
            st_ref[d, h] = new[0]


def _hgrn_call(p, lb):
    fq, bq = _chunk_maps(C_Q)
    fi, bi = _chunk_maps(C_I)
    ff, _ = _chunk_maps(C_FF)
    _, bf = _chunk_maps(C_FB)
    fo, bo = _chunk_maps(0)
    blk = lambda m: pl.BlockSpec((1, CH, 512), m)
    return pl.pallas_call(
        _hgrn_kernel,
        grid=(NB, T // CH),
        in_specs=[blk(fq), blk(fi), blk(ff), blk(bq), blk(bi), blk(bf),
                  pl.BlockSpec((2, 512), lambda b, s: (0, 0))],
        out_specs=[blk(fo), blk(bo)],
        out_shape=[jax.ShapeDtypeStruct((NB, T, 512), F32)] * 2,
        scratch_shapes=[pltpu.VMEM((2, 4, 128, 128), F32)],
        compiler_params=_cparams(("parallel", "arbitrary"), 32),
        name="hgrn2_scan",
    )(p, p, p, p, p, p, lb)


def _gla_kernel(qkf_ref, vf_ref, zf_ref, qkb_ref, vb_ref, zb_ref, w2_ref, b2_ref, of_ref, ob_ref, st_ref):
    @pl.when(pl.program_id(1) == 0)
    def _():
        st_ref[...] = jnp.zeros_like(st_ref)

    lane = lax.broadcasted_iota(jnp.int32, (1, 128), 1)
    masks = [lane < GLA_DK, lane >= GLA_DK]
    dirs = ((qkf_ref, vf_ref, zf_ref, of_ref), (qkb_ref, vb_ref, zb_ref, ob_ref))
    for d, (qk_ref, v_ref, z_ref, o_ref) in enumerate(dirs):
        logit = jnp.dot(z_ref[0].astype(BF16), w2_ref[d].astype(BF16), preferred_element_type=F32) + b2_ref[d]
        g = _log_sigmoid(logit) * (1.0 / GLA_TAU)
        for grp in range(2):
            sl = slice(128 * grp, 128 * grp + 128)
            q = qk_ref[0, :, sl] * (GLA_DK ** -0.5)
            k = qk_ref[0, :, 256 + 128 * grp:256 + 128 * grp + 128]
            heads = (2 * grp, 2 * grp + 1)
            vs = [v_ref[0, :, 128 * h:128 * h + 128] for h in heads]
            sts = [st_ref[d, h] for h in heads]
            outs, new = _gla_chunk(q, k, g[:, sl], vs, sts, masks, d == 1)
            for j, h in enumerate(heads):
                o_ref[0, :, 128 * h:128 * h + 128] = outs[j]
                st_ref[d, h] = new[j]


def _gla_call(p, pt, w2p, b2):
    fqk, bqk = _chunk_maps(D_QK)
    fv, bv = _chunk_maps(D_V)
    fz, bz = _chunk_maps(TAIL_Z)
    fo, bo = _chunk_maps(0)
    blk = lambda m: pl.BlockSpec((1, CH, 512), m)
    zblk = lambda m: pl.BlockSpec((1, CH, 128), m)
    return pl.pallas_call(
        _gla_kernel,
        grid=(NB, T // CH),
        in_specs=[blk(fqk), blk(fv), zblk(fz), blk(bqk), blk(bv), zblk(bz),
                  pl.BlockSpec((2, 128, 256), lambda b, s: (0, 0, 0)),
                  pl.BlockSpec((2, 1, 256), lambda b, s: (0, 0, 0))],
        out_specs=[blk(fo), blk(bo)],
        out_shape=[jax.ShapeDtypeStruct((NB, T, 512), F32)] * 2,
        scratch_shapes=[pltpu.VMEM((2, 4, 128, 128), F32)],
        compiler_params=_cparams(("parallel", "arbitrary"), 32),
        name="gla_scan",
    )(p, p, pt, p, p, pt, w2p, b2)


def _gated_norm(o, gate, g):
    parts = []
    for h in range(4):
        sl = slice(128 * h, 128 * h + 128)
        oh = o[:, sl]
        ms = jnp.mean(oh * oh, axis=-1, keepdims=True)
        parts.append((oh * lax.rsqrt(ms + EPS) * g * _silu(gate[:, sl])).astype(BF16))
    return parts


def _cdpost_kernel(cf_ref, cb_ref, cg_ref, df_ref, db_ref, dr_ref, gc_ref, gd_ref, o_ref):
    for h, part in enumerate(_gated_norm(cf_ref[0] + cb_ref[0], cg_ref[0], gc_ref[...])):
        o_ref[0, :, 128 * h:128 * h + 128] = part
    for h, part in enumerate(_gated_norm(df_ref[0] + db_ref[0], dr_ref[0], gd_ref[...])):
        o_ref[0, :, 512 + 128 * h:512 + 128 * h + 128] = part


def _cdpost_call(cf, cb, df, db, p, pt, gc, gd):
    tm = 256
    row = lambda col: pl.BlockSpec((1, tm, 512), lambda b, i: (b, i, col))
    vec = pl.BlockSpec((1, 128), lambda b, i: (0, 0))
    return pl.pallas_call(
        _cdpost_kernel,
        grid=(NB, T // tm),
        in_specs=[row(0), row(0), row(C_G), row(0), row(0), row(TAIL_R), vec, vec],
        out_specs=pl.BlockSpec((1, tm, 1024), lambda b, i: (b, i, 0)),
        out_shape=jax.ShapeDtypeStruct((NB, T, 1024), BF16),
        compiler_params=_cparams(("parallel", "parallel"), 32),
        name="scan_post",
    )(cf, cb, p, df, db, pt, gc, gd)


def _ffn_kernel(h_ref, w1_ref, w3_ref, w2_ref, o_ref):
    @pl.when(pl.program_id(2) == 0)
    def _():
        o_ref[...] = jnp.zeros_like(o_ref)

    h = h_ref[0]
    u = jnp.dot(h, w1_ref[...].astype(BF16), preferred_element_type=F32)
    v = jnp.dot(h, w3_ref[...].astype(BF16), preferred_element_type=F32)
    o_ref[0] += jnp.dot((_silu(u) * v).astype(BF16), w2_ref[...].astype(BF16), preferred_element_type=F32)


def _ffn_call(h, w1, w3, w2):
    tm, tf = 768, 512
    return pl.pallas_call(
        _ffn_kernel,
        grid=(NB, T // tm, FFN_DENSE // tf),
        in_specs=[
            pl.BlockSpec((1, tm, D), lambda b, i, f: (b, i, 0)),
            pl.BlockSpec((D, tf), lambda b, i, f: (0, f)),
            pl.BlockSpec((D, tf), lambda b, i, f: (0, f)),
            pl.BlockSpec((tf, D), lambda b, i, f: (f, 0)),
        ],
        out_specs=pl.BlockSpec((1, tm, D), lambda b, i, f: (b, i, 0)),
        out_shape=jax.ShapeDtypeStruct((NB, T, D), F32),
        compiler_params=_cparams(("parallel", "parallel", "arbitrary"), 56),
        name="dense_ffn",
    )(h, w1, w3, w2)


MOE_TM = 1024
MOE_SUB = 256
MOE_TF = 256
MOE_ROWS = 2 * NB * SEQ + N_EXPERTS * MOE_TM
MOE_TILES = MOE_ROWS // MOE_TM
MOE_PREFETCH_STEPS = 16
MOE_PREFETCH_ROWS = MOE_TM // MOE_PREFETCH_STEPS


def _route_kernel(x_ref, g_ref, sh_ref, sc_ref, wr_ref, h_ref, r_ref):
    x = x_ref[0]
    ms = jnp.mean(x * x, axis=-1, keepdims=True)
    h = x * lax.rsqrt(ms + EPS) * g_ref[...] * (1.0 + sc_ref[0]) + sh_ref[0]
    h_ref[0] = h
    logits = jnp.dot(h, wr_ref[...], preferred_element_type=F32, precision=lax.Precision.HIGHEST)
    lane = lax.broadcasted_iota(jnp.int32, logits.shape, 1)
    lanef = lane.astype(F32)
    lg = jnp.where(lane < N_EXPERTS, logits, -jnp.inf)
    m1 = jnp.max(lg, axis=-1, keepdims=True)
    i1 = jnp.min(jnp.where(lg == m1, lanef, 128.0), axis=-1, keepdims=True)
    lg2 = jnp.where(lanef == i1, -jnp.inf, lg)
    m2 = jnp.max(lg2, axis=-1, keepdims=True)
    i2 = jnp.min(jnp.where(lg2 == m2, lanef, 128.0), axis=-1, keepdims=True)
    e = jnp.exp(m2 - m1)
    w1 = 1.0 / (1.0 + e)
    w2 = e * w1
    r_ref[0] = jnp.where(lane == 0, i1, jnp.where(lane == 1, i2, jnp.where(lane == 2, w1,
                         jnp.where(lane == 3, w2, 0.0))))


def _route_call(xall, g, mods, wr):
    tm = 256
    return pl.pallas_call(
        _route_kernel,
        grid=(NB, SEQ // tm),
        in_specs=[
            pl.BlockSpec((1, tm, D), lambda b, i: (b, i + NCTX // tm, 0)),
            pl.BlockSpec((1, D), lambda b, i: (0, 0)),
            _mod_spec(3, False), _mod_spec(4, False),
            pl.BlockSpec((D, 128), lambda b, i: (0, 0)),
        ],
        out_specs=[pl.BlockSpec((1, tm, D), lambda b, i: (b, i, 0)),
                   pl.BlockSpec((1, tm, 128), lambda b, i: (b, i, 0))],
        out_shape=[jax.ShapeDtypeStruct((NB, SEQ, D), F32), jax.ShapeDtypeStruct((NB, SEQ, 128), F32)],
        compiler_params=_cparams(("parallel", "parallel"), 32),
        name="moe_route",
    )(xall, g, mods, mods, wr)


def _moe_kernel(te_ref, nl_ref, src_ref, h_ref, w1_ref, w3_ref, w2_ref, o_ref, xg_ref, xb_ref, sem):
    i = pl.program_id(0)
    f = pl.program_id(1)
    n_live = nl_ref[i]
    nxt_live = nl_ref[jnp.minimum(i + 1, MOE_TILES - 1)]
    fetch = (i + 1 < MOE_TILES) & (nxt_live > 0) & (f >= 1) & (f <= MOE_PREFETCH_STEPS)

    def row_copy(tile, r):
        row = src_ref[tile * MOE_TM + r]
        return pltpu.make_async_copy(h_ref.at[pl.ds(row, 1), :], xg_ref.at[pl.ds(r, 1), :], sem)

    @pl.when(f == 0)
    def _():
        o_ref[...] = jnp.zeros_like(o_ref)

    @pl.when((i == 0) & (f == 0) & (n_live > 0))
    def _():
        def body(r, c):
            row_copy(0, r).start()
            return c

        lax.fori_loop(0, MOE_TM, body, 0)

    @pl.when((f == 0) & (n_live > 0))
    def _():
        pltpu.make_async_copy(h_ref.at[pl.ds(0, MOE_TM), :], xg_ref, sem).wait()
        xb_ref[...] = xg_ref[...].astype(BF16)

    def step(n, do_fetch):
        def body():
            if do_fetch:
                r0 = (f - 1) * MOE_PREFETCH_ROWS
                for r in range(MOE_PREFETCH_ROWS):
                    row_copy(i + 1, r0 + r).start()
            rows = slice(0, n * MOE_SUB)
            h = xb_ref[rows, :]
            u = jnp.dot(h, w1_ref[0].astype(BF16), preferred_element_type=F32)
            v = jnp.dot(h, w3_ref[0].astype(BF16), preferred_element_type=F32)
            o_ref[rows, :] += jnp.dot((_silu(u) * v).astype(BF16), w2_ref[0].astype(BF16),
                                      preferred_element_type=F32)

        return body

    for n in range(1, MOE_TM // MOE_SUB + 1):
        for do_fetch in (False, True):
            pl.when((n_live == n) & (fetch == do_fetch))(step(n, do_fetch))


def _moe_call(tile_expert, tile_live, src_rows, h, w1, w3, w2):
    nf = FFN_EXPERT // MOE_TF

    def fidx(i, f, nl):
        return jnp.where(nl[i] > 0, f, nf - 1)

    return pl.pallas_call(
        _moe_kernel,
        grid_spec=pltpu.PrefetchScalarGridSpec(
            num_scalar_prefetch=3,
            grid=(MOE_TILES, nf),
            in_specs=[
                pl.BlockSpec(memory_space=pl.ANY),
                pl.BlockSpec((1, D, MOE_TF), lambda i, f, te, nl, src: (te[i], 0, fidx(i, f, nl))),
                pl.BlockSpec((1, D, MOE_TF), lambda i, f, te, nl, src: (te[i], 0, fidx(i, f, nl))),
                pl.BlockSpec((1, MOE_TF, D), lambda i, f, te, nl, src: (te[i], fidx(i, f, nl), 0)),
            ],
            out_specs=pl.BlockSpec((MOE_TM, D), lambda i, f, te, nl, src: (i, 0)),
            scratch_shapes=[pltpu.VMEM((MOE_TM, D), F32), pltpu.VMEM((MOE_TM, D), BF16),
                            pltpu.SemaphoreType.DMA(())],
        ),
        out_shape=jax.ShapeDtypeStruct((MOE_ROWS, D), F32),
        compiler_params=_cparams(("arbitrary", "arbitrary"), 52),
        name="moe_experts",
    )(tile_expert, tile_live, src_rows, h, w1, w3, w2)


def _row_copy(src_hbm, row, dst_ref, r, sem):
    return pltpu.make_async_copy(src_hbm.at[pl.ds(row, 1), :], dst_ref.at[pl.ds(r, 1), :], sem)


def _combine_kernel(dest_ref, x_ref, r_ref, gm_ref, gf_ref, y_ref, o_ref, y0_ref, y1_ref, sem, *, tc):
    base = (pl.program_id(0) * (SEQ // tc) + pl.program_id(1)) * tc

    def issue(r, c):
        _row_copy(y_ref, dest_ref[2 * (base + r)], y0_ref, r, sem).start()
        _row_copy(y_ref, dest_ref[2 * (base + r) + 1], y1_ref, r, sem).start()
        return c

    lax.fori_loop(0, tc, issue, 0)

    def drain(r, c):
        _row_copy(y_ref, 0, y0_ref, r, sem).wait()
        _row_copy(y_ref, 0, y1_ref, r, sem).wait()
        return c

    lax.fori_loop(0, tc, drain, 0)
    rt = r_ref[0]
    moe = rt[:, 2:3] * y0_ref[...] + rt[:, 3:4] * y1_ref[...]
    x = x_ref[0] + gm_ref[0] * moe
    ms = jnp.mean(x * x, axis=-1, keepdims=True)
    o_ref[0] = x * lax.rsqrt(ms + EPS) * gf_ref[...]


def _combine_call(dest, xall, route, mods, gfinal, y):
    tc = 256
    return pl.pallas_call(
        functools.partial(_combine_kernel, tc=tc),
        grid_spec=pltpu.PrefetchScalarGridSpec(
            num_scalar_prefetch=1,
            grid=(NB, SEQ // tc),
            in_specs=[
                pl.BlockSpec((1, tc, D), lambda b, i, d: (b, i + NCTX // tc, 0)),
                pl.BlockSpec((1, tc, 128), lambda b, i, d: (b, i, 0)),
                pl.BlockSpec((1, 1, D), lambda b, i, d: (b * 6 + 5, 0, 0)),
                pl.BlockSpec((1, D), lambda b, i, d: (0, 0)),
                pl.BlockSpec(memory_space=pl.ANY),
            ],
            out_specs=pl.BlockSpec((1, tc, D), lambda b, i, d: (b, i, 0)),
            scratch_shapes=[pltpu.VMEM((tc, D), F32), pltpu.VMEM((tc, D), F32), pltpu.SemaphoreType.DMA(())],
        ),
        out_shape=jax.ShapeDtypeStruct((NB, SEQ, D), F32),
        compiler_params=_cparams(("arbitrary", "arbitrary"), 32),
        name="moe_combine",
    )(dest, xall, route, mods, gfinal, y)


def _moe_plan(route):
    n_pairs = 2 * NB * SEQ
    e = route[..., 0:2].astype(jnp.int32).reshape(n_pairs)
    onehot = (e[:, None] == jnp.arange(N_EXPERTS, dtype=jnp.int32)[None, :]).astype(jnp.int32)
    csum = jnp.cumsum(onehot, axis=0)
    rank = jnp.sum(onehot * csum, axis=1) - 1
    counts = csum[-1]
    padded = ((counts + MOE_TM - 1) // MOE_TM) * MOE_TM
    ends = jnp.cumsum(padded)
    starts = ends - padded
    dest = (starts[e] + rank).astype(jnp.int32)
    src_rows = jnp.zeros((MOE_ROWS,), jnp.int32).at[dest].set(jnp.arange(n_pairs, dtype=jnp.int32) // 2)
    tile_start = jnp.arange(MOE_TILES, dtype=jnp.int32) * MOE_TM
    t_eff = jnp.minimum(tile_start, ends[-1] - MOE_TM)
    tile_expert = jnp.sum((ends[None, :] <= t_eff[:, None]).astype(jnp.int32), axis=1)
    rows_left = (starts + counts)[tile_expert] - tile_start
    tile_live = jnp.clip((rows_left + MOE_SUB - 1) // MOE_SUB, 0, MOE_TM // MOE_SUB).astype(jnp.int32)
    return dest, src_rows, tile_expert.astype(jnp.int32), tile_live


def _rope_tables():
    per_axis = DIFF_QK // 4
    t = np.arange(SEQ)
    inv = ROPE_BASE ** (-np.arange(per_axis, dtype=np.float32) / per_axis)
    ang = np.concatenate([(t // GRID_W).astype(np.float32)[:, None] * inv,
                          (t % GRID_W).astype(np.float32)[:, None] * inv], axis=-1).astype(np.float32)
    cos = np.concatenate([np.ones((NCTX, 2 * per_axis), np.float32), np.cos(ang)], axis=0)
    sin = np.concatenate([np.zeros((NCTX, 2 * per_axis), np.float32), np.sin(ang)], axis=0)
    reps = 128 // (2 * per_axis)
    return jnp.asarray(np.tile(cos, (1, reps))), jnp.asarray(np.tile(sin, (1, reps)))


def _in_proj_tail(w):
    z = w[:, IN_MAIN:IN_MAIN + 2 * GLA_RANK]
    r = w[:, IN_MAIN + 2 * GLA_RANK:]
    return jnp.concatenate([r, z, jnp.zeros((D, TAIL_W - r.shape[1] - z.shape[1]), w.dtype)], axis=1)


def kernel(x, c, ctx, c_ctx, ada_w, ada_b, norm_mix, norm_ffn, w_in, w_out, diff_lambda, diff_norm,
           na_rel_bias, hgrn_lower_bounds, hgrn_norm, gla_gate_w2, gla_gate_b, gla_norm,
           ffn_w1, ffn_w3, ffn_w2, moe_router, moe_w1, moe_w3, moe_w2, final_norm):
    lb_soft = jax.nn.softmax(hgrn_lower_bounds.astype(F32), axis=1)
    lower_bounds = jnp.clip(jnp.cumsum(lb_soft, axis=1) - lb_soft[:, :1], 0.0, 1.0 - 1e-6)
    cond8 = jnp.concatenate([c, c_ctx[None, :], jnp.zeros((8 - NB - 1, D), F32)], axis=0)
    mods_all = _ada_call(cond8, ada_w, ada_b).reshape(DEPTH, 8 * 6, 1, D)
    cos, sin = _rope_tables()
    xall = jnp.concatenate([ctx, x], axis=1)

    out = None
    pending = None
    for l in range(DEPTH):
        mods = mods_all[l]
        if pending is None:
            h = _norm_call(xall, norm_mix[l][None, :], mods, 0, 1)
        else:
            xall, h = _norm_call(xall, norm_mix[l][None, :], mods, 0, 1, resid=pending)
            pending = None
        p = _wsmm_call([h], w_in, l, 1664, IN_MAIN // 1664, vmem_mb=56, name="in_proj")
        pt = _wsmm_call([h], _in_proj_tail(w_in[l])[None], 0, TAIL_W, 1, vmem_mb=32, name="in_proj_tail")

        lambda_init = 0.8 - 0.6 * math.exp(-0.3 * l)
        lp = diff_lambda[l].astype(F32)
        lam = (jnp.exp(jnp.sum(lp[0] * lp[1])) - jnp.exp(jnp.sum(lp[2] * lp[3])) + lambda_init).reshape(1)
        a = _attn_a_call(p, lam, cos, sin, jnp.tile(diff_norm[l], 2)[None, :], 1.0 - lambda_init)
        n = _attn_na_call(p, _na_bias_table(na_rel_bias[l]))
        cf, cb = _hgrn_call(p, lower_bounds[:, l])
        w2p = jnp.zeros((2, 128, 4 * GLA_DK), F32)
        w2p = w2p.at[0, 0:GLA_RANK].set(gla_gate_w2[l, 0]).at[1, GLA_RANK:2 * GLA_RANK].set(gla_gate_w2[l, 1])
        df, db = _gla_call(p, pt, w2p, gla_gate_b[l][:, None, :])
        cd = _cdpost_call(cf, cb, df, db, p, pt, hgrn_norm[l][None, :], gla_norm[l][None, :])
        xall = _wsmm_call([a, n, cd], w_out, l, 1024, D // 1024, resid=(xall, mods, 2), name="out_proj")

        if l % 2 == 0:
            h2 = _norm_call(xall, norm_ffn[l][None, :], mods, 3, 4)
            y = _ffn_call(h2, ffn_w1[l // 2], ffn_w3[l // 2], ffn_w2[l // 2])
            pending = (y, mods, 5)
        else:
            assert l == DEPTH - 1
            wr = jnp.zeros((D, 128), F32).at[:, :N_EXPERTS].set(moe_router[l // 2])
            hr, route = _route_call(xall, norm_ffn[l][None, :], mods, wr)
            dest, src_rows, tile_expert, tile_live = _moe_plan(route)
            y = _moe_call(tile_expert, tile_live, src_rows, hr.reshape(NB * SEQ, D),
                          moe_w1[l // 2], moe_w3[l // 2], moe_w2[l // 2])
            out = _combine_call(dest, xall, route, mods, final_norm[None, :], y)
    return out
```

```python
import functools
import math

import numpy as np
import jax
import jax.numpy as jnp
from jax import lax
from jax.experimental import pallas as pl
from jax.experimental.pallas import tpu as pltpu

F32 = jnp.float32
BF16 = jnp.bfloat16

D = 2048
NB = 4
SEQ = 2048
NCTX = 256
T = NCTX + SEQ
DEPTH = 2
GRID_W = 64
ROPE_BASE = 10000.0
EPS = 1e-6
LOG2E = math.log2(math.e)

DIFF_QK = 32
NA_DIM = 64
NA_WIN_R = 8
NA_WIN_C = 16
NA_GROUP = 4
NA_KROWS = NA_GROUP + NA_WIN_R - 1
GLA_DK = 64
GLA_TAU = 16.0
GLA_RANK = 16
CH = 64
SUB = 8

FFN_DENSE = 5632
N_EXPERTS = 8
FFN_EXPERT = 7168

IN_MAIN = 6656
A_Q, A_K, A_V = 0, 4, 8
B_Q, B_K, B_V = 12, 16, 20
C_Q, C_I, C_FF, C_FB, C_G = 6, 7, 8, 9, 10
D_QK, D_V = 11, 12
TAIL_W = 640
TAIL_R, TAIL_Z = 0, 4

V7X_VMEM_BYTES = 64 * 1024 * 1024


def _cparams(sem, vmem_mb):
    assert vmem_mb * 1024 * 1024 < V7X_VMEM_BYTES
    return pltpu.CompilerParams(dimension_semantics=sem, vmem_limit_bytes=vmem_mb * 1024 * 1024)


def _sigmoid(x):
    return 1.0 / (1.0 + jnp.exp(-x))


def _silu(x):
    return x * _sigmoid(x)


def _log_sigmoid(x):
    return jnp.minimum(x, 0.0) - jnp.log(1.0 + jnp.exp(-jnp.abs(x)))


def _nt(a, b):
    return lax.dot_general(a, b, (((1,), (1,)), ((), ())), preferred_element_type=F32)


def _tn(a, b):
    return lax.dot_general(a, b, (((0,), (0,)), ((), ())), preferred_element_type=F32)


def _ada_kernel(c_ref, w_ref, b_ref, o_ref):
    s = _silu(c_ref[...]).astype(BF16)
    o_ref[0] = jnp.dot(s, w_ref[0].astype(BF16), preferred_element_type=F32) + b_ref[0]


def _ada_call(cond8, ada_w, ada_b):
    tn = 1536
    n = 6 * D
    return pl.pallas_call(
        _ada_kernel,
        grid=(DEPTH, n // tn),
        in_specs=[
            pl.BlockSpec((8, D), lambda l, j: (0, 0)),
            pl.BlockSpec((1, D, tn), lambda l, j: (l, 0, j)),
            pl.BlockSpec((1, 1, tn), lambda l, j: (l, 0, j)),
        ],
        out_specs=pl.BlockSpec((1, 8, tn), lambda l, j: (l, 0, j)),
        out_shape=jax.ShapeDtypeStruct((DEPTH, 8, n), F32),
        compiler_params=_cparams(("parallel", "parallel"), 40),
        name="ada_mod",
    )(cond8, ada_w, ada_b.reshape(DEPTH, 1, n))


def _mod_spec(k, ctx):
    if ctx:
        return pl.BlockSpec((1, 1, D), lambda b, i: (NB * 6 + k, 0, 0))
    return pl.BlockSpec((1, 1, D), lambda b, i: (b * 6 + k, 0, 0))


def _row_is_ctx(row0, tm):
    return (row0 + lax.broadcasted_iota(jnp.int32, (tm, 1), 0)) < NCTX


def _modnorm(x, g, is_c, shl, scl, shc, scc):
    ms = jnp.mean(x * x, axis=-1, keepdims=True)
    y = x * lax.rsqrt(ms + EPS) * g
    sc = jnp.where(is_c, scc, scl)
    sh = jnp.where(is_c, shc, shl)
    return y * (1.0 + sc) + sh


def _norm_kernel(*refs, tm, resid):
    if resid:
        x_ref, y_ref, gl_ref, gx_ref, g_ref, shl_ref, scl_ref, shc_ref, scc_ref, xo_ref, h_ref = refs
    else:
        x_ref, g_ref, shl_ref, scl_ref, shc_ref, scc_ref, h_ref = refs
    is_c = _row_is_ctx(pl.program_id(1) * tm, tm)
    x = x_ref[0]
    if resid:
        x = x + jnp.where(is_c, gx_ref[0], gl_ref[0]) * y_ref[0]
        xo_ref[0] = x
    h_ref[0] = _modnorm(x, g_ref[...], is_c, shl_ref[0], scl_ref[0], shc_ref[0], scc_ref[0]).astype(BF16)


def _norm_call(xall, g, mods, k_shift, k_scale, resid=None):
    tm = 384
    row = pl.BlockSpec((1, tm, D), lambda b, i: (b, i, 0))
    in_specs, args = [row], [xall]
    if resid is not None:
        y, mods_prev, k_gate = resid
        in_specs += [row, _mod_spec(k_gate, False), _mod_spec(k_gate, True)]
        args += [y, mods_prev, mods_prev]
    in_specs += [pl.BlockSpec((1, D), lambda b, i: (0, 0)),
                 _mod_spec(k_shift, False), _mod_spec(k_scale, False),
                 _mod_spec(k_shift, True), _mod_spec(k_scale, True)]
    args += [g, mods, mods, mods, mods]
    h_shape = jax.ShapeDtypeStruct((NB, T, D), BF16)
    out_shape, out_specs = h_shape, row
    if resid is not None:
        out_shape, out_specs = [jax.ShapeDtypeStruct((NB, T, D), F32), h_shape], [row, row]
    return pl.pallas_call(
        functools.partial(_norm_kernel, tm=tm, resid=resid is not None),
        grid=(NB, T // tm),
        in_specs=in_specs, out_specs=out_specs, out_shape=out_shape,
        compiler_params=_cparams(("parallel", "parallel"), 32),
        name="mod_norm",
    )(*args)


def _wsmm_kernel(*refs, ksizes, tm, epilogue, w_t):
    n_in = len(ksizes)
    ins, w_ref = refs[:n_in], refs[n_in]
    o_ref, wb_ref = refs[-2], refs[-1]

    @pl.when((pl.program_id(1) == 0) & (pl.program_id(2) == 0))
    def _():
        wb_ref[...] = w_ref[0].astype(BF16)

    acc, k0 = None, 0
    for r, ks in zip(ins, ksizes):
        if w_t:
            part = _nt(r[0], wb_ref[:, k0:k0 + ks])
        else:
            part = jnp.dot(r[0], wb_ref[k0:k0 + ks, :], preferred_element_type=F32)
        acc = part if acc is None else acc + part
        k0 += ks
    if epilogue:
        x_ref, gl_ref, gx_ref = refs[n_in + 1:n_in + 4]
        gate = jnp.where(_row_is_ctx(pl.program_id(2) * tm, tm), gx_ref[0], gl_ref[0])
        acc = x_ref[0] + gate * acc
    o_ref[0] = acc


def _wsmm_call(acts, w, layer, tn, n_j, resid=None, w_t=False, vmem_mb=48, name="matmul"):
    tm = 768
    ksizes = tuple(a.shape[-1] for a in acts)
    kdim = sum(ksizes)
    in_specs = [pl.BlockSpec((1, tm, ks), lambda j, b, i: (b, i, 0)) for ks in ksizes]
    if w_t:
        in_specs.append(pl.BlockSpec((1, tn, kdim), lambda j, b, i: (layer, j, 0)))
    else:
        in_specs.append(pl.BlockSpec((1, kdim, tn), lambda j, b, i: (layer, 0, j)))
    args = list(acts) + [w]
    if resid is not None:
        x, mods, k_gate = resid
        in_specs += [pl.BlockSpec((1, tm, tn), lambda j, b, i: (b, i, j)),
                     pl.BlockSpec((1, 1, tn), lambda j, b, i: (b * 6 + k_gate, 0, j)),
                     pl.BlockSpec((1, 1, tn), lambda j, b, i: (NB * 6 + k_gate, 0, j))]
        args += [x, mods, mods]
    return pl.pallas_call(
        functools.partial(_wsmm_kernel, ksizes=ksizes, tm=tm, epilogue=resid is not None, w_t=w_t),
        grid=(n_j, NB, T // tm),
        in_specs=in_specs,
        out_specs=pl.BlockSpec((1, tm, tn), lambda j, b, i: (b, i, j)),
        out_shape=jax.ShapeDtypeStruct((NB, T, n_j * tn), F32),
        scratch_shapes=[pltpu.VMEM((tn, kdim) if w_t else (kdim, tn), BF16)],
        compiler_params=_cparams(("arbitrary", "arbitrary", "arbitrary"), vmem_mb),
        name=name,
    )(*args)


def _rope(x, cos, sin):
    lane = lax.broadcasted_iota(jnp.int32, x.shape, 1)
    first = (lane & (DIFF_QK - 1)) < (DIFF_QK // 2)
    rot = jnp.where(first, -pltpu.roll(x, 128 - DIFF_QK // 2, 1), pltpu.roll(x, DIFF_QK // 2, 1))
    return x * cos + rot * sin


def _attn_a_kernel(lam_ref, q_ref, k_ref, v_ref, cos_ref, sin_ref, g_ref, o_ref, ks_ref, va_ref, *, post_scale):
    t = pl.program_id(2)
    lane = lax.broadcasted_iota(jnp.int32, (1, 128), 1)
    ones_lane = (64, 0)

    @pl.when(t == 0)
    def _():
        ks_ref[...] = _rope(k_ref[0], cos_ref[...], sin_ref[...]).astype(BF16)
        v = v_ref[0]
        va_ref[0] = jnp.where(lane < 64, v, jnp.where(lane == ones_lane[0], 1.0, 0.0)).astype(BF16)
        va_ref[1] = jnp.where(lane >= 64, v, jnp.where(lane == ones_lane[1], 1.0, 0.0)).astype(BF16)

    lam = lam_ref[0]

    def attend(q0, nq, nk):
        rows = pl.ds(q0, nq)
        q = _rope(q_ref[0, rows, :], cos_ref[rows, :], sin_ref[rows, :]) * (DIFF_QK ** -0.5 * LOG2E)
        kk = ks_ref[0:nk, :]
        outs = []
        for hh in range(2):
            acc = None
            for m in range(2):
                lo = 64 * hh + DIFF_QK * m
                qm = jnp.where((lane >= lo) & (lane < lo + DIFF_QK), q, 0.0).astype(BF16)
                s = _nt(qm, kk)
                e = jnp.exp2((s - jnp.max(s, axis=-1, keepdims=True)).astype(BF16))
                num = jnp.dot(e, va_ref[hh, 0:nk, :], preferred_element_type=F32)
                den = jnp.sum(jnp.where(lane == ones_lane[hh], num, 0.0), axis=-1, keepdims=True)
                acc = num * (1.0 / den) if m == 0 else acc - num * (lam / den)
            outs.append(acc)
        o = jnp.where(lane < 64, outs[0], outs[1])
        sq = o * o
        s0 = jnp.sum(jnp.where(lane < 64, sq, 0.0), axis=-1, keepdims=True)
        s1 = jnp.sum(jnp.where(lane >= 64, sq, 0.0), axis=-1, keepdims=True)
        ms = jnp.where(lane < 64, s0, s1) * (1.0 / 64.0)
        o_ref[0, rows, :] = (o * lax.rsqrt(ms + EPS) * (g_ref[...] * post_scale)).astype(o_ref.dtype)

    @pl.when(t == 0)
    def _():
        attend(0, NCTX, NCTX)

    @pl.when(t > 0)
    def _():
        attend(pl.multiple_of(NCTX + (t - 1) * ATTN_TQ, ATTN_TQ // 2), ATTN_TQ, T)


ATTN_TQ = 512


def _attn_a_call(p, lam, cos, sin, g2, post_scale):
    full = lambda col: pl.BlockSpec((1, T, 128), lambda b, h, t: (b, 0, col + h))
    tab = pl.BlockSpec((T, 128), lambda b, h, t: (0, 0))
    return pl.pallas_call(
        functools.partial(_attn_a_kernel, post_scale=post_scale),
        grid=(NB, 4, 1 + SEQ // ATTN_TQ),
        in_specs=[pl.BlockSpec(memory_space=pltpu.SMEM), full(A_Q), full(A_K), full(A_V), tab, tab,
                  pl.BlockSpec((1, 128), lambda b, h, t: (0, 0))],
        out_specs=pl.BlockSpec((1, T, 128), lambda b, h, t: (b, 0, h)),
        out_shape=jax.ShapeDtypeStruct((NB, T, 512), BF16),
        scratch_shapes=[pltpu.VMEM((T, 128), BF16), pltpu.VMEM((2, T, 128), BF16)],
        compiler_params=_cparams(("parallel", "parallel", "arbitrary"), 48),
        name="diff_attn",
    )(lam, p, p, p, cos, sin, g2)


def _attn_na_kernel(q_ref, k_ref, v_ref, bias_ref, o_ref):
    g = pl.program_id(1)
    lane = lax.broadcasted_iota(jnp.int32, (1, 128), 1)

    def heads(p, fn):
        sl = slice(128 * p, 128 * p + 128)
        q = q_ref[0, :, sl] * (NA_DIM ** -0.5 * LOG2E)
        outs = [fn(2 * p + hh, jnp.where((lane >= 64 * hh) & (lane < 64 * hh + 64), q, 0.0).astype(BF16))
                for hh in range(2)]
        o_ref[0, :, sl] = jnp.where(lane < 64, outs[0], outs[1]).astype(o_ref.dtype)

    @pl.when(g == 0)
    def _():
        for p in range(4):
            sl = slice(128 * p, 128 * p + 128)
            kc = k_ref[0, 0:NCTX, sl].astype(BF16)
            vc = v_ref[0, 0:NCTX, sl].astype(BF16)

            def ctx_head(h, qm):
                s = _nt(qm, kc)
                e = jnp.exp2(s - jnp.max(s, axis=-1, keepdims=True))
                r = 1.0 / jnp.sum(e, axis=-1, keepdims=True)
                return jnp.dot(e.astype(BF16), vc, preferred_element_type=F32) * r

            heads(p, ctx_head)

    @pl.when(g > 0)
    def _():
        u0 = jnp.clip(NA_GROUP * (g - 1) - NA_WIN_R // 2, 0, SEQ // GRID_W - NA_KROWS)
        win = pl.ds(pl.multiple_of(NCTX + GRID_W * u0, GRID_W), NA_KROWS * GRID_W)
        for p in range(4):
            sl = slice(128 * p, 128 * p + 128)
            kc = k_ref[0, 0:NCTX, sl].astype(BF16)
            vc = v_ref[0, 0:NCTX, sl].astype(BF16)
            kw = k_ref[0, win, sl].astype(BF16)
            vw = v_ref[0, win, sl].astype(BF16)

            def lat_head(h, qm):
                sl_ = _nt(qm, kw) + bias_ref[h, 0]
                sc = _nt(qm, kc)
                mx = jnp.maximum(jnp.max(sl_, axis=-1, keepdims=True), jnp.max(sc, axis=-1, keepdims=True))
                el = jnp.exp2(sl_ - mx)
                ec = jnp.exp2(sc - mx)
                r = 1.0 / (jnp.sum(el, axis=-1, keepdims=True) + jnp.sum(ec, axis=-1, keepdims=True))
                o = (jnp.dot(el.astype(BF16), vw, preferred_element_type=F32)
                     + jnp.dot(ec.astype(BF16), vc, preferred_element_type=F32))
                return o * r

            heads(p, lat_head)


def _attn_na_call(p, bias):
    tq = NA_GROUP * GRID_W
    nk = NA_KROWS * GRID_W
    n_grp = SEQ // tq

    def bias_map(b, g):
        grp = jnp.maximum(g - 1, 0)
        return (0, jnp.where(grp == 0, 0, jnp.where(grp == n_grp - 1, 2, 1)), 0, 0)

    return pl.pallas_call(
        _attn_na_kernel,
        grid=(NB, T // tq),
        in_specs=[
            pl.BlockSpec((1, tq, 512), lambda b, g: (b, g, B_Q // 4)),
            pl.BlockSpec((1, T, 512), lambda b, g: (b, 0, B_K // 4)),
            pl.BlockSpec((1, T, 512), lambda b, g: (b, 0, B_V // 4)),
            pl.BlockSpec((8, 1, tq, nk), bias_map),
        ],
        out_specs=pl.BlockSpec((1, tq, 512), lambda b, g: (b, g, 0)),
        out_shape=jax.ShapeDtypeStruct((NB, T, 512), BF16),
        compiler_params=_cparams(("parallel", "arbitrary"), 48),
        name="nbr_attn",
    )(p, p, p, bias)


def _na_bias_table(rel_bias):
    rows = SEQ // GRID_W
    n_grp = rows // NA_GROUP
    cq = np.arange(GRID_W)
    col_start = np.clip(cq - NA_WIN_C // 2, 0, GRID_W - NA_WIN_C)
    col_ok = (cq[None, :] >= col_start[:, None]) & (cq[None, :] < col_start[:, None] + NA_WIN_C)
    dc = np.clip(cq[None, :] - cq[:, None], -(NA_WIN_C - 1), NA_WIN_C - 1) + (NA_WIN_C - 1)
    gi = np.arange(n_grp)[:, None, None]
    qr = np.arange(NA_GROUP)[None, :, None]
    ku = np.arange(NA_KROWS)[None, None, :]
    r = NA_GROUP * gi + qr
    u = np.clip(NA_GROUP * gi - NA_WIN_R // 2, 0, rows - NA_KROWS) + ku
    row_start = np.clip(r - NA_WIN_R // 2, 0, rows - NA_WIN_R)
    row_ok = (u >= row_start) & (u < row_start + NA_WIN_R)
    dr = np.where(row_ok, u - r + (NA_WIN_R - 1), -1)
    for g in range(2, n_grp - 1):
        assert np.array_equal(dr[g], dr[1])
    assert np.all((dc == cq[None, :] - cq[:, None] + NA_WIN_C - 1)[col_ok])
    rbp = jnp.pad(rel_bias.astype(F32) * LOG2E, ((0, 0), (0, 0), (GRID_W, GRID_W)))
    rbt = jnp.stack([rbp[:, :, GRID_W + NA_WIN_C - 1 - q:2 * GRID_W + NA_WIN_C - 1 - q] for q in range(GRID_W)],
                    axis=2)
    rbm = jnp.where(col_ok[None, None], rbt, -jnp.inf)
    ninf = jnp.full((rel_bias.shape[0], GRID_W, GRID_W), -jnp.inf, F32)
    pats = []
    for g in (0, 1, n_grp - 1):
        qrows = []
        for a in range(NA_GROUP):
            blocks = [rbm[:, dr[g, a, b]] if dr[g, a, b] >= 0 else ninf for b in range(NA_KROWS)]
            qrows.append(jnp.concatenate(blocks, axis=-1))
        pats.append(jnp.concatenate(qrows, axis=-2))
    return jnp.stack(pats, axis=1)


def _cumsum_rows(g, rev):
    r = lax.broadcasted_iota(jnp.int32, (CH, CH), 0)
    c = lax.broadcasted_iota(jnp.int32, (CH, CH), 1)
    tri = jnp.where((c >= r) if rev else (c <= r), 1.0, 0.0).astype(BF16)
    g1 = g.astype(BF16)
    r1 = g - g1.astype(F32)
    g2 = r1.astype(BF16)
    g3 = (r1 - g2.astype(F32)).astype(BF16)
    dot = lambda a: jnp.dot(tri, a, preferred_element_type=F32)
    return dot(g1) + dot(g2) + dot(g3)


def _block_rows(a, first, period):
    return jnp.concatenate(
        [jnp.broadcast_to(a[r:r + 1, :], (period, 128)) for r in range(first, CH, period)], axis=0)


def _gla_chunk(q, k, g, vs, sts, masks, rev):
    nh = len(vs)
    b = _cumsum_rows(g * LOG2E, rev)
    b_end = b[0:1] if rev else b[CH - 1:CH]
    r2 = lax.broadcasted_iota(jnp.int32, (CH, CH), 0)
    c2 = lax.broadcasted_iota(jnp.int32, (CH, CH), 1)

    def headq(a, h):
        return a if masks[h] is None else jnp.where(masks[h], a, 0.0)

    atts = [jnp.zeros((CH, CH), F32) for _ in range(nh)]
    n = CH // 2
    while n >= SUB:
        first = n if rev else n - 1
        ref = _block_rows(b, first, 2 * n) if 2 * n < CH else b[first:first + 1]
        qn = (q * jnp.exp2(b - ref)).astype(BF16)
        kn = (k * jnp.exp2(ref - b)).astype(BF16)
        sh = (2 * n).bit_length() - 1
        same = (r2 >> sh) == (c2 >> sh)
        rin = r2 & (2 * n - 1)
        cin = c2 & (2 * n - 1)
        valid = (same & (rin < n) & (cin >= n)) if rev else (same & (rin >= n) & (cin < n))
        for h in range(nh):
            a = _nt(headq(qn, h), kn)
            atts[h] = jnp.where(valid, a, atts[h])
        n //= 2

    sub_sh = SUB.bit_length() - 1
    dvalid = ((r2 >> sub_sh) == (c2 >> sub_sh)) & ((c2 >= r2) if rev else (c2 <= r2))
    diag = [jnp.zeros((CH, CH), F32) for _ in range(nh)]
    for s in range(SUB):
        e = q * _block_rows(k, s, SUB) * jnp.exp2(b - _block_rows(b, s, SUB))
        hit = (c2 & (SUB - 1)) == s
        for h in range(nh):
            rs = jnp.sum(headq(e, h), axis=-1, keepdims=True)
            diag[h] = jnp.where(hit, rs, diag[h])

    qe = (q * jnp.exp2(b)).astype(BF16)
    kd = k * jnp.exp2(b_end - b)
    dec = jnp.exp2(b_end)
    outs, new = [], []
    for h in range(nh):
        att = jnp.where(dvalid, diag[h], atts[h])
        vb = vs[h].astype(BF16)
        o = jnp.dot(att.astype(BF16), vb, preferred_element_type=F32) + _nt(qe, sts[h].astype(BF16))
        outs.append(o)
        new.append(sts[h] * dec + _tn(vb, headq(kd, h).astype(BF16)))
    return outs, new


SCAN_CHUNKS = 2
SCAN_ROWS = SCAN_CHUNKS * CH


def _chunk_maps(col_block):
    n_c = NCTX // SCAN_ROWS
    n_all = T // SCAN_ROWS

    def fwd(b, s):
        return (b, s, col_block)

    def bwd(b, s):
        return (b, jnp.where(s < n_c, n_c - 1 - s, n_all + n_c - 1 - s), col_block)

    return fwd, bwd


def _scan_order(rev):
    return tuple(reversed(range(SCAN_CHUNKS))) if rev else tuple(range(SCAN_CHUNKS))


def _hgrn_kernel(qf_ref, if_ref, ff_ref, qb_ref, ib_ref, fb_ref, lb_ref, of_ref, ob_ref, st_ref):
    @pl.when(pl.program_id(1) == 0)
    def _():
        st_ref[...] = jnp.zeros_like(st_ref)

    dirs = ((qf_ref, if_ref, ff_ref, of_ref), (qb_ref, ib_ref, fb_ref, ob_ref))
    for d, (q_ref, i_ref, f_ref, o_ref) in enumerate(dirs):
        for h in range(4):
            sl = slice(128 * h, 128 * h + 128)
            lb = lb_ref[d:d + 1, sl]
            sts = [st_ref[d, h]]
            for j in _scan_order(d == 1):
                rows = slice(CH * j, CH * j + CH)
                q = _silu(q_ref[0, rows, sl])
                f = f_ref[0, rows, sl]
                e = jnp.exp(-jnp.abs(f))
                inv = 1.0 / (1.0 + e)
                k = (1.0 - lb) * jnp.where(f >= 0.0, e * inv, inv)
                a = jnp.log(lb)
                c = jnp.log(1.0 - lb) + (jnp.minimum(f, 0.0) - jnp.log(1.0 + e))
                g = jnp.maximum(a, c) + jnp.log(1.0 + jnp.exp(-jnp.abs(a - c)))
                outs, sts = _gla_chunk(q, k, g, [i_ref[0, rows, sl]], sts, [None], d == 1)
                o_ref[0, rows, sl] = outs[0]
            st_ref[d, h] = sts[0]


def _hgrn_call(p, lb):
    fq, bq = _chunk_maps(C_Q)
    fi, bi = _chunk_maps(C_I)
    ff, _ = _chunk_maps(C_FF)
    _, bf = _chunk_maps(C_FB)
    fo, bo = _chunk_maps(0)
    blk = lambda m: pl.BlockSpec((1, SCAN_ROWS, 512), m)
    return pl.pallas_call(
        _hgrn_kernel,
        grid=(NB, T // SCAN_ROWS),
        in_specs=[blk(fq), blk(fi), blk(ff), blk(bq), blk(bi), blk(bf),
                  pl.BlockSpec((2, 512), lambda b, s: (0, 0))],
        out_specs=[blk(fo), blk(bo)],
        out_shape=[jax.ShapeDtypeStruct((NB, T, 512), F32)] * 2,
        scratch_shapes=[pltpu.VMEM((2, 4, 128, 128), F32)],
        compiler_params=_cparams(("parallel", "arbitrary"), 32),
        name="hgrn2_scan",
    )(p, p, p, p, p, p, lb)


def _gla_kernel(qkf_ref, vf_ref, zf_ref, qkb_ref, vb_ref, zb_ref, w2_ref, b2_ref, of_ref, ob_ref, st_ref):
    @pl.when(pl.program_id(1) == 0)
    def _():
        st_ref[...] = jnp.zeros_like(st_ref)

    lane = lax.broadcasted_iota(jnp.int32, (1, 128), 1)
    masks = [lane < GLA_DK, lane >= GLA_DK]
    dirs = ((qkf_ref, vf_ref, zf_ref, of_ref), (qkb_ref, vb_ref, zb_ref, ob_ref))
    for d, (qk_ref, v_ref, z_ref, o_ref) in enumerate(dirs):
        logit = jnp.dot(z_ref[0].astype(BF16), w2_ref[d].astype(BF16), preferred_element_type=F32) + b2_ref[d]
        g = _log_sigmoid(logit) * (1.0 / GLA_TAU)
        for grp in range(2):
            sl = slice(128 * grp, 128 * grp + 128)
            heads = (2 * grp, 2 * grp + 1)
            sts = [st_ref[d, h] for h in heads]
            for j in _scan_order(d == 1):
                rows = slice(CH * j, CH * j + CH)
                q = qk_ref[0, rows, sl] * (GLA_DK ** -0.5)
                k = qk_ref[0, rows, 256 + 128 * grp:256 + 128 * grp + 128]
                vs = [v_ref[0, rows, 128 * h:128 * h + 128] for h in heads]
                outs, sts = _gla_chunk(q, k, g[rows, sl], vs, sts, masks, d == 1)
                for o, h in zip(outs, heads):
                    o_ref[0, rows, 128 * h:128 * h + 128] = o
            for st, h in zip(sts, heads):
                st_ref[d, h] = st


def _gla_call(p, pt, w2p, b2):
    fqk, bqk = _chunk_maps(D_QK)
    fv, bv = _chunk_maps(D_V)
    fz, bz = _chunk_maps(TAIL_Z)
    fo, bo = _chunk_maps(0)
    blk = lambda m: pl.BlockSpec((1, SCAN_ROWS, 512), m)
    zblk = lambda m: pl.BlockSpec((1, SCAN_ROWS, 128), m)
    return pl.pallas_call(
        _gla_kernel,
        grid=(NB, T // SCAN_ROWS),
        in_specs=[blk(fqk), blk(fv), zblk(fz), blk(bqk), blk(bv), zblk(bz),
                  pl.BlockSpec((2, 128, 256), lambda b, s: (0, 0, 0)),
                  pl.BlockSpec((2, 1, 256), lambda b, s: (0, 0, 0))],
        out_specs=[blk(fo), blk(bo)],
        out_shape=[jax.ShapeDtypeStruct((NB, T, 512), F32)] * 2,
        scratch_shapes=[pltpu.VMEM((2, 4, 128, 128), F32)],
        compiler_params=_cparams(("parallel", "arbitrary"), 32),
        name="gla_scan",
    )(p, p, pt, p, p, pt, w2p, b2)


def _gated_norm(o, gate, g):
    parts = []
    for h in range(4):
        sl = slice(128 * h, 128 * h + 128)
        oh = o[:, sl]
        ms = jnp.mean(oh * oh, axis=-1, keepdims=True)
        parts.append((oh * lax.rsqrt(ms + EPS) * g * _silu(gate[:, sl])).astype(BF16))
    return parts


def _cdpost_kernel(cf_ref, cb_ref, cg_ref, df_ref, db_ref, dr_ref, gc_ref, gd_ref, o_ref):
    for h, part in enumerate(_gated_norm(cf_ref[0] + cb_ref[0], cg_ref[0], gc_ref[...])):
        o_ref[0, :, 128 * h:128 * h + 128] = part
    for h, part in enumerate(_gated_norm(df_ref[0] + db_ref[0], dr_ref[0], gd_ref[...])):
        o_ref[0, :, 512 + 128 * h:512 + 128 * h + 128] = part


def _cdpost_call(cf, cb, df, db, p, pt, gc, gd):
    tm = 256
    row = lambda col: pl.BlockSpec((1, tm, 512), lambda b, i: (b, i, col))
    vec = pl.BlockSpec((1, 128), lambda b, i: (0, 0))
    return pl.pallas_call(
        _cdpost_kernel,
        grid=(NB, T // tm),
        in_specs=[row(0), row(0), row(C_G), row(0), row(0), row(TAIL_R), vec, vec],
        out_specs=pl.BlockSpec((1, tm, 1024), lambda b, i: (b, i, 0)),
        out_shape=jax.ShapeDtypeStruct((NB, T, 1024), BF16),
        compiler_params=_cparams(("parallel", "parallel"), 32),
        name="scan_post",
    )(cf, cb, p, df, db, pt, gc, gd)


def _ffn_kernel(h_ref, w1_ref, w3_ref, w2_ref, o_ref):
    @pl.when(pl.program_id(2) == 0)
    def _():
        o_ref[...] = jnp.zeros_like(o_ref)

    h = h_ref[0]
    u = jnp.dot(h, w1_ref[...].astype(BF16), preferred_element_type=F32)
    v = jnp.dot(h, w3_ref[...].astype(BF16), preferred_element_type=F32)
    o_ref[0] += jnp.dot((_silu(u) * v).astype(BF16), w2_ref[...].astype(BF16), preferred_element_type=F32)


def _ffn_call(h, w1, w3, w2):
    tm, tf = 1152, 256
    return pl.pallas_call(
        _ffn_kernel,
        grid=(NB, T // tm, FFN_DENSE // tf),
        in_specs=[
            pl.BlockSpec((1, tm, D), lambda b, i, f: (b, i, 0)),
            pl.BlockSpec((D, tf), lambda b, i, f: (0, f)),
            pl.BlockSpec((D, tf), lambda b, i, f: (0, f)),
            pl.BlockSpec((tf, D), lambda b, i, f: (f, 0)),
        ],
        out_specs=pl.BlockSpec((1, tm, D), lambda b, i, f: (b, i, 0)),
        out_shape=jax.ShapeDtypeStruct((NB, T, D), F32),
        compiler_params=_cparams(("parallel", "parallel", "arbitrary"), 56),
        name="dense_ffn",
    )(h, w1, w3, w2)


MOE_TM = 1024
MOE_SUB = 256
MOE_TF = 512
MOE_ROWS = 2 * NB * SEQ + N_EXPERTS * MOE_TM
MOE_TILES = MOE_ROWS // MOE_TM
MOE_PREFETCH_STEPS = 8
MOE_PREFETCH_ROWS = MOE_TM // MOE_PREFETCH_STEPS


def _route_kernel(x_ref, g_ref, sh_ref, sc_ref, wr_ref, h_ref, r_ref):
    x = x_ref[0]
    ms = jnp.mean(x * x, axis=-1, keepdims=True)
    h = x * lax.rsqrt(ms + EPS) * g_ref[...] * (1.0 + sc_ref[0]) + sh_ref[0]
    h_ref[0] = h
    logits = jnp.dot(h, wr_ref[...], preferred_element_type=F32, precision=lax.Precision.HIGHEST)
    lane = lax.broadcasted_iota(jnp.int32, logits.shape, 1)
    lanef = lane.astype(F32)
    lg = jnp.where(lane < N_EXPERTS, logits, -jnp.inf)
    m1 = jnp.max(lg, axis=-1, keepdims=True)
    i1 = jnp.min(jnp.where(lg == m1, lanef, 128.0), axis=-1, keepdims=True)
    lg2 = jnp.where(lanef == i1, -jnp.inf, lg)
    m2 = jnp.max(lg2, axis=-1, keepdims=True)
    i2 = jnp.min(jnp.where(lg2 == m2, lanef, 128.0), axis=-1, keepdims=True)
    e = jnp.exp(m2 - m1)
    w1 = 1.0 / (1.0 + e)
    w2 = e * w1
    r_ref[0] = jnp.where(lane == 0, i1, jnp.where(lane == 1, i2, jnp.where(lane == 2, w1,
                         jnp.where(lane == 3, w2, 0.0))))


def _route_call(xall, g, mods, wr):
    tm = 256
    return pl.pallas_call(
        _route_kernel,
        grid=(NB, SEQ // tm),
        in_specs=[
            pl.BlockSpec((1, tm, D), lambda b, i: (b, i + NCTX // tm, 0)),
            pl.BlockSpec((1, D), lambda b, i: (0, 0)),
            _mod_spec(3, False), _mod_spec(4, False),
            pl.BlockSpec((D, 128), lambda b, i: (0, 0)),
        ],
        out_specs=[pl.BlockSpec((1, tm, D), lambda b, i: (b, i, 0)),
                   pl.BlockSpec((1, tm, 128), lambda b, i: (b, i, 0))],
        out_shape=[jax.ShapeDtypeStruct((NB, SEQ, D), F32), jax.ShapeDtypeStruct((NB, SEQ, 128), F32)],
        compiler_params=_cparams(("parallel", "parallel"), 32),
        name="moe_route",
    )(xall, g, mods, mods, wr)


def _moe_kernel(te_ref, nl_ref, src_ref, h_ref, w1_ref, w3_ref, w2_ref, o_ref, xg_ref, xb_ref, sem):
    i = pl.program_id(0)
    f = pl.program_id(1)
    n_live = nl_ref[i]
    nxt_live = nl_ref[jnp.minimum(i + 1, MOE_TILES - 1)]
    fetch = (i + 1 < MOE_TILES) & (nxt_live > 0) & (f >= 1) & (f <= MOE_PREFETCH_STEPS)

    def row_copy(tile, r):
        row = src_ref[tile * MOE_TM + r]
        return pltpu.make_async_copy(h_ref.at[pl.ds(row, 1), :], xg_ref.at[pl.ds(r, 1), :], sem)

    @pl.when(f == 0)
    def _():
        o_ref[...] = jnp.zeros_like(o_ref)

    @pl.when((i == 0) & (f == 0) & (n_live > 0))
    def _():
        def body(r, c):
            row_copy(0, r).start()
            return c

        lax.fori_loop(0, MOE_TM, body, 0)

    @pl.when((f == 0) & (n_live > 0))
    def _():
        pltpu.make_async_copy(h_ref.at[pl.ds(0, MOE_TM), :], xg_ref, sem).wait()
        xb_ref[...] = xg_ref[...].astype(BF16)

    def step(n, do_fetch):
        def body():
            if do_fetch:
                r0 = (f - 1) * MOE_PREFETCH_ROWS
                for r in range(MOE_PREFETCH_ROWS):
                    row_copy(i + 1, r0 + r).start()
            rows = slice(0, n * MOE_SUB)
            h = xb_ref[rows, :]
            u = jnp.dot(h, w1_ref[0].astype(BF16), preferred_element_type=F32)
            v = jnp.dot(h, w3_ref[0].astype(BF16), preferred_element_type=F32)
            o_ref[rows, :] += jnp.dot((_silu(u) * v).astype(BF16), w2_ref[0].astype(BF16),
                                      preferred_element_type=F32)

        return body

    for n in range(1, MOE_TM // MOE_SUB + 1):
        for do_fetch in (False, True):
            pl.when((n_live == n) & (fetch == do_fetch))(step(n, do_fetch))


def _moe_call(tile_expert, tile_live, src_rows, h, w1, w3, w2):
    nf = FFN_EXPERT // MOE_TF

    def fidx(i, f, nl):
        return jnp.where(nl[i] > 0, f, nf - 1)

    return pl.pallas_call(
        _moe_kernel,
        grid_spec=pltpu.PrefetchScalarGridSpec(
            num_scalar_prefetch=3,
            grid=(MOE_TILES, nf),
            in_specs=[
                pl.BlockSpec(memory_space=pl.ANY),
                pl.BlockSpec((1, D, MOE_TF), lambda i, f, te, nl, src: (te[i], 0, fidx(i, f, nl))),
                pl.BlockSpec((1, D, MOE_TF), lambda i, f, te, nl, src: (te[i], 0, fidx(i, f, nl))),
                pl.BlockSpec((1, MOE_TF, D), lambda i, f, te, nl, src: (te[i], fidx(i, f, nl), 0)),
            ],
            out_specs=pl.BlockSpec((MOE_TM, D), lambda i, f, te, nl, src: (i, 0), pipeline_mode=pl.Buffered(1)),
            scratch_shapes=[pltpu.VMEM((MOE_TM, D), F32), pltpu.VMEM((MOE_TM, D), BF16),
                            pltpu.SemaphoreType.DMA(())],
        ),
        out_shape=jax.ShapeDtypeStruct((MOE_ROWS, D), F32),
        compiler_params=_cparams(("arbitrary", "arbitrary"), 58),
        name="moe_experts",
    )(tile_expert, tile_live, src_rows, h, w1, w3, w2)


def _row_copy(src_hbm, row, dst_ref, r, sem):
    return pltpu.make_async_copy(src_hbm.at[pl.ds(row, 1), :], dst_ref.at[pl.ds(r, 1), :], sem)


def _combine_kernel(dest_ref, x_ref, r_ref, gm_ref, gf_ref, y_ref, o_ref, y0_ref, y1_ref, sem, *, tc):
    base = (pl.program_id(0) * (SEQ // tc) + pl.program_id(1)) * tc

    def issue(r, c):
        _row_copy(y_ref, dest_ref[2 * (base + r)], y0_ref, r, sem).start()
        _row_copy(y_ref, dest_ref[2 * (base + r) + 1], y1_ref, r, sem).start()
        return c

    lax.fori_loop(0, tc, issue, 0, unroll=8)
    pltpu.make_async_copy(y_ref.at[pl.ds(0, tc), :], y0_ref, sem).wait()
    pltpu.make_async_copy(y_ref.at[pl.ds(0, tc), :], y1_ref, sem).wait()
    rt = r_ref[0]
    moe = rt[:, 2:3] * y0_ref[...] + rt[:, 3:4] * y1_ref[...]
    x = x_ref[0] + gm_ref[0] * moe
    ms = jnp.mean(x * x, axis=-1, keepdims=True)
    o_ref[0] = x * lax.rsqrt(ms + EPS) * gf_ref[...]


def _combine_call(dest, xall, route, mods, gfinal, y):
    tc = 256
    return pl.pallas_call(
        functools.partial(_combine_kernel, tc=tc),
        grid_spec=pltpu.PrefetchScalarGridSpec(
            num_scalar_prefetch=1,
            grid=(NB, SEQ // tc),
            in_specs=[
                pl.BlockSpec((1, tc, D), lambda b, i, d: (b, i + NCTX // tc, 0)),
                pl.BlockSpec((1, tc, 128), lambda b, i, d: (b, i, 0)),
                pl.BlockSpec((1, 1, D), lambda b, i, d: (b * 6 + 5, 0, 0)),
                pl.BlockSpec((1, D), lambda b, i, d: (0, 0)),
                pl.BlockSpec(memory_space=pl.ANY),
            ],
            out_specs=pl.BlockSpec((1, tc, D), lambda b, i, d: (b, i, 0)),
            scratch_shapes=[pltpu.VMEM((tc, D), F32), pltpu.VMEM((tc, D), F32), pltpu.SemaphoreType.DMA(())],
        ),
        out_shape=jax.ShapeDtypeStruct((NB, SEQ, D), F32),
        compiler_params=_cparams(("arbitrary", "arbitrary"), 32),
        name="moe_combine",
    )(dest, xall, route, mods, gfinal, y)


def _moe_plan(route):
    n_pairs = 2 * NB * SEQ
    e = route[..., 0:2].astype(jnp.int32).reshape(n_pairs)
    onehot = (e[:, None] == jnp.arange(N_EXPERTS, dtype=jnp.int32)[None, :]).astype(jnp.int32)
    csum = jnp.cumsum(onehot, axis=0)
    rank = jnp.sum(onehot * csum, axis=1) - 1
    counts = csum[-1]
    padded = ((counts + MOE_TM - 1) // MOE_TM) * MOE_TM
    ends = jnp.cumsum(padded)
    starts = ends - padded
    dest = (starts[e] + rank).astype(jnp.int32)
    src_rows = jnp.zeros((MOE_ROWS,), jnp.int32).at[dest].set(jnp.arange(n_pairs, dtype=jnp.int32) // 2)
    tile_start = jnp.arange(MOE_TILES, dtype=jnp.int32) * MOE_TM
    t_eff = jnp.minimum(tile_start, ends[-1] - MOE_TM)
    tile_expert = jnp.sum((ends[None, :] <= t_eff[:, None]).astype(jnp.int32), axis=1)
    rows_left = (starts + counts)[tile_expert] - tile_start
    tile_live = jnp.clip((rows_left + MOE_SUB - 1) // MOE_SUB, 0, MOE_TM // MOE_SUB).astype(jnp.int32)
    return dest, src_rows, tile_expert.astype(jnp.int32), tile_live


def _rope_tables():
    per_axis = DIFF_QK // 4
    t = np.arange(SEQ)
    inv = ROPE_BASE ** (-np.arange(per_axis, dtype=np.float32) / per_axis)
    ang = np.concatenate([(t // GRID_W).astype(np.float32)[:, None] * inv,
                          (t % GRID_W).astype(np.float32)[:, None] * inv], axis=-1).astype(np.float32)
    cos = np.concatenate([np.ones((NCTX, 2 * per_axis), np.float32), np.cos(ang)], axis=0)
    sin = np.concatenate([np.zeros((NCTX, 2 * per_axis), np.float32), np.sin(ang)], axis=0)
    reps = 128 // (2 * per_axis)
    return jnp.asarray(np.tile(cos, (1, reps))), jnp.asarray(np.tile(sin, (1, reps)))


def _in_proj_tail(w_t):
    z = w_t[IN_MAIN:IN_MAIN + 2 * GLA_RANK]
    r = w_t[IN_MAIN + 2 * GLA_RANK:]
    return jnp.concatenate([r, z, jnp.zeros((TAIL_W - r.shape[0] - z.shape[0], D), w_t.dtype)], axis=0)


def kernel(x, c, ctx, c_ctx, ada_w, ada_b, norm_mix, norm_ffn, w_in, w_out, diff_lambda, diff_norm,
           na_rel_bias, hgrn_lower_bounds, hgrn_norm, gla_gate_w2, gla_gate_b, gla_norm,
           ffn_w1, ffn_w3, ffn_w2, moe_router, moe_w1, moe_w3, moe_w2, final_norm):
    lb_soft = jax.nn.softmax(hgrn_lower_bounds.astype(F32), axis=1)
    lower_bounds = jnp.clip(jnp.cumsum(lb_soft, axis=1) - lb_soft[:, :1], 0.0, 1.0 - 1e-6)
    cond8 = jnp.concatenate([c, c_ctx[None, :], jnp.zeros((8 - NB - 1, D), F32)], axis=0)
    mods_all = _ada_call(cond8, ada_w, ada_b).reshape(DEPTH, 8 * 6, 1, D)
    cos, sin = _rope_tables()
    xall = jnp.concatenate([ctx, x], axis=1)
    w_in_t = jnp.swapaxes(w_in, 1, 2)

    out = None
    pending = None
    for l in range(DEPTH):
        mods = mods_all[l]
        if pending is None:
            h = _norm_call(xall, norm_mix[l][None, :], mods, 0, 1)
        else:
            xall, h = _norm_call(xall, norm_mix[l][None, :], mods, 0, 1, resid=pending)
            pending = None
        p = _wsmm_call([h], w_in_t, l, 1664, IN_MAIN // 1664, w_t=True, vmem_mb=56, name="in_proj")
        pt = _wsmm_call([h], _in_proj_tail(w_in_t[l])[None], 0, TAIL_W, 1, w_t=True, vmem_mb=32,
                        name="in_proj_tail")

        lambda_init = 0.8 - 0.6 * math.exp(-0.3 * l)
        lp = diff_lambda[l].astype(F32)
        lam = (jnp.exp(jnp.sum(lp[0] * lp[1])) - jnp.exp(jnp.sum(lp[2] * lp[3])) + lambda_init).reshape(1)
        a = _attn_a_call(p, lam, cos, sin, jnp.tile(diff_norm[l], 2)[None, :], 1.0 - lambda_init)
        n = _attn_na_call(p, _na_bias_table(na_rel_bias[l]))
        cf, cb = _hgrn_call(p, lower_bounds[:, l])
        w2p = jnp.zeros((2, 128, 4 * GLA_DK), F32)
        w2p = w2p.at[0, 0:GLA_RANK].set(gla_gate_w2[l, 0]).at[1, GLA_RANK:2 * GLA_RANK].set(gla_gate_w2[l, 1])
        df, db = _gla_call(p, pt, w2p, gla_gate_b[l][:, None, :])
        cd = _cdpost_call(cf, cb, df, db, p, pt, hgrn_norm[l][None, :], gla_norm[l][None, :])
        xall = _wsmm_call([a, n, cd], w_out, l, 1024, D // 1024, resid=(xall, mods, 2), name="out_proj")

        if l % 2 == 0:
            h2 = _norm_call(xall, norm_ffn[l][None, :], mods, 3, 4)
            y = _ffn_call(h2, ffn_w1[l // 2], ffn_w3[l // 2], ffn_w2[l // 2])
            pending = (y, mods, 5)
        else:
            assert l == DEPTH - 1
            wr = jnp.zeros((D, 128), F32).at[:, :N_EXPERTS].set(moe_router[l // 2])
            hr, route = _route_call(xall, norm_ffn[l][None, :], mods, wr)
            dest, src_rows, tile_expert, tile_live = _moe_plan(route)
            y = _moe_call(tile_expert, tile_live, src_rows, hr.reshape(NB * SEQ, D),
                          moe_w1[l // 2], moe_w3[l // 2], moe_w2[l // 2])
            out = _combine_call(dest, xall, route, mods, final_norm[None, :], y)
    return out
```

```python
import functools
import math

import numpy as np
import jax
import jax.numpy as jnp
from jax import lax
from jax.experimental import pallas as pl
from jax.experimental.pallas import tpu as pltpu

F32 = jnp.float32
BF16 = jnp.bfloat16

D = 2048
NB = 4
SEQ = 2048
NCTX = 256
T = NCTX + SEQ
DEPTH = 2
GRID_W = 64
ROPE_BASE = 10000.0
EPS = 1e-6
LOG2E = math.log2(math.e)

DIFF_QK = 32
NA_DIM = 64
NA_WIN_R = 8
NA_WIN_C = 16
NA_GROUP = 4
NA_KROWS = NA_GROUP + NA_WIN_R - 1
GLA_DK = 64
GLA_TAU = 16.0
GLA_RANK = 16
CH = 64
SUB = 8

FFN_DENSE = 5632
N_EXPERTS = 8
FFN_EXPERT = 7168

IN_MAIN = 6656
A_Q, A_K, A_V = 0, 4, 8
B_Q, B_K, B_V = 12, 16, 20
C_Q, C_I, C_FF, C_FB, C_G = 6, 7, 8, 9, 10
D_QK, D_V = 11, 12
TAIL_W = 640
TAIL_R, TAIL_Z = 0, 4

V7X_VMEM_BYTES = 64 * 1024 * 1024


def _cparams(sem, vmem_mb):
    assert vmem_mb * 1024 * 1024 < V7X_VMEM_BYTES
    return pltpu.CompilerParams(dimension_semantics=sem, vmem_limit_bytes=vmem_mb * 1024 * 1024)


def _sigmoid(x):
    return 1.0 / (1.0 + jnp.exp(-x))


def _silu(x):
    return x * _sigmoid(x)


def _log_sigmoid(x):
    return jnp.minimum(x, 0.0) - jnp.log(1.0 + jnp.exp(-jnp.abs(x)))


def _nt(a, b):
    return lax.dot_general(a, b, (((1,), (1,)), ((), ())), preferred_element_type=F32)


def _tn(a, b):
    return lax.dot_general(a, b, (((0,), (0,)), ((), ())), preferred_element_type=F32)


def _interleave(gens):
    live = list(gens)
    while live:
        still = []
        for gen in live:
            try:
                next(gen)
                still.append(gen)
            except StopIteration:
                pass
        live = still


def _ada_kernel(c_ref, w_ref, b_ref, o_ref):
    s = _silu(c_ref[...]).astype(BF16)
    o_ref[0] = jnp.dot(s, w_ref[0].astype(BF16), preferred_element_type=F32) + b_ref[0]


def _ada_call(cond8, ada_w, ada_b):
    tn = 1536
    n = 6 * D
    return pl.pallas_call(
        _ada_kernel,
        grid=(DEPTH, n // tn),
        in_specs=[
            pl.BlockSpec((8, D), lambda l, j: (0, 0)),
            pl.BlockSpec((1, D, tn), lambda l, j: (l, 0, j)),
            pl.BlockSpec((1, 1, tn), lambda l, j: (l, 0, j)),
        ],
        out_specs=pl.BlockSpec((1, 8, tn), lambda l, j: (l, 0, j)),
        out_shape=jax.ShapeDtypeStruct((DEPTH, 8, n), F32),
        compiler_params=_cparams(("parallel", "parallel"), 40),
        name="ada_mod",
    )(cond8, ada_w, ada_b.reshape(DEPTH, 1, n))


def _mod_spec(k, ctx):
    if ctx:
        return pl.BlockSpec((1, 1, D), lambda b, i: (NB * 6 + k, 0, 0))
    return pl.BlockSpec((1, 1, D), lambda b, i: (b * 6 + k, 0, 0))


def _row_is_ctx(row0, tm):
    return (row0 + lax.broadcasted_iota(jnp.int32, (tm, 1), 0)) < NCTX


def _modnorm(x, g, is_c, shl, scl, shc, scc):
    ms = jnp.mean(x * x, axis=-1, keepdims=True)
    y = x * lax.rsqrt(ms + EPS) * g
    sc = jnp.where(is_c, scc, scl)
    sh = jnp.where(is_c, shc, shl)
    return y * (1.0 + sc) + sh


def _norm_kernel(*refs, tm, resid):
    if resid:
        x_ref, y_ref, gl_ref, gx_ref, g_ref, shl_ref, scl_ref, shc_ref, scc_ref, xo_ref, h_ref = refs
    else:
        x_ref, g_ref, shl_ref, scl_ref, shc_ref, scc_ref, h_ref = refs
    is_c = _row_is_ctx(pl.program_id(1) * tm, tm)
    x = x_ref[0]
    if resid:
        x = x + jnp.where(is_c, gx_ref[0], gl_ref[0]) * y_ref[0]
        xo_ref[0] = x
    h_ref[0] = _modnorm(x, g_ref[...], is_c, shl_ref[0], scl_ref[0], shc_ref[0], scc_ref[0]).astype(BF16)


def _norm_call(xall, g, mods, k_shift, k_scale, resid=None):
    tm = 384
    row = pl.BlockSpec((1, tm, D), lambda b, i: (b, i, 0))
    in_specs, args = [row], [xall]
    if resid is not None:
        y, mods_prev, k_gate = resid
        in_specs += [row, _mod_spec(k_gate, False), _mod_spec(k_gate, True)]
        args += [y, mods_prev, mods_prev]
    in_specs += [pl.BlockSpec((1, D), lambda b, i: (0, 0)),
                 _mod_spec(k_shift, False), _mod_spec(k_scale, False),
                 _mod_spec(k_shift, True), _mod_spec(k_scale, True)]
    args += [g, mods, mods, mods, mods]
    h_shape = jax.ShapeDtypeStruct((NB, T, D), BF16)
    out_shape, out_specs = h_shape, row
    if resid is not None:
        out_shape, out_specs = [jax.ShapeDtypeStruct((NB, T, D), F32), h_shape], [row, row]
    return pl.pallas_call(
        functools.partial(_norm_kernel, tm=tm, resid=resid is not None),
        grid=(NB, T // tm),
        in_specs=in_specs, out_specs=out_specs, out_shape=out_shape,
        compiler_params=_cparams(("parallel", "parallel"), 32),
        name="mod_norm",
    )(*args)


def _wsmm_kernel(*refs, ksizes, tm, epilogue, w_t):
    n_in = len(ksizes)
    ins, w_ref = refs[:n_in], refs[n_in]
    o_ref, wb_ref = refs[-2], refs[-1]

    @pl.when((pl.program_id(1) == 0) & (pl.program_id(2) == 0))
    def _():
        wb_ref[...] = w_ref[0].astype(BF16)

    acc, k0 = None, 0
    for r, ks in zip(ins, ksizes):
        if w_t:
            part = _nt(r[0], wb_ref[:, k0:k0 + ks])
        else:
            part = jnp.dot(r[0], wb_ref[k0:k0 + ks, :], preferred_element_type=F32)
        acc = part if acc is None else acc + part
        k0 += ks
    if epilogue:
        x_ref, gl_ref, gx_ref = refs[n_in + 1:n_in + 4]
        gate = jnp.where(_row_is_ctx(pl.program_id(2) * tm, tm), gx_ref[0], gl_ref[0])
        acc = x_ref[0] + gate * acc
    o_ref[0] = acc


def _wsmm_call(acts, w, layer, tn, n_j, resid=None, w_t=False, vmem_mb=48, name="matmul"):
    tm = 768
    ksizes = tuple(a.shape[-1] for a in acts)
    kdim = sum(ksizes)
    in_specs = [pl.BlockSpec((1, tm, ks), lambda j, b, i: (b, i, 0)) for ks in ksizes]
    if w_t:
        in_specs.append(pl.BlockSpec((1, tn, kdim), lambda j, b, i: (layer, j, 0)))
    else:
        in_specs.append(pl.BlockSpec((1, kdim, tn), lambda j, b, i: (layer, 0, j)))
    args = list(acts) + [w]
    if resid is not None:
        x, mods, k_gate = resid
        in_specs += [pl.BlockSpec((1, tm, tn), lambda j, b, i: (b, i, j)),
                     pl.BlockSpec((1, 1, tn), lambda j, b, i: (b * 6 + k_gate, 0, j)),
                     pl.BlockSpec((1, 1, tn), lambda j, b, i: (NB * 6 + k_gate, 0, j))]
        args += [x, mods, mods]
    return pl.pallas_call(
        functools.partial(_wsmm_kernel, ksizes=ksizes, tm=tm, epilogue=resid is not None, w_t=w_t),
        grid=(n_j, NB, T // tm),
        in_specs=in_specs,
        out_specs=pl.BlockSpec((1, tm, tn), lambda j, b, i: (b, i, j)),
        out_shape=jax.ShapeDtypeStruct((NB, T, n_j * tn), F32),
        scratch_shapes=[pltpu.VMEM((tn, kdim) if w_t else (kdim, tn), BF16)],
        compiler_params=_cparams(("arbitrary", "arbitrary", "arbitrary"), vmem_mb),
        name=name,
    )(*args)


def _rope(x, cos, sin):
    lane = lax.broadcasted_iota(jnp.int32, x.shape, 1)
    first = (lane & (DIFF_QK - 1)) < (DIFF_QK // 2)
    rot = jnp.where(first, -pltpu.roll(x, 128 - DIFF_QK // 2, 1), pltpu.roll(x, DIFF_QK // 2, 1))
    return x * cos + rot * sin


def _attn_a_kernel(lam_ref, q_ref, k_ref, v_ref, cos_ref, sin_ref, g_ref, o_ref, ks_ref, va_ref, *, post_scale):
    t = pl.program_id(2)
    lane = lax.broadcasted_iota(jnp.int32, (1, 128), 1)
    ones_lane = (64, 0)

    @pl.when(t == 0)
    def _():
        ks_ref[...] = _rope(k_ref[0], cos_ref[...], sin_ref[...]).astype(BF16)
        v = v_ref[0]
        va_ref[0] = jnp.where(lane < 64, v, jnp.where(lane == ones_lane[0], 1.0, 0.0)).astype(BF16)
        va_ref[1] = jnp.where(lane >= 64, v, jnp.where(lane == ones_lane[1], 1.0, 0.0)).astype(BF16)

    lam = lam_ref[0]

    def attend(q0, nq, nk):
        rows = pl.ds(q0, nq)
        q = _rope(q_ref[0, rows, :], cos_ref[rows, :], sin_ref[rows, :]) * (DIFF_QK ** -0.5 * LOG2E)
        kk = ks_ref[0:nk, :]
        terms = {}

        def softmax_pv(hh, m):
            lo = 64 * hh + DIFF_QK * m
            qm = jnp.where((lane >= lo) & (lane < lo + DIFF_QK), q, 0.0).astype(BF16)
            s = _nt(qm, kk)
            yield
            mx = jnp.max(s, axis=-1, keepdims=True)
            yield
            e = jnp.exp2((s - mx).astype(BF16))
            yield
            num = jnp.dot(e, va_ref[hh, 0:nk, :], preferred_element_type=F32)
            yield
            den = jnp.sum(jnp.where(lane == ones_lane[hh], num, 0.0), axis=-1, keepdims=True)
            terms[hh, m] = num * ((1.0 if m == 0 else lam) / den)

        _interleave([softmax_pv(hh, m) for hh in range(2) for m in range(2)])
        outs = [terms[hh, 0] - terms[hh, 1] for hh in range(2)]
        o = jnp.where(lane < 64, outs[0], outs[1])
        sq = o * o
        s0 = jnp.sum(jnp.where(lane < 64, sq, 0.0), axis=-1, keepdims=True)
        s1 = jnp.sum(jnp.where(lane >= 64, sq, 0.0), axis=-1, keepdims=True)
        ms = jnp.where(lane < 64, s0, s1) * (1.0 / 64.0)
        o_ref[0, rows, :] = (o * lax.rsqrt(ms + EPS) * (g_ref[...] * post_scale)).astype(o_ref.dtype)

    @pl.when(t == 0)
    def _():
        attend(0, NCTX, NCTX)

    @pl.when(t > 0)
    def _():
        attend(pl.multiple_of(NCTX + (t - 1) * ATTN_TQ, ATTN_TQ // 2), ATTN_TQ, T)


ATTN_TQ = 512


def _attn_a_call(p, lam, cos, sin, g2, post_scale):
    full = lambda col: pl.BlockSpec((1, T, 128), lambda b, h, t: (b, 0, col + h))
    tab = pl.BlockSpec((T, 128), lambda b, h, t: (0, 0))
    return pl.pallas_call(
        functools.partial(_attn_a_kernel, post_scale=post_scale),
        grid=(NB, 4, 1 + SEQ // ATTN_TQ),
        in_specs=[pl.BlockSpec(memory_space=pltpu.SMEM), full(A_Q), full(A_K), full(A_V), tab, tab,
                  pl.BlockSpec((1, 128), lambda b, h, t: (0, 0))],
        out_specs=pl.BlockSpec((1, T, 128), lambda b, h, t: (b, 0, h)),
        out_shape=jax.ShapeDtypeStruct((NB, T, 512), BF16),
        scratch_shapes=[pltpu.VMEM((T, 128), BF16), pltpu.VMEM((2, T, 128), BF16)],
        compiler_params=_cparams(("parallel", "parallel", "arbitrary"), 48),
        name="diff_attn",
    )(lam, p, p, p, cos, sin, g2)


def _attn_na_kernel(q_ref, k_ref, v_ref, bias_ref, o_ref):
    g = pl.program_id(1)
    lane = lax.broadcasted_iota(jnp.int32, (1, 128), 1)

    def run(chain):
        results = {}
        gens = []
        for p in range(4):
            sl = slice(128 * p, 128 * p + 128)
            q = q_ref[0, :, sl] * (NA_DIM ** -0.5 * LOG2E)
            for hh in range(2):
                qm = jnp.where((lane >= 64 * hh) & (lane < 64 * hh + 64), q, 0.0).astype(BF16)
                gens.append(chain(2 * p + hh, qm, sl, results))
        _interleave(gens)
        for p in range(4):
            o_ref[0, :, 128 * p:128 * p + 128] = jnp.where(lane < 64, results[2 * p], results[2 * p + 1]
                                                           ).astype(o_ref.dtype)

    @pl.when(g == 0)
    def _():
        def ctx_head(h, qm, sl, results):
            s = _nt(qm, k_ref[0, 0:NCTX, sl].astype(BF16))
            yield
            e = jnp.exp2(s - jnp.max(s, axis=-1, keepdims=True))
            yield
            r = 1.0 / jnp.sum(e, axis=-1, keepdims=True)
            results[h] = jnp.dot(e.astype(BF16), v_ref[0, 0:NCTX, sl].astype(BF16), preferred_element_type=F32) * r

        run(ctx_head)

    @pl.when(g > 0)
    def _():
        u0 = jnp.clip(NA_GROUP * (g - 1) - NA_WIN_R // 2, 0, SEQ // GRID_W - NA_KROWS)
        win = pl.ds(pl.multiple_of(NCTX + GRID_W * u0, GRID_W), NA_KROWS * GRID_W)

        def lat_head(h, qm, sl, results):
            sw = _nt(qm, k_ref[0, win, sl].astype(BF16)) + bias_ref[h, 0]
            sc = _nt(qm, k_ref[0, 0:NCTX, sl].astype(BF16))
            yield
            mx = jnp.maximum(jnp.max(sw, axis=-1, keepdims=True), jnp.max(sc, axis=-1, keepdims=True))
            yield
            ew = jnp.exp2(sw - mx)
            ec = jnp.exp2(sc - mx)
            yield
            r = 1.0 / (jnp.sum(ew, axis=-1, keepdims=True) + jnp.sum(ec, axis=-1, keepdims=True))
            o = (jnp.dot(ew.astype(BF16), v_ref[0, win, sl].astype(BF16), preferred_element_type=F32)
                 + jnp.dot(ec.astype(BF16), v_ref[0, 0:NCTX, sl].astype(BF16), preferred_element_type=F32))
            yield
            results[h] = o * r

        run(lat_head)


def _attn_na_call(p, bias):
    tq = NA_GROUP * GRID_W
    nk = NA_KROWS * GRID_W
    n_grp = SEQ // tq

    def bias_map(b, g):
        grp = jnp.maximum(g - 1, 0)
        return (0, jnp.where(grp == 0, 0, jnp.where(grp == n_grp - 1, 2, 1)), 0, 0)

    return pl.pallas_call(
        _attn_na_kernel,
        grid=(NB, T // tq),
        in_specs=[
            pl.BlockSpec((1, tq, 512), lambda b, g: (b, g, B_Q // 4)),
            pl.BlockSpec((1, T, 512), lambda b, g: (b, 0, B_K // 4)),
            pl.BlockSpec((1, T, 512), lambda b, g: (b, 0, B_V // 4)),
            pl.BlockSpec((8, 1, tq, nk), bias_map),
        ],
        out_specs=pl.BlockSpec((1, tq, 512), lambda b, g: (b, g, 0)),
        out_shape=jax.ShapeDtypeStruct((NB, T, 512), BF16),
        compiler_params=_cparams(("parallel", "arbitrary"), 48),
        name="nbr_attn",
    )(p, p, p, bias)


def _na_bias_table(rel_bias):
    rows = SEQ // GRID_W
    n_grp = rows // NA_GROUP
    cq = np.arange(GRID_W)
    col_start = np.clip(cq - NA_WIN_C // 2, 0, GRID_W - NA_WIN_C)
    col_ok = (cq[None, :] >= col_start[:, None]) & (cq[None, :] < col_start[:, None] + NA_WIN_C)
    dc = np.clip(cq[None, :] - cq[:, None], -(NA_WIN_C - 1), NA_WIN_C - 1) + (NA_WIN_C - 1)
    gi = np.arange(n_grp)[:, None, None]
    qr = np.arange(NA_GROUP)[None, :, None]
    ku = np.arange(NA_KROWS)[None, None, :]
    r = NA_GROUP * gi + qr
    u = np.clip(NA_GROUP * gi - NA_WIN_R // 2, 0, rows - NA_KROWS) + ku
    row_start = np.clip(r - NA_WIN_R // 2, 0, rows - NA_WIN_R)
    row_ok = (u >= row_start) & (u < row_start + NA_WIN_R)
    dr = np.where(row_ok, u - r + (NA_WIN_R - 1), -1)
    for g in range(2, n_grp - 1):
        assert np.array_equal(dr[g], dr[1])
    assert np.all((dc == cq[None, :] - cq[:, None] + NA_WIN_C - 1)[col_ok])
    rbp = jnp.pad(rel_bias.astype(F32) * LOG2E, ((0, 0), (0, 0), (GRID_W, GRID_W)))
    rbt = jnp.stack([rbp[:, :, GRID_W + NA_WIN_C - 1 - q:2 * GRID_W + NA_WIN_C - 1 - q] for q in range(GRID_W)],
                    axis=2)
    rbm = jnp.where(col_ok[None, None], rbt, -jnp.inf)
    ninf = jnp.full((rel_bias.shape[0], GRID_W, GRID_W), -jnp.inf, F32)
    pats = []
    for g in (0, 1, n_grp - 1):
        qrows = []
        for a in range(NA_GROUP):
            blocks = [rbm[:, dr[g, a, b]] if dr[g, a, b] >= 0 else ninf for b in range(NA_KROWS)]
            qrows.append(jnp.concatenate(blocks, axis=-1))
        pats.append(jnp.concatenate(qrows, axis=-2))
    return jnp.stack(pats, axis=1)


def _cumsum_rows(g, rev):
    r = lax.broadcasted_iota(jnp.int32, (CH, CH), 0)
    c = lax.broadcasted_iota(jnp.int32, (CH, CH), 1)
    tri = jnp.where((c >= r) if rev else (c <= r), 1.0, 0.0).astype(BF16)
    g1 = g.astype(BF16)
    r1 = g - g1.astype(F32)
    g2 = r1.astype(BF16)
    g3 = (r1 - g2.astype(F32)).astype(BF16)
    dot = lambda a: jnp.dot(tri, a, preferred_element_type=F32)
    return dot(g1) + dot(g2) + dot(g3)


def _block_rows(a, first, period):
    return jnp.concatenate(
        [jnp.broadcast_to(a[r:r + 1, :], (period, 128)) for r in range(first, CH, period)], axis=0)


def _gla_chunk(load, get_state, put, masks, rev):
    q, k, g, vs = load()
    nh = len(vs)
    b = _cumsum_rows(g * LOG2E, rev)
    b_end = b[0:1] if rev else b[CH - 1:CH]
    yield
    r2 = lax.broadcasted_iota(jnp.int32, (CH, CH), 0)
    c2 = lax.broadcasted_iota(jnp.int32, (CH, CH), 1)

    def headq(a, h):
        return a if masks[h] is None else jnp.where(masks[h], a, 0.0)

    atts = [jnp.zeros((CH, CH), F32) for _ in range(nh)]
    n = CH // 2
    while n >= SUB:
        first = n if rev else n - 1
        ref = _block_rows(b, first, 2 * n) if 2 * n < CH else b[first:first + 1]
        qn = (q * jnp.exp2(b - ref)).astype(BF16)
        kn = (k * jnp.exp2(ref - b)).astype(BF16)
        sh = (2 * n).bit_length() - 1
        same = (r2 >> sh) == (c2 >> sh)
        rin = r2 & (2 * n - 1)
        cin = c2 & (2 * n - 1)
        valid = (same & (rin < n) & (cin >= n)) if rev else (same & (rin >= n) & (cin < n))
        for h in range(nh):
            a = _nt(headq(qn, h), kn)
            atts[h] = jnp.where(valid, a, atts[h])
        n //= 2
        yield

    sub_sh = SUB.bit_length() - 1
    dvalid = ((r2 >> sub_sh) == (c2 >> sub_sh)) & ((c2 >= r2) if rev else (c2 <= r2))
    diag = [jnp.zeros((CH, CH), F32) for _ in range(nh)]
    for s in range(SUB):
        e = q * _block_rows(k, s, SUB) * jnp.exp2(b - _block_rows(b, s, SUB))
        hit = (c2 & (SUB - 1)) == s
        for h in range(nh):
            rs = jnp.sum(headq(e, h), axis=-1, keepdims=True)
            diag[h] = jnp.where(hit, rs, diag[h])
        yield

    qe = (q * jnp.exp2(b)).astype(BF16)
    kd = k * jnp.exp2(b_end - b)
    dec = jnp.exp2(b_end)
    yield
    sts = get_state()
    outs, new = [], []
    for h in range(nh):
        att = jnp.where(dvalid, diag[h], atts[h])
        vb = vs[h].astype(BF16)
        o = jnp.dot(att.astype(BF16), vb, preferred_element_type=F32) + _nt(qe, sts[h].astype(BF16))
        outs.append(o)
        new.append(sts[h] * dec + _tn(vb, headq(kd, h).astype(BF16)))
    put(outs, new)


SCAN_CHUNKS = 2
SCAN_ROWS = SCAN_CHUNKS * CH


def _chunk_maps(col_block):
    n_c = NCTX // SCAN_ROWS
    n_all = T // SCAN_ROWS

    def fwd(b, s):
        return (b, s, col_block)

    def bwd(b, s):
        return (b, jnp.where(s < n_c, n_c - 1 - s, n_all + n_c - 1 - s), col_block)

    return fwd, bwd


def _scan_order(rev):
    return tuple(reversed(range(SCAN_CHUNKS))) if rev else tuple(range(SCAN_CHUNKS))


def _hgrn_kernel(qf_ref, if_ref, ff_ref, qb_ref, ib_ref, fb_ref, lb_ref, of_ref, ob_ref, st_ref):
    @pl.when(pl.program_id(1) == 0)
    def _():
        st_ref[...] = jnp.zeros_like(st_ref)

    dirs = ((qf_ref, if_ref, ff_ref, of_ref), (qb_ref, ib_ref, fb_ref, ob_ref))
    gens, finals = [], []
    for d, (q_ref, i_ref, f_ref, o_ref) in enumerate(dirs):
        for h in range(4):
            sl = slice(128 * h, 128 * h + 128)
            state = [[st_ref[d, h]]]
            finals.append((d, h, state))
            for j in _scan_order(d == 1):
                rows = slice(CH * j, CH * j + CH)

                def load(q_ref=q_ref, i_ref=i_ref, f_ref=f_ref, d=d, sl=sl, rows=rows):
                    lb = lb_ref[d:d + 1, sl]
                    q = _silu(q_ref[0, rows, sl])
                    f = f_ref[0, rows, sl]
                    e = jnp.exp(-jnp.abs(f))
                    inv = 1.0 / (1.0 + e)
                    k = (1.0 - lb) * jnp.where(f >= 0.0, e * inv, inv)
                    a = jnp.log(lb)
                    c = jnp.log(1.0 - lb) + (jnp.minimum(f, 0.0) - jnp.log(1.0 + e))
                    g = jnp.maximum(a, c) + jnp.log(1.0 + jnp.exp(-jnp.abs(a - c)))
                    return q, k, g, [i_ref[0, rows, sl]]

                def put(outs, new, o_ref=o_ref, sl=sl, rows=rows, state=state):
                    o_ref[0, rows, sl] = outs[0]
                    state[0] = new

                gens.append(_gla_chunk(load, lambda state=state: state[0], put, [None], d == 1))
    _interleave(gens)
    for d, h, state in finals:
        st_ref[d, h] = state[0][0]


def _hgrn_call(p, lb):
    fq, bq = _chunk_maps(C_Q)
    fi, bi = _chunk_maps(C_I)
    ff, _ = _chunk_maps(C_FF)
    _, bf = _chunk_maps(C_FB)
    fo, bo = _chunk_maps(0)
    blk = lambda m: pl.BlockSpec((1, SCAN_ROWS, 512), m)
    return pl.pallas_call(
        _hgrn_kernel,
        grid=(NB, T // SCAN_ROWS),
        in_specs=[blk(fq), blk(fi), blk(ff), blk(bq), blk(bi), blk(bf),
                  pl.BlockSpec((2, 512), lambda b, s: (0, 0))],
        out_specs=[blk(fo), blk(bo)],
        out_shape=[jax.ShapeDtypeStruct((NB, T, 512), F32)] * 2,
        scratch_shapes=[pltpu.VMEM((2, 4, 128, 128), F32)],
        compiler_params=_cparams(("parallel", "arbitrary"), 32),
        name="hgrn2_scan",
    )(p, p, p, p, p, p, lb)


def _gla_kernel(qkf_ref, vf_ref, zf_ref, qkb_ref, vb_ref, zb_ref, w2_ref, b2_ref, of_ref, ob_ref, st_ref):
    @pl.when(pl.program_id(1) == 0)
    def _():
        st_ref[...] = jnp.zeros_like(st_ref)

    lane = lax.broadcasted_iota(jnp.int32, (1, 128), 1)
    masks = [lane < GLA_DK, lane >= GLA_DK]
    dirs = ((qkf_ref, vf_ref, zf_ref, of_ref), (qkb_ref, vb_ref, zb_ref, ob_ref))
    gens, finals = [], []
    for d, (qk_ref, v_ref, z_ref, o_ref) in enumerate(dirs):
        logit = jnp.dot(z_ref[0].astype(BF16), w2_ref[d].astype(BF16), preferred_element_type=F32) + b2_ref[d]
        g = _log_sigmoid(logit) * (1.0 / GLA_TAU)
        for grp in range(2):
            sl = slice(128 * grp, 128 * grp + 128)
            heads = (2 * grp, 2 * grp + 1)
            state = [[st_ref[d, h] for h in heads]]
            finals.append((d, heads, state))
            for j in _scan_order(d == 1):
                rows = slice(CH * j, CH * j + CH)

                def load(qk_ref=qk_ref, v_ref=v_ref, g=g, grp=grp, sl=sl, heads=heads, rows=rows):
                    q = qk_ref[0, rows, sl] * (GLA_DK ** -0.5)
                    k = qk_ref[0, rows, 256 + 128 * grp:256 + 128 * grp + 128]
                    return q, k, g[rows, sl], [v_ref[0, rows, 128 * h:128 * h + 128] for h in heads]

                def put(outs, new, o_ref=o_ref, heads=heads, rows=rows, state=state):
                    for o, h in zip(outs, heads):
                        o_ref[0, rows, 128 * h:128 * h + 128] = o
                    state[0] = new

                gens.append(_gla_chunk(load, lambda state=state: state[0], put, masks, d == 1))
    _interleave(gens)
    for d, heads, state in finals:
        for st, h in zip(state[0], heads):
            st_ref[d, h] = st


def _gla_call(p, pt, w2p, b2):
    fqk, bqk = _chunk_maps(D_QK)
    fv, bv = _chunk_maps(D_V)
    fz, bz = _chunk_maps(TAIL_Z)
    fo, bo = _chunk_maps(0)
    blk = lambda m: pl.BlockSpec((1, SCAN_ROWS, 512), m)
    zblk = lambda m: pl.BlockSpec((1, SCAN_ROWS, 128), m)
    return pl.pallas_call(
        _gla_kernel,
        grid=(NB, T // SCAN_ROWS),
        in_specs=[blk(fqk), blk(fv), zblk(fz), blk(bqk), blk(bv), zblk(bz),
                  pl.BlockSpec((2, 128, 256), lambda b, s: (0, 0, 0)),
                  pl.BlockSpec((2, 1, 256), lambda b, s: (0, 0, 0))],
        out_specs=[blk(fo), blk(bo)],
        out_shape=[jax.ShapeDtypeStruct((NB, T, 512), F32)] * 2,
        scratch_shapes=[pltpu.VMEM((2, 4, 128, 128), F32)],
        compiler_params=_cparams(("parallel", "arbitrary"), 32),
        name="gla_scan",
    )(p, p, pt, p, p, pt, w2p, b2)


def _gated_norm(o, gate, g):
    parts = []
    for h in range(4):
        sl = slice(128 * h, 128 * h + 128)
        oh = o[:, sl]
        ms = jnp.mean(oh * oh, axis=-1, keepdims=True)
        parts.append((oh * lax.rsqrt(ms + EPS) * g * _silu(gate[:, sl])).astype(BF16))
    return parts


def _cdpost_kernel(cf_ref, cb_ref, cg_ref, df_ref, db_ref, dr_ref, gc_ref, gd_ref, o_ref):
    for h, part in enumerate(_gated_norm(cf_ref[0] + cb_ref[0], cg_ref[0], gc_ref[...])):
        o_ref[0, :, 128 * h:128 * h + 128] = part
    for h, part in enumerate(_gated_norm(df_ref[0] + db_ref[0], dr_ref[0], gd_ref[...])):
        o_ref[0, :, 512 + 128 * h:512 + 128 * h + 128] = part


def _cdpost_call(cf, cb, df, db, p, pt, gc, gd):
    tm = 256
    row = lambda col: pl.BlockSpec((1, tm, 512), lambda b, i: (b, i, col))
    vec = pl.BlockSpec((1, 128), lambda b, i: (0, 0))
    return pl.pallas_call(
        _cdpost_kernel,
        grid=(NB, T // tm),
        in_specs=[row(0), row(0), row(C_G), row(0), row(0), row(TAIL_R), vec, vec],
        out_specs=pl.BlockSpec((1, tm, 1024), lambda b, i: (b, i, 0)),
        out_shape=jax.ShapeDtypeStruct((NB, T, 1024), BF16),
        compiler_params=_cparams(("parallel", "parallel"), 32),
        name="scan_post",
    )(cf, cb, p, df, db, pt, gc, gd)


def _ffn_kernel(h_ref, w1_ref, w3_ref, w2_ref, o_ref):
    @pl.when(pl.program_id(2) == 0)
    def _():
        o_ref[...] = jnp.zeros_like(o_ref)

    h = h_ref[0]
    u = jnp.dot(h, w1_ref[...].astype(BF16), preferred_element_type=F32)
    v = jnp.dot(h, w3_ref[...].astype(BF16), preferred_element_type=F32)
    o_ref[0] += jnp.dot((_silu(u) * v).astype(BF16), w2_ref[...].astype(BF16), preferred_element_type=F32)


def _ffn_call(h, w1, w3, w2):
    tm, tf = 1152, 512
    return pl.pallas_call(
        _ffn_kernel,
        grid=(NB, T // tm, FFN_DENSE // tf),
        in_specs=[
            pl.BlockSpec((1, tm, D), lambda b, i, f: (b, i, 0)),
            pl.BlockSpec((D, tf), lambda b, i, f: (0, f)),
            pl.BlockSpec((D, tf), lambda b, i, f: (0, f)),
            pl.BlockSpec((tf, D), lambda b, i, f: (f, 0)),
        ],
        out_specs=pl.BlockSpec((1, tm, D), lambda b, i, f: (b, i, 0), pipeline_mode=pl.Buffered(1)),
        out_shape=jax.ShapeDtypeStruct((NB, T, D), F32),
        compiler_params=_cparams(("parallel", "parallel", "arbitrary"), 56),
        name="dense_ffn",
    )(h, w1, w3, w2)


MOE_TM = 1152
MOE_SUB = 384
MOE_TF = 512
MOE_TILES = -(-(2 * NB * SEQ + N_EXPERTS * (MOE_TM - 1)) // MOE_TM)
MOE_ROWS = MOE_TILES * MOE_TM
MOE_PREFETCH_STEPS = 8
MOE_PREFETCH_ROWS = MOE_TM // MOE_PREFETCH_STEPS


def _route_kernel(x_ref, g_ref, sh_ref, sc_ref, wr_ref, h_ref, r_ref):
    x = x_ref[0]
    ms = jnp.mean(x * x, axis=-1, keepdims=True)
    h = x * lax.rsqrt(ms + EPS) * g_ref[...] * (1.0 + sc_ref[0]) + sh_ref[0]
    h_ref[0] = h
    logits = jnp.dot(h, wr_ref[...], preferred_element_type=F32, precision=lax.Precision.HIGHEST)
    lane = lax.broadcasted_iota(jnp.int32, logits.shape, 1)
    lanef = lane.astype(F32)
    lg = jnp.where(lane < N_EXPERTS, logits, -jnp.inf)
    m1 = jnp.max(lg, axis=-1, keepdims=True)
    i1 = jnp.min(jnp.where(lg == m1, lanef, 128.0), axis=-1, keepdims=True)
    lg2 = jnp.where(lanef == i1, -jnp.inf, lg)
    m2 = jnp.max(lg2, axis=-1, keepdims=True)
    i2 = jnp.min(jnp.where(lg2 == m2, lanef, 128.0), axis=-1, keepdims=True)
    e = jnp.exp(m2 - m1)
    w1 = 1.0 / (1.0 + e)
    w2 = e * w1
    r_ref[0] = jnp.where(lane == 0, i1, jnp.where(lane == 1, i2, jnp.where(lane == 2, w1,
                         jnp.where(lane == 3, w2, 0.0))))


def _route_call(xall, g, mods, wr):
    tm = 256
    return pl.pallas_call(
        _route_kernel,
        grid=(NB, SEQ // tm),
        in_specs=[
            pl.BlockSpec((1, tm, D), lambda b, i: (b, i + NCTX // tm, 0)),
            pl.BlockSpec((1, D), lambda b, i: (0, 0)),
            _mod_spec(3, False), _mod_spec(4, False),
            pl.BlockSpec((D, 128), lambda b, i: (0, 0)),
        ],
        out_specs=[pl.BlockSpec((1, tm, D), lambda b, i: (b, i, 0)),
                   pl.BlockSpec((1, tm, 128), lambda b, i: (b, i, 0))],
        out_shape=[jax.ShapeDtypeStruct((NB, SEQ, D), F32), jax.ShapeDtypeStruct((NB, SEQ, 128), F32)],
        compiler_params=_cparams(("parallel", "parallel"), 32),
        name="moe_route",
    )(xall, g, mods, mods, wr)


def _moe_kernel(te_ref, nl_ref, src_ref, h_ref, w1_ref, w3_ref, w2_ref, o_ref, xg_ref, xb_ref, sem):
    i = pl.program_id(0)
    f = pl.program_id(1)
    n_live = nl_ref[i]
    nxt_live = nl_ref[jnp.minimum(i + 1, MOE_TILES - 1)]
    fetch = (i + 1 < MOE_TILES) & (nxt_live > 0) & (f >= 1) & (f <= MOE_PREFETCH_STEPS)

    def row_copy(tile, r):
        row = src_ref[tile * MOE_TM + r]
        return pltpu.make_async_copy(h_ref.at[pl.ds(row, 1), :], xg_ref.at[pl.ds(r, 1), :], sem)

    @pl.when(f == 0)
    def _():
        o_ref[...] = jnp.zeros_like(o_ref)

    @pl.when((i == 0) & (f == 0) & (n_live > 0))
    def _():
        def body(r, c):
            row_copy(0, r).start()
            return c

        lax.fori_loop(0, MOE_TM, body, 0)

    @pl.when((f == 0) & (n_live > 0))
    def _():
        pltpu.make_async_copy(h_ref.at[pl.ds(0, MOE_TM), :], xg_ref, sem).wait()
        xb_ref[...] = xg_ref[...].astype(BF16)

    def step(n, do_fetch):
        def body():
            if do_fetch:
                r0 = (f - 1) * MOE_PREFETCH_ROWS
                for r in range(MOE_PREFETCH_ROWS):
                    row_copy(i + 1, r0 + r).start()
            rows = slice(0, n * MOE_SUB)
            h = xb_ref[rows, :]
            u = jnp.dot(h, w1_ref[0].astype(BF16), preferred_element_type=F32)
            v = jnp.dot(h, w3_ref[0].astype(BF16), preferred_element_type=F32)
            o_ref[rows, :] += jnp.dot((_silu(u) * v).astype(BF16), w2_ref[0].astype(BF16),
                                      preferred_element_type=F32)

        return body

    for n in range(1, MOE_TM // MOE_SUB + 1):
        for do_fetch in (False, True):
            pl.when((n_live == n) & (fetch == do_fetch))(step(n, do_fetch))


def _moe_call(tile_expert, tile_live, src_rows, h, w1, w3, w2):
    nf = FFN_EXPERT // MOE_TF

    def fidx(i, f, nl):
        return jnp.where(nl[i] > 0, f, nf - 1)

    return pl.pallas_call(
        _moe_kernel,
        grid_spec=pltpu.PrefetchScalarGridSpec(
            num_scalar_prefetch=3,
            grid=(MOE_TILES, nf),
            in_specs=[
                pl.BlockSpec(memory_space=pl.ANY),
                pl.BlockSpec((1, D, MOE_TF), lambda i, f, te, nl, src: (te[i], 0, fidx(i, f, nl))),
                pl.BlockSpec((1, D, MOE_TF), lambda i, f, te, nl, src: (te[i], 0, fidx(i, f, nl))),
                pl.BlockSpec((1, MOE_TF, D), lambda i, f, te, nl, src: (te[i], fidx(i, f, nl), 0)),
            ],
            out_specs=pl.BlockSpec((MOE_TM, D), lambda i, f, te, nl, src: (i, 0), pipeline_mode=pl.Buffered(1)),
            scratch_shapes=[pltpu.VMEM((MOE_TM, D), F32), pltpu.VMEM((MOE_TM, D), BF16),
                            pltpu.SemaphoreType.DMA(())],
        ),
        out_shape=jax.ShapeDtypeStruct((MOE_ROWS, D), F32),
        compiler_params=_cparams(("arbitrary", "arbitrary"), 58),
        name="moe_experts",
    )(tile_expert, tile_live, src_rows, h, w1, w3, w2)


def _row_copy(src_hbm, row, dst_ref, r, sem):
    return pltpu.make_async_copy(src_hbm.at[pl.ds(row, 1), :], dst_ref.at[pl.ds(r, 1), :], sem)


def _combine_kernel(dest_ref, x_ref, r_ref, gm_ref, gf_ref, y_ref, o_ref, y0_ref, y1_ref, sem, *, tc):
    base = (pl.program_id(0) * (SEQ // tc) + pl.program_id(1)) * tc

    def issue(r, c):
        _row_copy(y_ref, dest_ref[2 * (base + r)], y0_ref, r, sem).start()
        _row_copy(y_ref, dest_ref[2 * (base + r) + 1], y1_ref, r, sem).start()
        return c

    lax.fori_loop(0, tc, issue, 0, unroll=8)
    pltpu.make_async_copy(y_ref.at[pl.ds(0, tc), :], y0_ref, sem).wait()
    pltpu.make_async_copy(y_ref.at[pl.ds(0, tc), :], y1_ref, sem).wait()
    rt = r_ref[0]
    moe = rt[:, 2:3] * y0_ref[...] + rt[:, 3:4] * y1_ref[...]
    x = x_ref[0] + gm_ref[0] * moe
    ms = jnp.mean(x * x, axis=-1, keepdims=True)
    o_ref[0] = x * lax.rsqrt(ms + EPS) * gf_ref[...]


def _combine_call(dest, xall, route, mods, gfinal, y):
    tc = 256
    return pl.pallas_call(
        functools.partial(_combine_kernel, tc=tc),
        grid_spec=pltpu.PrefetchScalarGridSpec(
            num_scalar_prefetch=1,
            grid=(NB, SEQ // tc),
            in_specs=[
                pl.BlockSpec((1, tc, D), lambda b, i, d: (b, i + NCTX // tc, 0)),
                pl.BlockSpec((1, tc, 128), lambda b, i, d: (b, i, 0)),
                pl.BlockSpec((1, 1, D), lambda b, i, d: (b * 6 + 5, 0, 0)),
                pl.BlockSpec((1, D), lambda b, i, d: (0, 0)),
                pl.BlockSpec(memory_space=pl.ANY),
            ],
            out_specs=pl.BlockSpec((1, tc, D), lambda b, i, d: (b, i, 0)),
            scratch_shapes=[pltpu.VMEM((tc, D), F32), pltpu.VMEM((tc, D), F32), pltpu.SemaphoreType.DMA(())],
        ),
        out_shape=jax.ShapeDtypeStruct((NB, SEQ, D), F32),
        compiler_params=_cparams(("arbitrary", "arbitrary"), 32),
        name="moe_combine",
    )(dest, xall, route, mods, gfinal, y)


def _moe_plan(route):
    n_pairs = 2 * NB * SEQ
    e = route[..., 0:2].astype(jnp.int32).reshape(n_pairs)
    onehot = (e[:, None] == jnp.arange(N_EXPERTS, dtype=jnp.int32)[None, :]).astype(jnp.int32)
    csum = jnp.cumsum(onehot, axis=0)
    rank = jnp.sum(onehot * csum, axis=1) - 1
    counts = csum[-1]
    padded = ((counts + MOE_TM - 1) // MOE_TM) * MOE_TM
    ends = jnp.cumsum(padded)
    starts = ends - padded
    dest = (starts[e] + rank).astype(jnp.int32)
    src_rows = jnp.zeros((MOE_ROWS,), jnp.int32).at[dest].set(jnp.arange(n_pairs, dtype=jnp.int32) // 2)
    tile_start = jnp.arange(MOE_TILES, dtype=jnp.int32) * MOE_TM
    t_eff = jnp.minimum(tile_start, ends[-1] - MOE_TM)
    tile_expert = jnp.sum((ends[None, :] <= t_eff[:, None]).astype(jnp.int32), axis=1)
    rows_left = (starts + counts)[tile_expert] - tile_start
    tile_live = jnp.clip((rows_left + MOE_SUB - 1) // MOE_SUB, 0, MOE_TM // MOE_SUB).astype(jnp.int32)
    return dest, src_rows, tile_expert.astype(jnp.int32), tile_live


def _rope_tables():
    per_axis = DIFF_QK // 4
    t = np.arange(SEQ)
    inv = ROPE_BASE ** (-np.arange(per_axis, dtype=np.float32) / per_axis)
    ang = np.concatenate([(t // GRID_W).astype(np.float32)[:, None] * inv,
                          (t % GRID_W).astype(np.float32)[:, None] * inv], axis=-1).astype(np.float32)
    cos = np.concatenate([np.ones((NCTX, 2 * per_axis), np.float32), np.cos(ang)], axis=0)
    sin = np.concatenate([np.zeros((NCTX, 2 * per_axis), np.float32), np.sin(ang)], axis=0)
    reps = 128 // (2 * per_axis)
    return jnp.asarray(np.tile(cos, (1, reps))), jnp.asarray(np.tile(sin, (1, reps)))


def _in_proj_tail(w_t):
    z = w_t[:, IN_MAIN:IN_MAIN + 2 * GLA_RANK]
    r = w_t[:, IN_MAIN + 2 * GLA_RANK:]
    pad = jnp.zeros((w_t.shape[0], TAIL_W - r.shape[1] - z.shape[1], D), w_t.dtype)
    return jnp.concatenate([r, z, pad], axis=1)


def kernel(x, c, ctx, c_ctx, ada_w, ada_b, norm_mix, norm_ffn, w_in, w_out, diff_lambda, diff_norm,
           na_rel_bias, hgrn_lower_bounds, hgrn_norm, gla_gate_w2, gla_gate_b, gla_norm,
           ffn_w1, ffn_w3, ffn_w2, moe_router, moe_w1, moe_w3, moe_w2, final_norm):
    lb_soft = jax.nn.softmax(hgrn_lower_bounds.astype(F32), axis=1)
    lower_bounds = jnp.clip(jnp.cumsum(lb_soft, axis=1) - lb_soft[:, :1], 0.0, 1.0 - 1e-6)
    cond8 = jnp.concatenate([c, c_ctx[None, :], jnp.zeros((8 - NB - 1, D), F32)], axis=0)
    mods_all = _ada_call(cond8, ada_w, ada_b).reshape(DEPTH, 8 * 6, 1, D)
    cos, sin = _rope_tables()
    xall = jnp.concatenate([ctx, x], axis=1)
    w_in_t = jnp.swapaxes(w_in, 1, 2)
    w_tail_t = _in_proj_tail(w_in_t)

    out = None
    pending = None
    for l in range(DEPTH):
        mods = mods_all[l]
        if pending is None:
            h = _norm_call(xall, norm_mix[l][None, :], mods, 0, 1)
        else:
            xall, h = _norm_call(xall, norm_mix[l][None, :], mods, 0, 1, resid=pending)
            pending = None
        p = _wsmm_call([h], w_in_t, l, 1664, IN_MAIN // 1664, w_t=True, vmem_mb=56, name="in_proj")
        pt = _wsmm_call([h], w_tail_t, l, TAIL_W, 1, w_t=True, vmem_mb=32, name="in_proj_tail")

        lambda_init = 0.8 - 0.6 * math.exp(-0.3 * l)
        lp = diff_lambda[l].astype(F32)
        lam = (jnp.exp(jnp.sum(lp[0] * lp[1])) - jnp.exp(jnp.sum(lp[2] * lp[3])) + lambda_init).reshape(1)
        a = _attn_a_call(p, lam, cos, sin, jnp.tile(diff_norm[l], 2)[None, :], 1.0 - lambda_init)
        n = _attn_na_call(p, _na_bias_table(na_rel_bias[l]))
        cf, cb = _hgrn_call(p, lower_bounds[:, l])
        w2p = jnp.zeros((2, 128, 4 * GLA_DK), F32)
        w2p = w2p.at[0, 0:GLA_RANK].set(gla_gate_w2[l, 0]).at[1, GLA_RANK:2 * GLA_RANK].set(gla_gate_w2[l, 1])
        df, db = _gla_call(p, pt, w2p, gla_gate_b[l][:, None, :])
        cd = _cdpost_call(cf, cb, df, db, p, pt, hgrn_norm[l][None, :], gla_norm[l][None, :])
        xall = _wsmm_call([a, n, cd], w_out, l, 1024, D // 1024, resid=(xall, mods, 2), name="out_proj")

        if l % 2 == 0:
            h2 = _norm_call(xall, norm_ffn[l][None, :], mods, 3, 4)
            y = _ffn_call(h2, ffn_w1[l // 2], ffn_w3[l // 2], ffn_w2[l // 2])
            pending = (y, mods, 5)
        else:
            assert l == DEPTH - 1
            wr = jnp.zeros((D, 128), F32).at[:, :N_EXPERTS].set(moe_router[l // 2])
            hr, route = _route_call(xall, norm_ffn[l][None, :], mods, wr)
            dest, src_rows, tile_expert, tile_live = _moe_plan(route)
            y = _moe_call(tile_expert, tile_live, src_rows, hr.reshape(NB * SEQ, D),
                          moe_w1[l // 2], moe_w3[l // 2], moe_w2[l // 2])
            out = _combine_call(dest, xall, route, mods, final_norm[None, :], y)
    return out
```

```python
import functools
import math

import numpy as np
import jax
import jax.numpy as jnp
from jax import lax
from jax.experimental import pallas as pl
from jax.experimental.pallas import tpu as pltpu

F32 = jnp.float32
BF16 = jnp.bfloat16

D = 2048
NB = 4
SEQ = 2048
NCTX = 256
T = NCTX + SEQ
DEPTH = 2
GRID_W = 64
ROPE_BASE = 10000.0
EPS = 1e-6
LOG2E = math.log2(math.e)

DIFF_QK = 32
NA_DIM = 64
NA_WIN_R = 8
NA_WIN_C = 16
NA_GROUP = 4
NA_KROWS = NA_GROUP + NA_WIN_R - 1
GLA_DK = 64
GLA_TAU = 16.0
GLA_RANK = 16
CH = 64
SUB = 8

FFN_DENSE = 5632
N_EXPERTS = 8
FFN_EXPERT = 7168

IN_MAIN = 6656
A_Q, A_K, A_V = 0, 4, 8
B_Q, B_K, B_V = 12, 16, 20
C_Q, C_I, C_FF, C_FB, C_G = 6, 7, 8, 9, 10
D_QK, D_V = 11, 12
TAIL_W = 640
TAIL_R, TAIL_Z = 0, 4

V7X_VMEM_BYTES = 64 * 1024 * 1024
W_SPLIT = 4


def _cparams(sem, vmem_mb):
    assert vmem_mb * 1024 * 1024 < V7X_VMEM_BYTES
    return pltpu.CompilerParams(dimension_semantics=sem, vmem_limit_bytes=vmem_mb * 1024 * 1024)


def _sigmoid(x):
    return 1.0 / (1.0 + jnp.exp(-x))


def _silu(x):
    return x * _sigmoid(x)


def _log_sigmoid(x):
    return jnp.minimum(x, 0.0) - jnp.log(1.0 + jnp.exp(-jnp.abs(x)))


def _nt(a, b):
    return lax.dot_general(a, b, (((1,), (1,)), ((), ())), preferred_element_type=F32)


def _tn(a, b):
    return lax.dot_general(a, b, (((0,), (0,)), ((), ())), preferred_element_type=F32)


def _interleave(gens):
    live = list(gens)
    while live:
        still = []
        for gen in live:
            try:
                next(gen)
                still.append(gen)
            except StopIteration:
                pass
        live = still


def _ada_kernel(c_ref, *refs):
    w_refs, b_ref, o_ref = refs[:W_SPLIT], refs[W_SPLIT], refs[W_SPLIT + 1]
    s = _silu(c_ref[...]).astype(BF16)
    kc = D // W_SPLIT
    acc = b_ref[0]
    for k in range(W_SPLIT):
        acc = acc + jnp.dot(s[:, k * kc:(k + 1) * kc], w_refs[k][0].astype(BF16), preferred_element_type=F32)
    o_ref[0] = acc


def _ada_call(cond8, ada_w, ada_b):
    tn = 1536
    n = 6 * D
    kc = D // W_SPLIT
    w_specs = [pl.BlockSpec((1, kc, tn), functools.partial(lambda l, j, k: (l, k, j), k=k)) for k in range(W_SPLIT)]
    return pl.pallas_call(
        _ada_kernel,
        grid=(DEPTH, n // tn),
        in_specs=[pl.BlockSpec((8, D), lambda l, j: (0, 0))] + w_specs
        + [pl.BlockSpec((1, 1, tn), lambda l, j: (l, 0, j))],
        out_specs=pl.BlockSpec((1, 8, tn), lambda l, j: (l, 0, j)),
        out_shape=jax.ShapeDtypeStruct((DEPTH, 8, n), F32),
        compiler_params=_cparams(("parallel", "parallel"), 40),
        name="ada_mod",
    )(cond8, *([ada_w] * W_SPLIT), ada_b.reshape(DEPTH, 1, n))


def _mod_spec(k, ctx):
    if ctx:
        return pl.BlockSpec((1, 1, D), lambda b, i: (NB * 6 + k, 0, 0))
    return pl.BlockSpec((1, 1, D), lambda b, i: (b * 6 + k, 0, 0))


def _row_is_ctx(row0, tm):
    return (row0 + lax.broadcasted_iota(jnp.int32, (tm, 1), 0)) < NCTX


def _modnorm(x, g, is_c, shl, scl, shc, scc):
    ms = jnp.mean(x * x, axis=-1, keepdims=True)
    y = x * lax.rsqrt(ms + EPS) * g
    sc = jnp.where(is_c, scc, scl)
    sh = jnp.where(is_c, shc, shl)
    return y * (1.0 + sc) + sh


def _norm_kernel(*refs, tm, resid):
    if resid:
        x_ref, y_ref, gl_ref, gx_ref, g_ref, shl_ref, scl_ref, shc_ref, scc_ref, xo_ref, h_ref = refs
    else:
        x_ref, g_ref, shl_ref, scl_ref, shc_ref, scc_ref, h_ref = refs
    is_c = _row_is_ctx(pl.program_id(1) * tm, tm)
    x = x_ref[0]
    if resid:
        x = x + jnp.where(is_c, gx_ref[0], gl_ref[0]) * y_ref[0]
        xo_ref[0] = x
    h_ref[0] = _modnorm(x, g_ref[...], is_c, shl_ref[0], scl_ref[0], shc_ref[0], scc_ref[0]).astype(BF16)


def _norm_call(xall, g, mods, k_shift, k_scale, resid=None):
    tm = 384
    row = pl.BlockSpec((1, tm, D), lambda b, i: (b, i, 0))
    in_specs, args = [row], [xall]
    if resid is not None:
        y, mods_prev, k_gate = resid
        in_specs += [row, _mod_spec(k_gate, False), _mod_spec(k_gate, True)]
        args += [y, mods_prev, mods_prev]
    in_specs += [pl.BlockSpec((1, D), lambda b, i: (0, 0)),
                 _mod_spec(k_shift, False), _mod_spec(k_scale, False),
                 _mod_spec(k_shift, True), _mod_spec(k_scale, True)]
    args += [g, mods, mods, mods, mods]
    h_shape = jax.ShapeDtypeStruct((NB, T, D), BF16)
    out_shape, out_specs = h_shape, row
    if resid is not None:
        out_shape, out_specs = [jax.ShapeDtypeStruct((NB, T, D), F32), h_shape], [row, row]
    return pl.pallas_call(
        functools.partial(_norm_kernel, tm=tm, resid=resid is not None),
        grid=(NB, T // tm),
        in_specs=in_specs, out_specs=out_specs, out_shape=out_shape,
        compiler_params=_cparams(("parallel", "parallel"), 32),
        name="mod_norm",
    )(*args)


def _wsmm_kernel(*refs, ksizes, tm, epilogue, w_t):
    n_in = len(ksizes)
    ins, w_ref = refs[:n_in], refs[n_in]
    o_ref, wb_ref = refs[-2], refs[-1]

    @pl.when((pl.program_id(1) == 0) & (pl.program_id(2) == 0))
    def _():
        wb_ref[...] = w_ref[0].astype(BF16)

    acc, k0 = None, 0
    for r, ks in zip(ins, ksizes):
        if w_t:
            part = _nt(r[0], wb_ref[:, k0:k0 + ks])
        else:
            part = jnp.dot(r[0], wb_ref[k0:k0 + ks, :], preferred_element_type=F32)
        acc = part if acc is None else acc + part
        k0 += ks
    if epilogue:
        x_ref, gl_ref, gx_ref = refs[n_in + 1:n_in + 4]
        gate = jnp.where(_row_is_ctx(pl.program_id(2) * tm, tm), gx_ref[0], gl_ref[0])
        acc = x_ref[0] + gate * acc
    o_ref[0] = acc


def _wsmm_call(acts, w, layer, tn, n_j, resid=None, w_t=False, vmem_mb=48, name="matmul"):
    tm = 768
    ksizes = tuple(a.shape[-1] for a in acts)
    kdim = sum(ksizes)
    in_specs = [pl.BlockSpec((1, tm, ks), lambda j, b, i: (b, i, 0)) for ks in ksizes]
    if w_t:
        in_specs.append(pl.BlockSpec((1, tn, kdim), lambda j, b, i: (layer, j, 0)))
    else:
        in_specs.append(pl.BlockSpec((1, kdim, tn), lambda j, b, i: (layer, 0, j)))
    args = list(acts) + [w]
    if resid is not None:
        x, mods, k_gate = resid
        in_specs += [pl.BlockSpec((1, tm, tn), lambda j, b, i: (b, i, j)),
                     pl.BlockSpec((1, 1, tn), lambda j, b, i: (b * 6 + k_gate, 0, j)),
                     pl.BlockSpec((1, 1, tn), lambda j, b, i: (NB * 6 + k_gate, 0, j))]
        args += [x, mods, mods]
    return pl.pallas_call(
        functools.partial(_wsmm_kernel, ksizes=ksizes, tm=tm, epilogue=resid is not None, w_t=w_t),
        grid=(n_j, NB, T // tm),
        in_specs=in_specs,
        out_specs=pl.BlockSpec((1, tm, tn), lambda j, b, i: (b, i, j)),
        out_shape=jax.ShapeDtypeStruct((NB, T, n_j * tn), F32),
        scratch_shapes=[pltpu.VMEM((tn, kdim) if w_t else (kdim, tn), BF16)],
        compiler_params=_cparams(("arbitrary", "arbitrary", "arbitrary"), vmem_mb),
        name=name,
    )(*args)


def _rope(x, cos, sin):
    lane = lax.broadcasted_iota(jnp.int32, x.shape, 1)
    first = (lane & (DIFF_QK - 1)) < (DIFF_QK // 2)
    rot = jnp.where(first, -pltpu.roll(x, 128 - DIFF_QK // 2, 1), pltpu.roll(x, DIFF_QK // 2, 1))
    return x * cos + rot * sin


def _attn_a_kernel(lam_ref, q_ref, k_ref, v_ref, cos_ref, sin_ref, g_ref, o_ref, ks_ref, va_ref, *, post_scale):
    t = pl.program_id(2)
    lane = lax.broadcasted_iota(jnp.int32, (1, 128), 1)
    ones_lane = (64, 0)

    @pl.when(t == 0)
    def _():
        ks_ref[...] = _rope(k_ref[0], cos_ref[...], sin_ref[...]).astype(BF16)
        v = v_ref[0]
        va_ref[0] = jnp.where(lane < 64, v, jnp.where(lane == ones_lane[0], 1.0, 0.0)).astype(BF16)
        va_ref[1] = jnp.where(lane >= 64, v, jnp.where(lane == ones_lane[1], 1.0, 0.0)).astype(BF16)

    lam = lam_ref[0]

    def attend(q0, nq, nk):
        rows = pl.ds(q0, nq)
        q = _rope(q_ref[0, rows, :], cos_ref[rows, :], sin_ref[rows, :]) * (DIFF_QK ** -0.5 * LOG2E)
        kk = ks_ref[0:nk, :]
        terms = {}

        def softmax_pv(hh, m):
            lo = 64 * hh + DIFF_QK * m
            qm = jnp.where((lane >= lo) & (lane < lo + DIFF_QK), q, 0.0).astype(BF16)
            s = _nt(qm, kk)
            yield
            mx = jnp.max(s, axis=-1, keepdims=True)
            yield
            e = jnp.exp2((s - mx).astype(BF16))
            yield
            num = jnp.dot(e, va_ref[hh, 0:nk, :], preferred_element_type=F32)
            yield
            den = jnp.sum(jnp.where(lane == ones_lane[hh], num, 0.0), axis=-1, keepdims=True)
            terms[hh, m] = num * ((1.0 if m == 0 else lam) / den)

        _interleave([softmax_pv(hh, m) for hh in range(2) for m in range(2)])
        outs = [terms[hh, 0] - terms[hh, 1] for hh in range(2)]
        o = jnp.where(lane < 64, outs[0], outs[1])
        sq = o * o
        s0 = jnp.sum(jnp.where(lane < 64, sq, 0.0), axis=-1, keepdims=True)
        s1 = jnp.sum(jnp.where(lane >= 64, sq, 0.0), axis=-1, keepdims=True)
        ms = jnp.where(lane < 64, s0, s1) * (1.0 / 64.0)
        o_ref[0, rows, :] = (o * lax.rsqrt(ms + EPS) * (g_ref[...] * post_scale)).astype(o_ref.dtype)

    @pl.when(t == 0)
    def _():
        attend(0, NCTX, NCTX)

    @pl.when(t > 0)
    def _():
        attend(pl.multiple_of(NCTX + (t - 1) * ATTN_TQ, ATTN_TQ // 2), ATTN_TQ, T)


ATTN_TQ = 512


def _attn_a_call(p, lam, cos, sin, g2, post_scale):
    full = lambda col: pl.BlockSpec((1, T, 128), lambda b, h, t: (b, 0, col + h))
    tab = pl.BlockSpec((T, 128), lambda b, h, t: (0, 0))
    return pl.pallas_call(
        functools.partial(_attn_a_kernel, post_scale=post_scale),
        grid=(NB, 4, 1 + SEQ // ATTN_TQ),
        in_specs=[pl.BlockSpec(memory_space=pltpu.SMEM), full(A_Q), full(A_K), full(A_V), tab, tab,
                  pl.BlockSpec((1, 128), lambda b, h, t: (0, 0))],
        out_specs=pl.BlockSpec((1, T, 128), lambda b, h, t: (b, 0, h)),
        out_shape=jax.ShapeDtypeStruct((NB, T, 512), BF16),
        scratch_shapes=[pltpu.VMEM((T, 128), BF16), pltpu.VMEM((2, T, 128), BF16)],
        compiler_params=_cparams(("parallel", "parallel", "arbitrary"), 48),
        name="diff_attn",
    )(lam, p, p, p, cos, sin, g2)


def _attn_na_kernel(q_ref, k_ref, v_ref, bias_ref, o_ref):
    g = pl.program_id(1)
    lane = lax.broadcasted_iota(jnp.int32, (1, 128), 1)

    def run(chain):
        results = {}
        gens = []
        for p in range(4):
            sl = slice(128 * p, 128 * p + 128)
            q = q_ref[0, :, sl] * (NA_DIM ** -0.5 * LOG2E)
            for hh in range(2):
                qm = jnp.where((lane >= 64 * hh) & (lane < 64 * hh + 64), q, 0.0).astype(BF16)
                gens.append(chain(2 * p + hh, qm, sl, results))
        _interleave(gens)
        for p in range(4):
            o_ref[0, :, 128 * p:128 * p + 128] = jnp.where(lane < 64, results[2 * p], results[2 * p + 1]
                                                           ).astype(o_ref.dtype)

    @pl.when(g == 0)
    def _():
        def ctx_head(h, qm, sl, results):
            s = _nt(qm, k_ref[0, 0:NCTX, sl].astype(BF16))
            yield
            e = jnp.exp2(s - jnp.max(s, axis=-1, keepdims=True))
            yield
            r = 1.0 / jnp.sum(e, axis=-1, keepdims=True)
            results[h] = jnp.dot(e.astype(BF16), v_ref[0, 0:NCTX, sl].astype(BF16), preferred_element_type=F32) * r

        run(ctx_head)

    @pl.when(g > 0)
    def _():
        u0 = jnp.clip(NA_GROUP * (g - 1) - NA_WIN_R // 2, 0, SEQ // GRID_W - NA_KROWS)
        win = pl.ds(pl.multiple_of(NCTX + GRID_W * u0, GRID_W), NA_KROWS * GRID_W)

        def lat_head(h, qm, sl, results):
            sw = _nt(qm, k_ref[0, win, sl].astype(BF16)) + bias_ref[h, 0]
            sc = _nt(qm, k_ref[0, 0:NCTX, sl].astype(BF16))
            yield
            mx = jnp.maximum(jnp.max(sw, axis=-1, keepdims=True), jnp.max(sc, axis=-1, keepdims=True))
            yield
            ew = jnp.exp2(sw - mx)
            ec = jnp.exp2(sc - mx)
            yield
            r = 1.0 / (jnp.sum(ew, axis=-1, keepdims=True) + jnp.sum(ec, axis=-1, keepdims=True))
            o = (jnp.dot(ew.astype(BF16), v_ref[0, win, sl].astype(BF16), preferred_element_type=F32)
                 + jnp.dot(ec.astype(BF16), v_ref[0, 0:NCTX, sl].astype(BF16), preferred_element_type=F32))
            yield
            results[h] = o * r

        run(lat_head)


def _attn_na_call(p, bias):
    tq = NA_GROUP * GRID_W
    nk = NA_KROWS * GRID_W
    n_grp = SEQ // tq

    def bias_map(b, g):
        grp = jnp.maximum(g - 1, 0)
        return (0, jnp.where(grp == 0, 0, jnp.where(grp == n_grp - 1, 2, 1)), 0, 0)

    return pl.pallas_call(
        _attn_na_kernel,
        grid=(NB, T // tq),
        in_specs=[
            pl.BlockSpec((1, tq, 512), lambda b, g: (b, g, B_Q // 4)),
            pl.BlockSpec((1, T, 512), lambda b, g: (b, 0, B_K // 4)),
            pl.BlockSpec((1, T, 512), lambda b, g: (b, 0, B_V // 4)),
            pl.BlockSpec((8, 1, tq, nk), bias_map),
        ],
        out_specs=pl.BlockSpec((1, tq, 512), lambda b, g: (b, g, 0)),
        out_shape=jax.ShapeDtypeStruct((NB, T, 512), BF16),
        compiler_params=_cparams(("parallel", "arbitrary"), 48),
        name="nbr_attn",
    )(p, p, p, bias)


def _na_bias_table(rel_bias):
    rows = SEQ // GRID_W
    n_grp = rows // NA_GROUP
    cq = np.arange(GRID_W)
    col_start = np.clip(cq - NA_WIN_C // 2, 0, GRID_W - NA_WIN_C)
    col_ok = (cq[None, :] >= col_start[:, None]) & (cq[None, :] < col_start[:, None] + NA_WIN_C)
    dc = np.clip(cq[None, :] - cq[:, None], -(NA_WIN_C - 1), NA_WIN_C - 1) + (NA_WIN_C - 1)
    gi = np.arange(n_grp)[:, None, None]
    qr = np.arange(NA_GROUP)[None, :, None]
    ku = np.arange(NA_KROWS)[None, None, :]
    r = NA_GROUP * gi + qr
    u = np.clip(NA_GROUP * gi - NA_WIN_R // 2, 0, rows - NA_KROWS) + ku
    row_start = np.clip(r - NA_WIN_R // 2, 0, rows - NA_WIN_R)
    row_ok = (u >= row_start) & (u < row_start + NA_WIN_R)
    dr = np.where(row_ok, u - r + (NA_WIN_R - 1), -1)
    for g in range(2, n_grp - 1):
        assert np.array_equal(dr[g], dr[1])
    assert np.all((dc == cq[None, :] - cq[:, None] + NA_WIN_C - 1)[col_ok])
    rbp = jnp.pad(rel_bias.astype(F32) * LOG2E, ((0, 0), (0, 0), (GRID_W, GRID_W)))
    rbt = jnp.stack([rbp[:, :, GRID_W + NA_WIN_C - 1 - q:2 * GRID_W + NA_WIN_C - 1 - q] for q in range(GRID_W)],
                    axis=2)
    rbm = jnp.where(col_ok[None, None], rbt, -jnp.inf)
    ninf = jnp.full((rel_bias.shape[0], GRID_W, GRID_W), -jnp.inf, F32)
    pats = []
    for g in (0, 1, n_grp - 1):
        qrows = []
        for a in range(NA_GROUP):
            blocks = [rbm[:, dr[g, a, b]] if dr[g, a, b] >= 0 else ninf for b in range(NA_KROWS)]
            qrows.append(jnp.concatenate(blocks, axis=-1))
        pats.append(jnp.concatenate(qrows, axis=-2))
    return jnp.stack(pats, axis=1)


def _cumsum_rows(g, rev):
    r = lax.broadcasted_iota(jnp.int32, (CH, CH), 0)
    c = lax.broadcasted_iota(jnp.int32, (CH, CH), 1)
    tri = jnp.where((c >= r) if rev else (c <= r), 1.0, 0.0).astype(BF16)
    g1 = g.astype(BF16)
    r1 = g - g1.astype(F32)
    g2 = r1.astype(BF16)
    g3 = (r1 - g2.astype(F32)).astype(BF16)
    dot = lambda a: jnp.dot(tri, a, preferred_element_type=F32)
    return dot(g1) + dot(g2) + dot(g3)


def _block_rows(a, first, period):
    return jnp.concatenate(
        [jnp.broadcast_to(a[r:r + 1, :], (period, 128)) for r in range(first, CH, period)], axis=0)


def _gla_chunk(load, get_state, put, masks, rev):
    q, k, g, vs = load()
    nh = len(vs)
    b = _cumsum_rows(g * LOG2E, rev)
    b_end = b[0:1] if rev else b[CH - 1:CH]
    yield
    r2 = lax.broadcasted_iota(jnp.int32, (CH, CH), 0)
    c2 = lax.broadcasted_iota(jnp.int32, (CH, CH), 1)

    def headq(a, h):
        return a if masks[h] is None else jnp.where(masks[h], a, 0.0)

    atts = [jnp.zeros((CH, CH), F32) for _ in range(nh)]
    n = CH // 2
    while n >= SUB:
        first = n if rev else n - 1
        ref = _block_rows(b, first, 2 * n) if 2 * n < CH else b[first:first + 1]
        qn = (q * jnp.exp2(b - ref)).astype(BF16)
        kn = (k * jnp.exp2(ref - b)).astype(BF16)
        sh = (2 * n).bit_length() - 1
        same = (r2 >> sh) == (c2 >> sh)
        rin = r2 & (2 * n - 1)
        cin = c2 & (2 * n - 1)
        valid = (same & (rin < n) & (cin >= n)) if rev else (same & (rin >= n) & (cin < n))
        for h in range(nh):
            a = _nt(headq(qn, h), kn)
            atts[h] = jnp.where(valid, a, atts[h])
        n //= 2
        yield

    sub_sh = SUB.bit_length() - 1
    dvalid = ((r2 >> sub_sh) == (c2 >> sub_sh)) & ((c2 >= r2) if rev else (c2 <= r2))
    diag = [jnp.zeros((CH, CH), F32) for _ in range(nh)]
    for s in range(SUB):
        e = q * _block_rows(k, s, SUB) * jnp.exp2(b - _block_rows(b, s, SUB))
        hit = (c2 & (SUB - 1)) == s
        for h in range(nh):
            rs = jnp.sum(headq(e, h), axis=-1, keepdims=True)
            diag[h] = jnp.where(hit, rs, diag[h])
        yield

    qe = (q * jnp.exp2(b)).astype(BF16)
    kd = k * jnp.exp2(b_end - b)
    dec = jnp.exp2(b_end)
    yield
    sts = get_state()
    outs, new = [], []
    for h in range(nh):
        att = jnp.where(dvalid, diag[h], atts[h])
        vb = vs[h].astype(BF16)
        o = jnp.dot(att.astype(BF16), vb, preferred_element_type=F32) + _nt(qe, sts[h].astype(BF16))
        outs.append(o)
        new.append(sts[h] * dec + _tn(vb, headq(kd, h).astype(BF16)))
    put(outs, new)


SCAN_CHUNKS = 2
SCAN_ROWS = SCAN_CHUNKS * CH


def _chunk_maps(col_block):
    n_c = NCTX // SCAN_ROWS
    n_all = T // SCAN_ROWS

    def fwd(b, s):
        return (b, s, col_block)

    def bwd(b, s):
        return (b, jnp.where(s < n_c, n_c - 1 - s, n_all + n_c - 1 - s), col_block)

    return fwd, bwd


def _scan_order(rev):
    return tuple(reversed(range(SCAN_CHUNKS))) if rev else tuple(range(SCAN_CHUNKS))


def _hgrn_kernel(qf_ref, if_ref, ff_ref, qb_ref, ib_ref, fb_ref, lb_ref, of_ref, ob_ref, st_ref):
    @pl.when(pl.program_id(1) == 0)
    def _():
        st_ref[...] = jnp.zeros_like(st_ref)

    dirs = ((qf_ref, if_ref, ff_ref, of_ref), (qb_ref, ib_ref, fb_ref, ob_ref))
    gens, finals = [], []
    for d, (q_ref, i_ref, f_ref, o_ref) in enumerate(dirs):
        for h in range(4):
            sl = slice(128 * h, 128 * h + 128)
            state = [[st_ref[d, h]]]
            finals.append((d, h, state))
            for j in _scan_order(d == 1):
                rows = slice(CH * j, CH * j + CH)

                def load(q_ref=q_ref, i_ref=i_ref, f_ref=f_ref, d=d, sl=sl, rows=rows):
                    lb = lb_ref[d:d + 1, sl]
                    q = _silu(q_ref[0, rows, sl])
                    f = f_ref[0, rows, sl]
                    e = jnp.exp(-jnp.abs(f))
                    inv = 1.0 / (1.0 + e)
                    k = (1.0 - lb) * jnp.where(f >= 0.0, e * inv, inv)
                    a = jnp.log(lb)
                    c = jnp.log(1.0 - lb) + (jnp.minimum(f, 0.0) - jnp.log(1.0 + e))
                    g = jnp.maximum(a, c) + jnp.log(1.0 + jnp.exp(-jnp.abs(a - c)))
                    return q, k, g, [i_ref[0, rows, sl]]

                def put(outs, new, o_ref=o_ref, sl=sl, rows=rows, state=state):
                    o_ref[0, rows, sl] = outs[0]
                    state[0] = new

                gens.append(_gla_chunk(load, lambda state=state: state[0], put, [None], d == 1))
    _interleave(gens)
    for d, h, state in finals:
        st_ref[d, h] = state[0][0]


def _hgrn_call(p, lb):
    fq, bq = _chunk_maps(C_Q)
    fi, bi = _chunk_maps(C_I)
    ff, _ = _chunk_maps(C_FF)
    _, bf = _chunk_maps(C_FB)
    fo, bo = _chunk_maps(0)
    blk = lambda m: pl.BlockSpec((1, SCAN_ROWS, 512), m)
    return pl.pallas_call(
        _hgrn_kernel,
        grid=(NB, T // SCAN_ROWS),
        in_specs=[blk(fq), blk(fi), blk(ff), blk(bq), blk(bi), blk(bf),
                  pl.BlockSpec((2, 512), lambda b, s: (0, 0))],
        out_specs=[blk(fo), blk(bo)],
        out_shape=[jax.ShapeDtypeStruct((NB, T, 512), F32)] * 2,
        scratch_shapes=[pltpu.VMEM((2, 4, 128, 128), F32)],
        compiler_params=_cparams(("parallel", "arbitrary"), 32),
        name="hgrn2_scan",
    )(p, p, p, p, p, p, lb)


def _gla_kernel(qkf_ref, vf_ref, zf_ref, qkb_ref, vb_ref, zb_ref, w2_ref, b2_ref, of_ref, ob_ref, st_ref):
    @pl.when(pl.program_id(1) == 0)
    def _():
        st_ref[...] = jnp.zeros_like(st_ref)

    lane = lax.broadcasted_iota(jnp.int32, (1, 128), 1)
    masks = [lane < GLA_DK, lane >= GLA_DK]
    dirs = ((qkf_ref, vf_ref, zf_ref, of_ref), (qkb_ref, vb_ref, zb_ref, ob_ref))
    gens, finals = [], []
    for d, (qk_ref, v_ref, z_ref, o_ref) in enumerate(dirs):
        logit = jnp.dot(z_ref[0].astype(BF16), w2_ref[d].astype(BF16), preferred_element_type=F32) + b2_ref[d]
        g = _log_sigmoid(logit) * (1.0 / GLA_TAU)
        for grp in range(2):
            sl = slice(128 * grp, 128 * grp + 128)
            heads = (2 * grp, 2 * grp + 1)
            state = [[st_ref[d, h] for h in heads]]
            finals.append((d, heads, state))
            for j in _scan_order(d == 1):
                rows = slice(CH * j, CH * j + CH)

                def load(qk_ref=qk_ref, v_ref=v_ref, g=g, grp=grp, sl=sl, heads=heads, rows=rows):
                    q = qk_ref[0, rows, sl] * (GLA_DK ** -0.5)
                    k = qk_ref[0, rows, 256 + 128 * grp:256 + 128 * grp + 128]
                    return q, k, g[rows, sl], [v_ref[0, rows, 128 * h:128 * h + 128] for h in heads]

                def put(outs, new, o_ref=o_ref, heads=heads, rows=rows, state=state):
                    for o, h in zip(outs, heads):
                        o_ref[0, rows, 128 * h:128 * h + 128] = o
                    state[0] = new

                gens.append(_gla_chunk(load, lambda state=state: state[0], put, masks, d == 1))
    _interleave(gens)
    for d, heads, state in finals:
        for st, h in zip(state[0], heads):
            st_ref[d, h] = st


def _gla_call(p, pt, w2p, b2):
    fqk, bqk = _chunk_maps(D_QK)
    fv, bv = _chunk_maps(D_V)
    fz, bz = _chunk_maps(TAIL_Z)
    fo, bo = _chunk_maps(0)
    blk = lambda m: pl.BlockSpec((1, SCAN_ROWS, 512), m)
    zblk = lambda m: pl.BlockSpec((1, SCAN_ROWS, 128), m)
    return pl.pallas_call(
        _gla_kernel,
        grid=(NB, T // SCAN_ROWS),
        in_specs=[blk(fqk), blk(fv), zblk(fz), blk(bqk), blk(bv), zblk(bz),
                  pl.BlockSpec((2, 128, 256), lambda b, s: (0, 0, 0)),
                  pl.BlockSpec((2, 1, 256), lambda b, s: (0, 0, 0))],
        out_specs=[blk(fo), blk(bo)],
        out_shape=[jax.ShapeDtypeStruct((NB, T, 512), F32)] * 2,
        scratch_shapes=[pltpu.VMEM((2, 4, 128, 128), F32)],
        compiler_params=_cparams(("parallel", "arbitrary"), 32),
        name="gla_scan",
    )(p, p, pt, p, p, pt, w2p, b2)


def _gated_norm(o, gate, g):
    parts = []
    for h in range(4):
        sl = slice(128 * h, 128 * h + 128)
        oh = o[:, sl]
        ms = jnp.mean(oh * oh, axis=-1, keepdims=True)
        parts.append((oh * lax.rsqrt(ms + EPS) * g * _silu(gate[:, sl])).astype(BF16))
    return parts


def _cdpost_kernel(cf_ref, cb_ref, cg_ref, df_ref, db_ref, dr_ref, gc_ref, gd_ref, o_ref):
    for h, part in enumerate(_gated_norm(cf_ref[0] + cb_ref[0], cg_ref[0], gc_ref[...])):
        o_ref[0, :, 128 * h:128 * h + 128] = part
    for h, part in enumerate(_gated_norm(df_ref[0] + db_ref[0], dr_ref[0], gd_ref[...])):
        o_ref[0, :, 512 + 128 * h:512 + 128 * h + 128] = part


def _cdpost_call(cf, cb, df, db, p, pt, gc, gd):
    tm = 256
    row = lambda col: pl.BlockSpec((1, tm, 512), lambda b, i: (b, i, col))
    vec = pl.BlockSpec((1, 128), lambda b, i: (0, 0))
    return pl.pallas_call(
        _cdpost_kernel,
        grid=(NB, T // tm),
        in_specs=[row(0), row(0), row(C_G), row(0), row(0), row(TAIL_R), vec, vec],
        out_specs=pl.BlockSpec((1, tm, 1024), lambda b, i: (b, i, 0)),
        out_shape=jax.ShapeDtypeStruct((NB, T, 1024), BF16),
        compiler_params=_cparams(("parallel", "parallel"), 32),
        name="scan_post",
    )(cf, cb, p, df, db, pt, gc, gd)


def _swiglu_split(h, w1_refs, w3_refs, w2_refs, lead):
    get = (lambda r: r[...]) if lead is None else (lambda r: r[lead])
    kc = D // W_SPLIT
    u = v = None
    for k in range(W_SPLIT):
        hk = h[:, k * kc:(k + 1) * kc]
        pu = jnp.dot(hk, get(w1_refs[k]).astype(BF16), preferred_element_type=F32)
        pv = jnp.dot(hk, get(w3_refs[k]).astype(BF16), preferred_element_type=F32)
        u = pu if u is None else u + pu
        v = pv if v is None else v + pv
    g = (_silu(u) * v).astype(BF16)
    return [jnp.dot(g, get(w2_refs[k]).astype(BF16), preferred_element_type=F32) for k in range(W_SPLIT)]


def _ffn_kernel(h_ref, *refs):
    w1_refs, w3_refs, w2_refs = refs[:W_SPLIT], refs[W_SPLIT:2 * W_SPLIT], refs[2 * W_SPLIT:3 * W_SPLIT]
    o_ref = refs[3 * W_SPLIT]

    @pl.when(pl.program_id(2) == 0)
    def _():
        o_ref[...] = jnp.zeros_like(o_ref)

    nc = D // W_SPLIT
    for k, part in enumerate(_swiglu_split(h_ref[0], w1_refs, w3_refs, w2_refs, None)):
        o_ref[0, :, k * nc:(k + 1) * nc] += part


def _ffn_call(h, w1, w3, w2):
    tm, tf = 1152, 512
    kc = D // W_SPLIT
    up = [pl.BlockSpec((kc, tf), functools.partial(lambda b, i, f, k: (k, f), k=k)) for k in range(W_SPLIT)]
    down = [pl.BlockSpec((tf, kc), functools.partial(lambda b, i, f, k: (f, k), k=k)) for k in range(W_SPLIT)]
    return pl.pallas_call(
        _ffn_kernel,
        grid=(NB, T // tm, FFN_DENSE // tf),
        in_specs=[pl.BlockSpec((1, tm, D), lambda b, i, f: (b, i, 0))] + up + up + down,
        out_specs=pl.BlockSpec((1, tm, D), lambda b, i, f: (b, i, 0), pipeline_mode=pl.Buffered(1)),
        out_shape=jax.ShapeDtypeStruct((NB, T, D), F32),
        compiler_params=_cparams(("parallel", "parallel", "arbitrary"), 56),
        name="dense_ffn",
    )(h, *([w1] * W_SPLIT), *([w3] * W_SPLIT), *([w2] * W_SPLIT))


MOE_TM = 1152
MOE_SUB = 384
MOE_TF = 512
MOE_TILES = -(-(2 * NB * SEQ + N_EXPERTS * (MOE_TM - 1)) // MOE_TM)
MOE_ROWS = MOE_TILES * MOE_TM
MOE_PREFETCH_STEPS = 8
MOE_PREFETCH_ROWS = MOE_TM // MOE_PREFETCH_STEPS


def _route_kernel(x_ref, g_ref, sh_ref, sc_ref, wr_ref, h_ref, r_ref):
    x = x_ref[0]
    ms = jnp.mean(x * x, axis=-1, keepdims=True)
    h = x * lax.rsqrt(ms + EPS) * g_ref[...] * (1.0 + sc_ref[0]) + sh_ref[0]
    h_ref[0] = h
    logits = jnp.dot(h, wr_ref[...], preferred_element_type=F32, precision=lax.Precision.HIGHEST)
    lane = lax.broadcasted_iota(jnp.int32, logits.shape, 1)
    lanef = lane.astype(F32)
    lg = jnp.where(lane < N_EXPERTS, logits, -jnp.inf)
    m1 = jnp.max(lg, axis=-1, keepdims=True)
    i1 = jnp.min(jnp.where(lg == m1, lanef, 128.0), axis=-1, keepdims=True)
    lg2 = jnp.where(lanef == i1, -jnp.inf, lg)
    m2 = jnp.max(lg2, axis=-1, keepdims=True)
    i2 = jnp.min(jnp.where(lg2 == m2, lanef, 128.0), axis=-1, keepdims=True)
    e = jnp.exp(m2 - m1)
    w1 = 1.0 / (1.0 + e)
    w2 = e * w1
    r_ref[0] = jnp.where(lane == 0, i1, jnp.where(lane == 1, i2, jnp.where(lane == 2, w1,
                         jnp.where(lane == 3, w2, 0.0))))


def _route_call(xall, g, mods, wr):
    tm = 256
    return pl.pallas_call(
        _route_kernel,
        grid=(NB, SEQ // tm),
        in_specs=[
            pl.BlockSpec((1, tm, D), lambda b, i: (b, i + NCTX // tm, 0)),
            pl.BlockSpec((1, D), lambda b, i: (0, 0)),
            _mod_spec(3, False), _mod_spec(4, False),
            pl.BlockSpec((D, 128), lambda b, i: (0, 0)),
        ],
        out_specs=[pl.BlockSpec((1, tm, D), lambda b, i: (b, i, 0)),
                   pl.BlockSpec((1, tm, 128), lambda b, i: (b, i, 0))],
        out_shape=[jax.ShapeDtypeStruct((NB, SEQ, D), F32), jax.ShapeDtypeStruct((NB, SEQ, 128), F32)],
        compiler_params=_cparams(("parallel", "parallel"), 32),
        name="moe_route",
    )(xall, g, mods, mods, wr)


def _moe_kernel(te_ref, nl_ref, src_ref, h_ref, *refs):
    w1_refs, w3_refs, w2_refs = refs[:W_SPLIT], refs[W_SPLIT:2 * W_SPLIT], refs[2 * W_SPLIT:3 * W_SPLIT]
    o_ref, xg_ref, xb_ref, sem = refs[3 * W_SPLIT:]
    i = pl.program_id(0)
    f = pl.program_id(1)
    n_live = nl_ref[i]
    nxt_live = nl_ref[jnp.minimum(i + 1, MOE_TILES - 1)]
    fetch = (i + 1 < MOE_TILES) & (nxt_live > 0) & (f >= 1) & (f <= MOE_PREFETCH_STEPS)

    def row_copy(tile, r):
        row = src_ref[tile * MOE_TM + r]
        return pltpu.make_async_copy(h_ref.at[pl.ds(row, 1), :], xg_ref.at[pl.ds(r, 1), :], sem)

    @pl.when(f == 0)
    def _():
        o_ref[...] = jnp.zeros_like(o_ref)

    @pl.when((i == 0) & (f == 0) & (n_live > 0))
    def _():
        def body(r, c):
            row_copy(0, r).start()
            return c

        lax.fori_loop(0, MOE_TM, body, 0)

    @pl.when((f == 0) & (n_live > 0))
    def _():
        pltpu.make_async_copy(h_ref.at[pl.ds(0, MOE_TM), :], xg_ref, sem).wait()
        xb_ref[...] = xg_ref[...].astype(BF16)

    def step(n, do_fetch):
        def body():
            if do_fetch:
                r0 = (f - 1) * MOE_PREFETCH_ROWS
                for r in range(MOE_PREFETCH_ROWS):
                    row_copy(i + 1, r0 + r).start()
            rows = slice(0, n * MOE_SUB)
            nc = D // W_SPLIT
            for k, part in enumerate(_swiglu_split(xb_ref[rows, :], w1_refs, w3_refs, w2_refs, 0)):
                o_ref[rows, k * nc:(k + 1) * nc] += part

        return body

    for n in range(1, MOE_TM // MOE_SUB + 1):
        for do_fetch in (False, True):
            pl.when((n_live == n) & (fetch == do_fetch))(step(n, do_fetch))


def _moe_call(tile_expert, tile_live, src_rows, h, w1, w3, w2):
    nf = FFN_EXPERT // MOE_TF

    def fidx(i, f, nl):
        return jnp.where(nl[i] > 0, f, nf - 1)

    kc = D // W_SPLIT
    up = [pl.BlockSpec((1, kc, MOE_TF), functools.partial(lambda i, f, te, nl, src, k: (te[i], k, fidx(i, f, nl)), k=k))
          for k in range(W_SPLIT)]
    down = [pl.BlockSpec((1, MOE_TF, kc), functools.partial(lambda i, f, te, nl, src, k: (te[i], fidx(i, f, nl), k), k=k))
            for k in range(W_SPLIT)]
    return pl.pallas_call(
        _moe_kernel,
        grid_spec=pltpu.PrefetchScalarGridSpec(
            num_scalar_prefetch=3,
            grid=(MOE_TILES, nf),
            in_specs=[pl.BlockSpec(memory_space=pl.ANY)] + up + up + down,
            out_specs=pl.BlockSpec((MOE_TM, D), lambda i, f, te, nl, src: (i, 0), pipeline_mode=pl.Buffered(1)),
            scratch_shapes=[pltpu.VMEM((MOE_TM, D), F32), pltpu.VMEM((MOE_TM, D), BF16),
                            pltpu.SemaphoreType.DMA(())],
        ),
        out_shape=jax.ShapeDtypeStruct((MOE_ROWS, D), F32),
        compiler_params=_cparams(("arbitrary", "arbitrary"), 58),
        name="moe_experts",
    )(tile_expert, tile_live, src_rows, h, *([w1] * W_SPLIT), *([w3] * W_SPLIT), *([w2] * W_SPLIT))


def _row_copy(src_hbm, row, dst_ref, r, sem):
    return pltpu.make_async_copy(src_hbm.at[pl.ds(row, 1), :], dst_ref.at[pl.ds(r, 1), :], sem)


def _combine_kernel(dest_ref, x_ref, r_ref, gm_ref, gf_ref, y_ref, o_ref, yb_ref, sem, *, tc, n_steps):
    step = pl.program_id(0) * (SEQ // tc) + pl.program_id(1)
    slot = lax.rem(step, 2)

    def start_rows(step_, slot_):
        def issue(r, c):
            tok = step_ * tc + r
            _row_copy(y_ref, dest_ref[2 * tok], yb_ref.at[slot_, 0], r, sem.at[slot_]).start()
            _row_copy(y_ref, dest_ref[2 * tok + 1], yb_ref.at[slot_, 1], r, sem.at[slot_]).start()
            return c

        lax.fori_loop(0, tc, issue, 0, unroll=8)

    @pl.when(step == 0)
    def _():
        start_rows(0, 0)

    @pl.when(step + 1 < n_steps)
    def _():
        start_rows(step + 1, 1 - slot)

    for j in range(2):
        pltpu.make_async_copy(y_ref.at[pl.ds(0, tc), :], yb_ref.at[slot, j], sem.at[slot]).wait()
    rt = r_ref[0]
    moe = rt[:, 2:3] * yb_ref[slot, 0] + rt[:, 3:4] * yb_ref[slot, 1]
    x = x_ref[0] + gm_ref[0] * moe
    ms = jnp.mean(x * x, axis=-1, keepdims=True)
    o_ref[0] = x * lax.rsqrt(ms + EPS) * gf_ref[...]


def _combine_call(dest, xall, route, mods, gfinal, y):
    tc = 256
    return pl.pallas_call(
        functools.partial(_combine_kernel, tc=tc, n_steps=NB * SEQ // tc),
        grid_spec=pltpu.PrefetchScalarGridSpec(
            num_scalar_prefetch=1,
            grid=(NB, SEQ // tc),
            in_specs=[
                pl.BlockSpec((1, tc, D), lambda b, i, d: (b, i + NCTX // tc, 0)),
                pl.BlockSpec((1, tc, 128), lambda b, i, d: (b, i, 0)),
                pl.BlockSpec((1, 1, D), lambda b, i, d: (b * 6 + 5, 0, 0)),
                pl.BlockSpec((1, D), lambda b, i, d: (0, 0)),
                pl.BlockSpec(memory_space=pl.ANY),
            ],
            out_specs=pl.BlockSpec((1, tc, D), lambda b, i, d: (b, i, 0)),
            scratch_shapes=[pltpu.VMEM((2, 2, tc, D), F32), pltpu.SemaphoreType.DMA((2,))],
        ),
        out_shape=jax.ShapeDtypeStruct((NB, SEQ, D), F32),
        compiler_params=_cparams(("arbitrary", "arbitrary"), 32),
        name="moe_combine",
    )(dest, xall, route, mods, gfinal, y)


def _moe_plan(route):
    n_pairs = 2 * NB * SEQ
    e = route[..., 0:2].astype(jnp.int32).reshape(n_pairs)
    onehot = (e[:, None] == jnp.arange(N_EXPERTS, dtype=jnp.int32)[None, :]).astype(jnp.int32)
    csum = jnp.cumsum(onehot, axis=0)
    rank = jnp.sum(onehot * csum, axis=1) - 1
    counts = csum[-1]
    padded = ((counts + MOE_TM - 1) // MOE_TM) * MOE_TM
    ends = jnp.cumsum(padded)
    starts = ends - padded
    dest = (starts[e] + rank).astype(jnp.int32)
    src_rows = jnp.zeros((MOE_ROWS,), jnp.int32).at[dest].set(jnp.arange(n_pairs, dtype=jnp.int32) // 2)
    tile_start = jnp.arange(MOE_TILES, dtype=jnp.int32) * MOE_TM
    t_eff = jnp.minimum(tile_start, ends[-1] - MOE_TM)
    tile_expert = jnp.sum((ends[None, :] <= t_eff[:, None]).astype(jnp.int32), axis=1)
    rows_left = (starts + counts)[tile_expert] - tile_start
    tile_live = jnp.clip((rows_left + MOE_SUB - 1) // MOE_SUB, 0, MOE_TM // MOE_SUB).astype(jnp.int32)
    return dest, src_rows, tile_expert.astype(jnp.int32), tile_live


def _rope_tables():
    per_axis = DIFF_QK // 4
    t = np.arange(SEQ)
    inv = ROPE_BASE ** (-np.arange(per_axis, dtype=np.float32) / per_axis)
    ang = np.concatenate([(t // GRID_W).astype(np.float32)[:, None] * inv,
                          (t % GRID_W).astype(np.float32)[:, None] * inv], axis=-1).astype(np.float32)
    cos = np.concatenate([np.ones((NCTX, 2 * per_axis), np.float32), np.cos(ang)], axis=0)
    sin = np.concatenate([np.zeros((NCTX, 2 * per_axis), np.float32), np.sin(ang)], axis=0)
    reps = 128 // (2 * per_axis)
    return jnp.asarray(np.tile(cos, (1, reps))), jnp.asarray(np.tile(sin, (1, reps)))


def _in_proj_tail(w_t):
    z = w_t[:, IN_MAIN:IN_MAIN + 2 * GLA_RANK]
    r = w_t[:, IN_MAIN + 2 * GLA_RANK:]
    pad = jnp.zeros((w_t.shape[0], TAIL_W - r.shape[1] - z.shape[1], D), w_t.dtype)
    return jnp.concatenate([r, z, pad], axis=1)


def kernel(x, c, ctx, c_ctx, ada_w, ada_b, norm_mix, norm_ffn, w_in, w_out, diff_lambda, diff_norm,
           na_rel_bias, hgrn_lower_bounds, hgrn_norm, gla_gate_w2, gla_gate_b, gla_norm,
           ffn_w1, ffn_w3, ffn_w2, moe_router, moe_w1, moe_w3, moe_w2, final_norm):
    lb_soft = jax.nn.softmax(hgrn_lower_bounds.astype(F32), axis=1)
    lower_bounds = jnp.clip(jnp.cumsum(lb_soft, axis=1) - lb_soft[:, :1], 0.0, 1.0 - 1e-6)
    cond8 = jnp.concatenate([c, c_ctx[None, :], jnp.zeros((8 - NB - 1, D), F32)], axis=0)
    mods_all = _ada_call(cond8, ada_w, ada_b).reshape(DEPTH, 8 * 6, 1, D)
    cos, sin = _rope_tables()
    xall = jnp.concatenate([ctx, x], axis=1)
    w_in_t = jnp.swapaxes(w_in, 1, 2)
    w_tail_t = _in_proj_tail(w_in_t)

    out = None
    pending = None
    for l in range(DEPTH):
        mods = mods_all[l]
        if pending is None:
            h = _norm_call(xall, norm_mix[l][None, :], mods, 0, 1)
        else:
            xall, h = _norm_call(xall, norm_mix[l][None, :], mods, 0, 1, resid=pending)
            pending = None
        p = _wsmm_call([h], w_in_t, l, 1664, IN_MAIN // 1664, w_t=True, vmem_mb=56, name="in_proj")
        pt = _wsmm_call([h], w_tail_t, l, TAIL_W, 1, w_t=True, vmem_mb=32, name="in_proj_tail")

        lambda_init = 0.8 - 0.6 * math.exp(-0.3 * l)
        lp = diff_lambda[l].astype(F32)
        lam = (jnp.exp(jnp.sum(lp[0] * lp[1])) - jnp.exp(jnp.sum(lp[2] * lp[3])) + lambda_init).reshape(1)
        a = _attn_a_call(p, lam, cos, sin, jnp.tile(diff_norm[l], 2)[None, :], 1.0 - lambda_init)
        n = _attn_na_call(p, _na_bias_table(na_rel_bias[l]))
        cf, cb = _hgrn_call(p, lower_bounds[:, l])
        w2p = jnp.zeros((2, 128, 4 * GLA_DK), F32)
        w2p = w2p.at[0, 0:GLA_RANK].set(gla_gate_w2[l, 0]).at[1, GLA_RANK:2 * GLA_RANK].set(gla_gate_w2[l, 1])
        df, db = _gla_call(p, pt, w2p, gla_gate_b[l][:, None, :])
        cd = _cdpost_call(cf, cb, df, db, p, pt, hgrn_norm[l][None, :], gla_norm[l][None, :])
        xall = _wsmm_call([a, n, cd], w_out, l, 1024, D // 1024, resid=(xall, mods, 2), name="out_proj")

        if l % 2 == 0:
            h2 = _norm_call(xall, norm_ffn[l][None, :], mods, 3, 4)
            y = _ffn_call(h2, ffn_w1[l // 2], ffn_w3[l // 2], ffn_w2[l // 2])
            pending = (y, mods, 5)
        else:
            assert l == DEPTH - 1
            wr = jnp.zeros((D, 128), F32).at[:, :N_EXPERTS].set(moe_router[l // 2])
            hr, route = _route_call(xall, norm_ffn[l][None, :], mods, wr)
            dest, src_rows, tile_expert, tile_live = _moe_plan(route)
            y = _moe_call(tile_expert, tile_live, src_rows, hr.reshape(NB * SEQ, D),
                          moe_w1[l // 2], moe_w3[l // 2], moe_w2[l // 2])
            out = _combine_call(dest, xall, route, mods, final_norm[None, :], y)
    return out
```

```python
import functools
import math

import numpy as np
import jax
import jax.numpy as jnp
from jax import lax
from jax.experimental import pallas as pl
from jax.experimental.pallas import tpu as pltpu

F32 = jnp.float32
BF16 = jnp.bfloat16

D = 2048
NB = 4
SEQ = 2048
NCTX = 256
T = NCTX + SEQ
DEPTH = 2
GRID_W = 64
ROPE_BASE = 10000.0
EPS = 1e-6
LOG2E = math.log2(math.e)

DIFF_QK = 32
NA_DIM = 64
NA_WIN_R = 8
NA_WIN_C = 16
NA_GROUP = 4
NA_KROWS = NA_GROUP + NA_WIN_R - 1
GLA_DK = 64
GLA_TAU = 16.0
GLA_RANK = 16
CH = 64
SUB = 8

FFN_DENSE = 5632
N_EXPERTS = 8
FFN_EXPERT = 7168

IN_MAIN = 6656
A_Q, A_K, A_V = 0, 4, 8
B_Q, B_K, B_V = 12, 16, 20
C_Q, C_I, C_FF, C_FB, C_G = 6, 7, 8, 9, 10
D_QK, D_V = 11, 12
TAIL_W = 640
TAIL_R, TAIL_Z = 0, 4

V7X_VMEM_BYTES = 64 * 1024 * 1024
W_SPLIT = 1


def _cparams(sem, vmem_mb):
    assert vmem_mb * 1024 * 1024 < V7X_VMEM_BYTES
    return pltpu.CompilerParams(dimension_semantics=sem, vmem_limit_bytes=vmem_mb * 1024 * 1024)


def _sigmoid(x):
    return 1.0 / (1.0 + jnp.exp(-x))


def _silu(x):
    return x * _sigmoid(x)


def _log_sigmoid(x):
    return jnp.minimum(x, 0.0) - jnp.log(1.0 + jnp.exp(-jnp.abs(x)))


def _nt(a, b):
    return lax.dot_general(a, b, (((1,), (1,)), ((), ())), preferred_element_type=F32)


def _tn(a, b):
    return lax.dot_general(a, b, (((0,), (0,)), ((), ())), preferred_element_type=F32)


def _interleave(gens):
    live = list(gens)
    while live:
        still = []
        for gen in live:
            try:
                next(gen)
                still.append(gen)
            except StopIteration:
                pass
        live = still


def _ada_kernel(c_ref, *refs):
    w_refs, b_ref, o_ref = refs[:W_SPLIT], refs[W_SPLIT], refs[W_SPLIT + 1]
    s = _silu(c_ref[...]).astype(BF16)
    kc = D // W_SPLIT
    acc = b_ref[0]
    for k in range(W_SPLIT):
        acc = acc + jnp.dot(s[:, k * kc:(k + 1) * kc], w_refs[k][0].astype(BF16), preferred_element_type=F32)
    o_ref[0] = acc


def _ada_call(cond8, ada_w, ada_b):
    tn = 1536
    n = 6 * D
    kc = D // W_SPLIT
    w_specs = [pl.BlockSpec((1, kc, tn), functools.partial(lambda l, j, k: (l, k, j), k=k)) for k in range(W_SPLIT)]
    return pl.pallas_call(
        _ada_kernel,
        grid=(DEPTH, n // tn),
        in_specs=[pl.BlockSpec((8, D), lambda l, j: (0, 0))] + w_specs
        + [pl.BlockSpec((1, 1, tn), lambda l, j: (l, 0, j))],
        out_specs=pl.BlockSpec((1, 8, tn), lambda l, j: (l, 0, j)),
        out_shape=jax.ShapeDtypeStruct((DEPTH, 8, n), F32),
        compiler_params=_cparams(("parallel", "parallel"), 40),
        name="ada_mod",
    )(cond8, *([ada_w] * W_SPLIT), ada_b.reshape(DEPTH, 1, n))


def _mod_spec(k, ctx):
    if ctx:
        return pl.BlockSpec((1, 1, D), lambda b, i: (NB * 6 + k, 0, 0))
    return pl.BlockSpec((1, 1, D), lambda b, i: (b * 6 + k, 0, 0))


def _row_is_ctx(row0, tm):
    return (row0 + lax.broadcasted_iota(jnp.int32, (tm, 1), 0)) < NCTX


def _modnorm(x, g, is_c, shl, scl, shc, scc):
    ms = jnp.mean(x * x, axis=-1, keepdims=True)
    y = x * lax.rsqrt(ms + EPS) * g
    sc = jnp.where(is_c, scc, scl)
    sh = jnp.where(is_c, shc, shl)
    return y * (1.0 + sc) + sh


def _norm_kernel(*refs, tm, resid):
    if resid:
        x_ref, y_ref, gl_ref, gx_ref, g_ref, shl_ref, scl_ref, shc_ref, scc_ref, xo_ref, h_ref = refs
    else:
        x_ref, g_ref, shl_ref, scl_ref, shc_ref, scc_ref, h_ref = refs
    is_c = _row_is_ctx(pl.program_id(1) * tm, tm)
    x = x_ref[0]
    if resid:
        x = x + jnp.where(is_c, gx_ref[0], gl_ref[0]) * y_ref[0]
        xo_ref[0] = x
    h_ref[0] = _modnorm(x, g_ref[...], is_c, shl_ref[0], scl_ref[0], shc_ref[0], scc_ref[0]).astype(BF16)


def _norm_call(xall, g, mods, k_shift, k_scale, resid=None):
    tm = 384
    row = pl.BlockSpec((1, tm, D), lambda b, i: (b, i, 0))
    in_specs, args = [row], [xall]
    if resid is not None:
        y, mods_prev, k_gate = resid
        in_specs += [row, _mod_spec(k_gate, False), _mod_spec(k_gate, True)]
        args += [y, mods_prev, mods_prev]
    in_specs += [pl.BlockSpec((1, D), lambda b, i: (0, 0)),
                 _mod_spec(k_shift, False), _mod_spec(k_scale, False),
                 _mod_spec(k_shift, True), _mod_spec(k_scale, True)]
    args += [g, mods, mods, mods, mods]
    h_shape = jax.ShapeDtypeStruct((NB, T, D), BF16)
    out_shape, out_specs = h_shape, row
    if resid is not None:
        out_shape, out_specs = [jax.ShapeDtypeStruct((NB, T, D), F32), h_shape], [row, row]
    return pl.pallas_call(
        functools.partial(_norm_kernel, tm=tm, resid=resid is not None),
        grid=(NB, T // tm),
        in_specs=in_specs, out_specs=out_specs, out_shape=out_shape,
        compiler_params=_cparams(("parallel", "parallel"), 32),
        name="mod_norm",
    )(*args)


def _wsmm_kernel(*refs, ksizes, tm, epilogue, w_t):
    n_in = len(ksizes)
    ins, w_ref = refs[:n_in], refs[n_in]
    o_ref, wb_ref = refs[-2], refs[-1]

    @pl.when((pl.program_id(1) == 0) & (pl.program_id(2) == 0))
    def _():
        wb_ref[...] = w_ref[0].astype(BF16)

    acc, k0 = None, 0
    for r, ks in zip(ins, ksizes):
        if w_t:
            part = _nt(r[0], wb_ref[:, k0:k0 + ks])
        else:
            part = jnp.dot(r[0], wb_ref[k0:k0 + ks, :], preferred_element_type=F32)
        acc = part if acc is None else acc + part
        k0 += ks
    if epilogue:
        x_ref, gl_ref, gx_ref = refs[n_in + 1:n_in + 4]
        gate = jnp.where(_row_is_ctx(pl.program_id(2) * tm, tm), gx_ref[0], gl_ref[0])
        acc = x_ref[0] + gate * acc
    o_ref[0] = acc


def _wsmm_call(acts, w, layer, tn, n_j, resid=None, w_t=False, vmem_mb=48, name="matmul"):
    tm = 768
    ksizes = tuple(a.shape[-1] for a in acts)
    kdim = sum(ksizes)
    in_specs = [pl.BlockSpec((1, tm, ks), lambda j, b, i: (b, i, 0)) for ks in ksizes]
    if w_t:
        in_specs.append(pl.BlockSpec((1, tn, kdim), lambda j, b, i: (layer, j, 0)))
    else:
        in_specs.append(pl.BlockSpec((1, kdim, tn), lambda j, b, i: (layer, 0, j)))
    args = list(acts) + [w]
    if resid is not None:
        x, mods, k_gate = resid
        in_specs += [pl.BlockSpec((1, tm, tn), lambda j, b, i: (b, i, j)),
                     pl.BlockSpec((1, 1, tn), lambda j, b, i: (b * 6 + k_gate, 0, j)),
                     pl.BlockSpec((1, 1, tn), lambda j, b, i: (NB * 6 + k_gate, 0, j))]
        args += [x, mods, mods]
    return pl.pallas_call(
        functools.partial(_wsmm_kernel, ksizes=ksizes, tm=tm, epilogue=resid is not None, w_t=w_t),
        grid=(n_j, NB, T // tm),
        in_specs=in_specs,
        out_specs=pl.BlockSpec((1, tm, tn), lambda j, b, i: (b, i, j)),
        out_shape=jax.ShapeDtypeStruct((NB, T, n_j * tn), F32),
        scratch_shapes=[pltpu.VMEM((tn, kdim) if w_t else (kdim, tn), BF16)],
        compiler_params=_cparams(("arbitrary", "arbitrary", "arbitrary"), vmem_mb),
        name=name,
    )(*args)


def _rope(x, cos, sin):
    lane = lax.broadcasted_iota(jnp.int32, x.shape, 1)
    first = (lane & (DIFF_QK - 1)) < (DIFF_QK // 2)
    rot = jnp.where(first, -pltpu.roll(x, 128 - DIFF_QK // 2, 1), pltpu.roll(x, DIFF_QK // 2, 1))
    return x * cos + rot * sin


def _attn_a_kernel(lam_ref, q_ref, k_ref, v_ref, cos_ref, sin_ref, g_ref, o_ref, ks_ref, va_ref, *, post_scale):
    t = pl.program_id(2)
    lane = lax.broadcasted_iota(jnp.int32, (1, 128), 1)
    ones_lane = (64, 0)

    @pl.when(t == 0)
    def _():
        ks_ref[...] = _rope(k_ref[0], cos_ref[...], sin_ref[...]).astype(BF16)
        v = v_ref[0]
        va_ref[0] = jnp.where(lane < 64, v, jnp.where(lane == ones_lane[0], 1.0, 0.0)).astype(BF16)
        va_ref[1] = jnp.where(lane >= 64, v, jnp.where(lane == ones_lane[1], 1.0, 0.0)).astype(BF16)

    lam = lam_ref[0]

    def attend(q0, nq, nk):
        rows = pl.ds(q0, nq)
        q = _rope(q_ref[0, rows, :], cos_ref[rows, :], sin_ref[rows, :]) * (DIFF_QK ** -0.5 * LOG2E)
        kk = ks_ref[0:nk, :]
        terms = {}

        def softmax_pv(hh, m):
            lo = 64 * hh + DIFF_QK * m
            qm = jnp.where((lane >= lo) & (lane < lo + DIFF_QK), q, 0.0).astype(BF16)
            s = _nt(qm, kk)
            yield
            mx = jnp.max(s, axis=-1, keepdims=True)
            yield
            e = jnp.exp2((s - mx).astype(BF16))
            yield
            num = jnp.dot(e, va_ref[hh, 0:nk, :], preferred_element_type=F32)
            yield
            den = jnp.sum(jnp.where(lane == ones_lane[hh], num, 0.0), axis=-1, keepdims=True)
            terms[hh, m] = num * ((1.0 if m == 0 else lam) / den)

        _interleave([softmax_pv(hh, m) for hh in range(2) for m in range(2)])
        outs = [terms[hh, 0] - terms[hh, 1] for hh in range(2)]
        o = jnp.where(lane < 64, outs[0], outs[1])
        sq = o * o
        s0 = jnp.sum(jnp.where(lane < 64, sq, 0.0), axis=-1, keepdims=True)
        s1 = jnp.sum(jnp.where(lane >= 64, sq, 0.0), axis=-1, keepdims=True)
        ms = jnp.where(lane < 64, s0, s1) * (1.0 / 64.0)
        o_ref[0, rows, :] = (o * lax.rsqrt(ms + EPS) * (g_ref[...] * post_scale)).astype(o_ref.dtype)

    @pl.when(t == 0)
    def _():
        attend(0, NCTX, NCTX)

    @pl.when(t > 0)
    def _():
        attend(pl.multiple_of(NCTX + (t - 1) * ATTN_TQ, ATTN_TQ // 2), ATTN_TQ, T)


ATTN_TQ = 512


def _attn_a_call(p, lam, cos, sin, g2, post_scale):
    full = lambda col: pl.BlockSpec((1, T, 128), lambda b, h, t: (b, 0, col + h))
    tab = pl.BlockSpec((T, 128), lambda b, h, t: (0, 0))
    return pl.pallas_call(
        functools.partial(_attn_a_kernel, post_scale=post_scale),
        grid=(NB, 4, 1 + SEQ // ATTN_TQ),
        in_specs=[pl.BlockSpec(memory_space=pltpu.SMEM), full(A_Q), full(A_K), full(A_V), tab, tab,
                  pl.BlockSpec((1, 128), lambda b, h, t: (0, 0))],
        out_specs=pl.BlockSpec((1, T, 128), lambda b, h, t: (b, 0, h)),
        out_shape=jax.ShapeDtypeStruct((NB, T, 512), BF16),
        scratch_shapes=[pltpu.VMEM((T, 128), BF16), pltpu.VMEM((2, T, 128), BF16)],
        compiler_params=_cparams(("parallel", "parallel", "arbitrary"), 48),
        name="diff_attn",
    )(lam, p, p, p, cos, sin, g2)


def _attn_na_kernel(q_ref, k_ref, v_ref, bias_ref, o_ref):
    g = pl.program_id(1)
    lane = lax.broadcasted_iota(jnp.int32, (1, 128), 1)

    def run(chain):
        results = {}
        gens = []
        for p in range(4):
            sl = slice(128 * p, 128 * p + 128)
            q = q_ref[0, :, sl] * (NA_DIM ** -0.5 * LOG2E)
            for hh in range(2):
                qm = jnp.where((lane >= 64 * hh) & (lane < 64 * hh + 64), q, 0.0).astype(BF16)
                gens.append(chain(2 * p + hh, qm, sl, results))
        _interleave(gens)
        for p in range(4):
            o_ref[0, :, 128 * p:128 * p + 128] = jnp.where(lane < 64, results[2 * p], results[2 * p + 1]
                                                           ).astype(o_ref.dtype)

    @pl.when(g == 0)
    def _():
        def ctx_head(h, qm, sl, results):
            s = _nt(qm, k_ref[0, 0:NCTX, sl].astype(BF16))
            yield
            e = jnp.exp2(s - jnp.max(s, axis=-1, keepdims=True))
            yield
            r = 1.0 / jnp.sum(e, axis=-1, keepdims=True)
            results[h] = jnp.dot(e.astype(BF16), v_ref[0, 0:NCTX, sl].astype(BF16), preferred_element_type=F32) * r

        run(ctx_head)

    @pl.when(g > 0)
    def _():
        u0 = jnp.clip(NA_GROUP * (g - 1) - NA_WIN_R // 2, 0, SEQ // GRID_W - NA_KROWS)
        win = pl.ds(pl.multiple_of(NCTX + GRID_W * u0, GRID_W), NA_KROWS * GRID_W)

        def lat_head(h, qm, sl, results):
            sw = _nt(qm, k_ref[0, win, sl].astype(BF16)) + bias_ref[h, 0]
            sc = _nt(qm, k_ref[0, 0:NCTX, sl].astype(BF16))
            yield
            mx = jnp.maximum(jnp.max(sw, axis=-1, keepdims=True), jnp.max(sc, axis=-1, keepdims=True))
            yield
            ew = jnp.exp2(sw - mx)
            ec = jnp.exp2(sc - mx)
            yield
            r = 1.0 / (jnp.sum(ew, axis=-1, keepdims=True) + jnp.sum(ec, axis=-1, keepdims=True))
            o = (jnp.dot(ew.astype(BF16), v_ref[0, win, sl].astype(BF16), preferred_element_type=F32)
                 + jnp.dot(ec.astype(BF16), v_ref[0, 0:NCTX, sl].astype(BF16), preferred_element_type=F32))
            yield
            results[h] = o * r

        run(lat_head)


def _attn_na_call(p, bias):
    tq = NA_GROUP * GRID_W
    nk = NA_KROWS * GRID_W
    n_grp = SEQ // tq

    def bias_map(b, g):
        grp = jnp.maximum(g - 1, 0)
        return (0, jnp.where(grp == 0, 0, jnp.where(grp == n_grp - 1, 2, 1)), 0, 0)

    return pl.pallas_call(
        _attn_na_kernel,
        grid=(NB, T // tq),
        in_specs=[
            pl.BlockSpec((1, tq, 512), lambda b, g: (b, g, B_Q // 4)),
            pl.BlockSpec((1, T, 512), lambda b, g: (b, 0, B_K // 4)),
            pl.BlockSpec((1, T, 512), lambda b, g: (b, 0, B_V // 4)),
            pl.BlockSpec((8, 1, tq, nk), bias_map),
        ],
        out_specs=pl.BlockSpec((1, tq, 512), lambda b, g: (b, g, 0)),
        out_shape=jax.ShapeDtypeStruct((NB, T, 512), BF16),
        compiler_params=_cparams(("parallel", "arbitrary"), 48),
        name="nbr_attn",
    )(p, p, p, bias)


def _na_bias_table(rel_bias):
    rows = SEQ // GRID_W
    n_grp = rows // NA_GROUP
    cq = np.arange(GRID_W)
    col_start = np.clip(cq - NA_WIN_C // 2, 0, GRID_W - NA_WIN_C)
    col_ok = (cq[None, :] >= col_start[:, None]) & (cq[None, :] < col_start[:, None] + NA_WIN_C)
    dc = np.clip(cq[None, :] - cq[:, None], -(NA_WIN_C - 1), NA_WIN_C - 1) + (NA_WIN_C - 1)
    gi = np.arange(n_grp)[:, None, None]
    qr = np.arange(NA_GROUP)[None, :, None]
    ku = np.arange(NA_KROWS)[None, None, :]
    r = NA_GROUP * gi + qr
    u = np.clip(NA_GROUP * gi - NA_WIN_R // 2, 0, rows - NA_KROWS) + ku
    row_start = np.clip(r - NA_WIN_R // 2, 0, rows - NA_WIN_R)
    row_ok = (u >= row_start) & (u < row_start + NA_WIN_R)
    dr = np.where(row_ok, u - r + (NA_WIN_R - 1), -1)
    for g in range(2, n_grp - 1):
        assert np.array_equal(dr[g], dr[1])
    assert np.all((dc == cq[None, :] - cq[:, None] + NA_WIN_C - 1)[col_ok])
    rbp = jnp.pad(rel_bias.astype(F32) * LOG2E, ((0, 0), (0, 0), (GRID_W, GRID_W)))
    rbt = jnp.stack([rbp[:, :, GRID_W + NA_WIN_C - 1 - q:2 * GRID_W + NA_WIN_C - 1 - q] for q in range(GRID_W)],
                    axis=2)
    rbm = jnp.where(col_ok[None, None], rbt, -jnp.inf)
    ninf = jnp.full((rel_bias.shape[0], GRID_W, GRID_W), -jnp.inf, F32)
    pats = []
    for g in (0, 1, n_grp - 1):
        qrows = []
        for a in range(NA_GROUP):
            blocks = [rbm[:, dr[g, a, b]] if dr[g, a, b] >= 0 else ninf for b in range(NA_KROWS)]
            qrows.append(jnp.concatenate(blocks, axis=-1))
        pats.append(jnp.concatenate(qrows, axis=-2))
    return jnp.stack(pats, axis=1)


def _scan_consts(rev):
    r2 = lax.broadcasted_iota(jnp.int32, (CH, CH), 0)
    c2 = lax.broadcasted_iota(jnp.int32, (CH, CH), 1)
    tri = jnp.where((c2 >= r2) if rev else (c2 <= r2), 1.0, 0.0).astype(BF16)
    levels = []
    n = CH // 2
    while n >= SUB:
        sh = (2 * n).bit_length() - 1
        same = (r2 >> sh) == (c2 >> sh)
        rin = r2 & (2 * n - 1)
        cin = c2 & (2 * n - 1)
        levels.append((same & (rin < n) & (cin >= n)) if rev else (same & (rin >= n) & (cin < n)))
        n //= 2
    sub_sh = SUB.bit_length() - 1
    dvalid = ((r2 >> sub_sh) == (c2 >> sub_sh)) & ((c2 >= r2) if rev else (c2 <= r2))
    hits = [(c2 & (SUB - 1)) == s for s in range(SUB)]
    return tri, levels, dvalid, hits


def _cumsum_rows(g, tri):
    g1 = g.astype(BF16)
    r1 = g - g1.astype(F32)
    g2 = r1.astype(BF16)
    g3 = (r1 - g2.astype(F32)).astype(BF16)
    dot = lambda a: jnp.dot(tri, a, preferred_element_type=F32)
    return dot(g1) + dot(g2) + dot(g3)


def _block_rows(a, first, period):
    return jnp.concatenate(
        [jnp.broadcast_to(a[r:r + 1, :], (period, 128)) for r in range(first, CH, period)], axis=0)


def _gla_chunk(load, get_state, put, masks, rev, consts):
    tri, level_valid, dvalid, hits = consts
    q, k, g, vs = load()
    nh = len(vs)
    b = _cumsum_rows(g * LOG2E, tri)
    b_end = b[0:1] if rev else b[CH - 1:CH]
    yield

    def headq(a, h):
        return a if masks[h] is None else jnp.where(masks[h], a, 0.0)

    atts = [jnp.zeros((CH, CH), F32) for _ in range(nh)]
    n = CH // 2
    for valid in level_valid:
        first = n if rev else n - 1
        ref = _block_rows(b, first, 2 * n) if 2 * n < CH else b[first:first + 1]
        qn = (q * jnp.exp2(b - ref)).astype(BF16)
        kn = (k * jnp.exp2(ref - b)).astype(BF16)
        for h in range(nh):
            a = _nt(headq(qn, h), kn)
            atts[h] = jnp.where(valid, a, atts[h])
        n //= 2
        yield

    diag = [jnp.zeros((CH, CH), F32) for _ in range(nh)]
    for s in range(SUB):
        e = q * _block_rows(k, s, SUB) * jnp.exp2(b - _block_rows(b, s, SUB))
        for h in range(nh):
            rs = jnp.sum(headq(e, h), axis=-1, keepdims=True)
            diag[h] = jnp.where(hits[s], rs, diag[h])
        yield

    qe = (q * jnp.exp2(b)).astype(BF16)
    kd = k * jnp.exp2(b_end - b)
    dec = jnp.exp2(b_end)
    yield
    sts = get_state()
    outs, new = [], []
    for h in range(nh):
        att = jnp.where(dvalid, diag[h], atts[h])
        vb = vs[h].astype(BF16)
        o = jnp.dot(att.astype(BF16), vb, preferred_element_type=F32) + _nt(qe, sts[h].astype(BF16))
        outs.append(o)
        new.append(sts[h] * dec + _tn(vb, headq(kd, h).astype(BF16)))
    put(outs, new)


SCAN_CHUNKS = 2
SCAN_ROWS = SCAN_CHUNKS * CH


def _chunk_maps(col_block):
    n_c = NCTX // SCAN_ROWS
    n_all = T // SCAN_ROWS

    def fwd(b, s):
        return (b, s, col_block)

    def bwd(b, s):
        return (b, jnp.where(s < n_c, n_c - 1 - s, n_all + n_c - 1 - s), col_block)

    return fwd, bwd


def _scan_order(rev):
    return tuple(reversed(range(SCAN_CHUNKS))) if rev else tuple(range(SCAN_CHUNKS))


def _hgrn_kernel(qf_ref, if_ref, ff_ref, qb_ref, ib_ref, fb_ref, lb_ref, of_ref, ob_ref, st_ref):
    @pl.when(pl.program_id(1) == 0)
    def _():
        st_ref[...] = jnp.zeros_like(st_ref)

    dirs = ((qf_ref, if_ref, ff_ref, of_ref), (qb_ref, ib_ref, fb_ref, ob_ref))
    consts = [_scan_consts(False), _scan_consts(True)]
    gens, finals = [], []
    for d, (q_ref, i_ref, f_ref, o_ref) in enumerate(dirs):
        for h in range(4):
            sl = slice(128 * h, 128 * h + 128)
            state = [[st_ref[d, h]]]
            finals.append((d, h, state))
            for j in _scan_order(d == 1):
                rows = slice(CH * j, CH * j + CH)

                def load(q_ref=q_ref, i_ref=i_ref, f_ref=f_ref, d=d, sl=sl, rows=rows):
                    lb = lb_ref[d:d + 1, sl]
                    q = _silu(q_ref[0, rows, sl])
                    f = f_ref[0, rows, sl]
                    e = jnp.exp(-jnp.abs(f))
                    inv = 1.0 / (1.0 + e)
                    k = (1.0 - lb) * jnp.where(f >= 0.0, e * inv, inv)
                    a = jnp.log(lb)
                    c = jnp.log(1.0 - lb) + (jnp.minimum(f, 0.0) - jnp.log(1.0 + e))
                    g = jnp.maximum(a, c) + jnp.log(1.0 + jnp.exp(-jnp.abs(a - c)))
                    return q, k, g, [i_ref[0, rows, sl]]

                def put(outs, new, o_ref=o_ref, sl=sl, rows=rows, state=state):
                    o_ref[0, rows, sl] = outs[0]
                    state[0] = new

                gens.append(_gla_chunk(load, lambda state=state: state[0], put, [None], d == 1, consts[d]))
    _interleave(gens)
    for d, h, state in finals:
        st_ref[d, h] = state[0][0]


def _hgrn_call(p, lb):
    fq, bq = _chunk_maps(C_Q)
    fi, bi = _chunk_maps(C_I)
    ff, _ = _chunk_maps(C_FF)
    _, bf = _chunk_maps(C_FB)
    fo, bo = _chunk_maps(0)
    blk = lambda m: pl.BlockSpec((1, SCAN_ROWS, 512), m)
    return pl.pallas_call(
        _hgrn_kernel,
        grid=(NB, T // SCAN_ROWS),
        in_specs=[blk(fq), blk(fi), blk(ff), blk(bq), blk(bi), blk(bf),
                  pl.BlockSpec((2, 512), lambda b, s: (0, 0))],
        out_specs=[blk(fo), blk(bo)],
        out_shape=[jax.ShapeDtypeStruct((NB, T, 512), F32)] * 2,
        scratch_shapes=[pltpu.VMEM((2, 4, 128, 128), F32)],
        compiler_params=_cparams(("parallel", "arbitrary"), 32),
        name="hgrn2_scan",
    )(p, p, p, p, p, p, lb)


def _gla_kernel(qkf_ref, vf_ref, zf_ref, qkb_ref, vb_ref, zb_ref, w2_ref, b2_ref, of_ref, ob_ref, st_ref):
    @pl.when(pl.program_id(1) == 0)
    def _():
        st_ref[...] = jnp.zeros_like(st_ref)

    lane = lax.broadcasted_iota(jnp.int32, (1, 128), 1)
    masks = [lane < GLA_DK, lane >= GLA_DK]
    dirs = ((qkf_ref, vf_ref, zf_ref, of_ref), (qkb_ref, vb_ref, zb_ref, ob_ref))
    consts = [_scan_consts(False), _scan_consts(True)]
    gens, finals = [], []
    for d, (qk_ref, v_ref, z_ref, o_ref) in enumerate(dirs):
        logit = jnp.dot(z_ref[0].astype(BF16), w2_ref[d].astype(BF16), preferred_element_type=F32) + b2_ref[d]
        g = _log_sigmoid(logit) * (1.0 / GLA_TAU)
        for grp in range(2):
            sl = slice(128 * grp, 128 * grp + 128)
            heads = (2 * grp, 2 * grp + 1)
            state = [[st_ref[d, h] for h in heads]]
            finals.append((d, heads, state))
            for j in _scan_order(d == 1):
                rows = slice(CH * j, CH * j + CH)

                def load(qk_ref=qk_ref, v_ref=v_ref, g=g, grp=grp, sl=sl, heads=heads, rows=rows):
                    q = qk_ref[0, rows, sl] * (GLA_DK ** -0.5)
                    k = qk_ref[0, rows, 256 + 128 * grp:256 + 128 * grp + 128]
                    return q, k, g[rows, sl], [v_ref[0, rows, 128 * h:128 * h + 128] for h in heads]

                def put(outs, new, o_ref=o_ref, heads=heads, rows=rows, state=state):
                    for o, h in zip(outs, heads):
                        o_ref[0, rows, 128 * h:128 * h + 128] = o
                    state[0] = new

                gens.append(_gla_chunk(load, lambda state=state: state[0], put, masks, d == 1, consts[d]))
    _interleave(gens)
    for d, heads, state in finals:
        for st, h in zip(state[0], heads):
            st_ref[d, h] = st


def _gla_call(p, pt, w2p, b2):
    fqk, bqk = _chunk_maps(D_QK)
    fv, bv = _chunk_maps(D_V)
    fz, bz = _chunk_maps(TAIL_Z)
    fo, bo = _chunk_maps(0)
    blk = lambda m: pl.BlockSpec((1, SCAN_ROWS, 512), m)
    zblk = lambda m: pl.BlockSpec((1, SCAN_ROWS, 128), m)
    return pl.pallas_call(
        _gla_kernel,
        grid=(NB, T // SCAN_ROWS),
        in_specs=[blk(fqk), blk(fv), zblk(fz), blk(bqk), blk(bv), zblk(bz),
                  pl.BlockSpec((2, 128, 256), lambda b, s: (0, 0, 0)),
                  pl.BlockSpec((2, 1, 256), lambda b, s: (0, 0, 0))],
        out_specs=[blk(fo), blk(bo)],
        out_shape=[jax.ShapeDtypeStruct((NB, T, 512), F32)] * 2,
        scratch_shapes=[pltpu.VMEM((2, 4, 128, 128), F32)],
        compiler_params=_cparams(("parallel", "arbitrary"), 32),
        name="gla_scan",
    )(p, p, pt, p, p, pt, w2p, b2)


def _gated_norm(o, gate, g):
    parts = []
    for h in range(4):
        sl = slice(128 * h, 128 * h + 128)
        oh = o[:, sl]
        ms = jnp.mean(oh * oh, axis=-1, keepdims=True)
        parts.append((oh * lax.rsqrt(ms + EPS) * g * _silu(gate[:, sl])).astype(BF16))
    return parts


def _cdpost_kernel(cf_ref, cb_ref, cg_ref, df_ref, db_ref, dr_ref, gc_ref, gd_ref, o_ref):
    for h, part in enumerate(_gated_norm(cf_ref[0] + cb_ref[0], cg_ref[0], gc_ref[...])):
        o_ref[0, :, 128 * h:128 * h + 128] = part
    for h, part in enumerate(_gated_norm(df_ref[0] + db_ref[0], dr_ref[0], gd_ref[...])):
        o_ref[0, :, 512 + 128 * h:512 + 128 * h + 128] = part


def _cdpost_call(cf, cb, df, db, p, pt, gc, gd):
    tm = 256
    row = lambda col: pl.BlockSpec((1, tm, 512), lambda b, i: (b, i, col))
    vec = pl.BlockSpec((1, 128), lambda b, i: (0, 0))
    return pl.pallas_call(
        _cdpost_kernel,
        grid=(NB, T // tm),
        in_specs=[row(0), row(0), row(C_G), row(0), row(0), row(TAIL_R), vec, vec],
        out_specs=pl.BlockSpec((1, tm, 1024), lambda b, i: (b, i, 0)),
        out_shape=jax.ShapeDtypeStruct((NB, T, 1024), BF16),
        compiler_params=_cparams(("parallel", "parallel"), 32),
        name="scan_post",
    )(cf, cb, p, df, db, pt, gc, gd)


def _swiglu_split(h, w1_refs, w3_refs, w2_refs, lead):
    get = (lambda r: r[...]) if lead is None else (lambda r: r[lead])
    kc = D // W_SPLIT
    u = v = None
    for k in range(W_SPLIT):
        hk = h[:, k * kc:(k + 1) * kc]
        pu = jnp.dot(hk, get(w1_refs[k]).astype(BF16), preferred_element_type=F32)
        pv = jnp.dot(hk, get(w3_refs[k]).astype(BF16), preferred_element_type=F32)
        u = pu if u is None else u + pu
        v = pv if v is None else v + pv
    g = (_silu(u) * v).astype(BF16)
    return [jnp.dot(g, get(w2_refs[k]).astype(BF16), preferred_element_type=F32) for k in range(W_SPLIT)]


def _ffn_kernel(h_ref, *refs):
    w1_refs, w3_refs, w2_refs = refs[:W_SPLIT], refs[W_SPLIT:2 * W_SPLIT], refs[2 * W_SPLIT:3 * W_SPLIT]
    o_ref = refs[3 * W_SPLIT]

    @pl.when(pl.program_id(2) == 0)
    def _():
        o_ref[...] = jnp.zeros_like(o_ref)

    nc = D // W_SPLIT
    for k, part in enumerate(_swiglu_split(h_ref[0], w1_refs, w3_refs, w2_refs, None)):
        o_ref[0, :, k * nc:(k + 1) * nc] += part


def _ffn_call(h, w1, w3, w2):
    tm, tf = 1152, 512
    kc = D // W_SPLIT
    up = [pl.BlockSpec((kc, tf), functools.partial(lambda b, i, f, k: (k, f), k=k)) for k in range(W_SPLIT)]
    down = [pl.BlockSpec((tf, kc), functools.partial(lambda b, i, f, k: (f, k), k=k)) for k in range(W_SPLIT)]
    return pl.pallas_call(
        _ffn_kernel,
        grid=(NB, T // tm, FFN_DENSE // tf),
        in_specs=[pl.BlockSpec((1, tm, D), lambda b, i, f: (b, i, 0))] + up + up + down,
        out_specs=pl.BlockSpec((1, tm, D), lambda b, i, f: (b, i, 0), pipeline_mode=pl.Buffered(1)),
        out_shape=jax.ShapeDtypeStruct((NB, T, D), F32),
        compiler_params=_cparams(("parallel", "parallel", "arbitrary"), 56),
        name="dense_ffn",
    )(h, *([w1] * W_SPLIT), *([w3] * W_SPLIT), *([w2] * W_SPLIT))


MOE_TM = 1152
MOE_SUB = 192
MOE_TF = 512
MOE_TILES = -(-(2 * NB * SEQ + N_EXPERTS * (MOE_TM - 1)) // MOE_TM)
MOE_ROWS = MOE_TILES * MOE_TM
MOE_PREFETCH_STEPS = 8
MOE_PREFETCH_ROWS = MOE_TM // MOE_PREFETCH_STEPS


def _route_kernel(x_ref, g_ref, sh_ref, sc_ref, wr_ref, h_ref, r_ref):
    x = x_ref[0]
    ms = jnp.mean(x * x, axis=-1, keepdims=True)
    h = x * lax.rsqrt(ms + EPS) * g_ref[...] * (1.0 + sc_ref[0]) + sh_ref[0]
    h_ref[0] = h
    wr = wr_ref[...]
    h_hi = h.astype(BF16)
    h_lo = (h - h_hi.astype(F32)).astype(BF16)
    w_hi = wr.astype(BF16)
    w_lo = (wr - w_hi.astype(F32)).astype(BF16)
    dot = lambda a, b: jnp.dot(a, b, preferred_element_type=F32)
    logits = dot(h_hi, w_hi) + (dot(h_lo, w_hi) + dot(h_hi, w_lo))
    lane = lax.broadcasted_iota(jnp.int32, logits.shape, 1)
    lanef = lane.astype(F32)
    lg = jnp.where(lane < N_EXPERTS, logits, -jnp.inf)
    m1 = jnp.max(lg, axis=-1, keepdims=True)
    i1 = jnp.min(jnp.where(lg == m1, lanef, 128.0), axis=-1, keepdims=True)
    lg2 = jnp.where(lanef == i1, -jnp.inf, lg)
    m2 = jnp.max(lg2, axis=-1, keepdims=True)
    i2 = jnp.min(jnp.where(lg2 == m2, lanef, 128.0), axis=-1, keepdims=True)
    e = jnp.exp(m2 - m1)
    w1 = 1.0 / (1.0 + e)
    w2 = e * w1
    r_ref[0] = jnp.where(lane == 0, i1, jnp.where(lane == 1, i2, jnp.where(lane == 2, w1,
                         jnp.where(lane == 3, w2, 0.0))))


def _route_call(xall, g, mods, wr):
    tm = 256
    return pl.pallas_call(
        _route_kernel,
        grid=(NB, SEQ // tm),
        in_specs=[
            pl.BlockSpec((1, tm, D), lambda b, i: (b, i + NCTX // tm, 0)),
            pl.BlockSpec((1, D), lambda b, i: (0, 0)),
            _mod_spec(3, False), _mod_spec(4, False),
            pl.BlockSpec((D, 128), lambda b, i: (0, 0)),
        ],
        out_specs=[pl.BlockSpec((1, tm, D), lambda b, i: (b, i, 0)),
                   pl.BlockSpec((1, tm, 128), lambda b, i: (b, i, 0))],
        out_shape=[jax.ShapeDtypeStruct((NB, SEQ, D), F32), jax.ShapeDtypeStruct((NB, SEQ, 128), F32)],
        compiler_params=_cparams(("parallel", "parallel"), 32),
        name="moe_route",
    )(xall, g, mods, mods, wr)


def _moe_kernel(te_ref, nl_ref, src_ref, h_ref, *refs):
    w1_refs, w3_refs, w2_refs = refs[:W_SPLIT], refs[W_SPLIT:2 * W_SPLIT], refs[2 * W_SPLIT:3 * W_SPLIT]
    o_ref, xg_ref, xb_ref, sem = refs[3 * W_SPLIT:]
    i = pl.program_id(0)
    f = pl.program_id(1)
    n_live = nl_ref[i]
    nxt_live = nl_ref[jnp.minimum(i + 1, MOE_TILES - 1)]
    fetch = (i + 1 < MOE_TILES) & (nxt_live > 0) & (f >= 1) & (f <= MOE_PREFETCH_STEPS)

    def row_copy(tile, r):
        row = src_ref[tile * MOE_TM + r]
        return pltpu.make_async_copy(h_ref.at[pl.ds(row, 1), :], xg_ref.at[pl.ds(r, 1), :], sem)

    @pl.when(f == 0)
    def _():
        o_ref[...] = jnp.zeros_like(o_ref)

    @pl.when((i == 0) & (f == 0) & (n_live > 0))
    def _():
        def body(r, c):
            row_copy(0, r).start()
            return c

        lax.fori_loop(0, MOE_TM, body, 0)

    @pl.when((f == 0) & (n_live > 0))
    def _():
        pltpu.make_async_copy(h_ref.at[pl.ds(0, MOE_TM), :], xg_ref, sem).wait()
        xb_ref[...] = xg_ref[...].astype(BF16)

    def step(n, do_fetch):
        def body():
            if do_fetch:
                r0 = (f - 1) * MOE_PREFETCH_ROWS
                for r in range(MOE_PREFETCH_ROWS):
                    row_copy(i + 1, r0 + r).start()
            rows = slice(0, n * MOE_SUB)
            nc = D // W_SPLIT
            for k, part in enumerate(_swiglu_split(xb_ref[rows, :], w1_refs, w3_refs, w2_refs, 0)):
                o_ref[rows, k * nc:(k + 1) * nc] += part

        return body

    for n in range(1, MOE_TM // MOE_SUB + 1):
        for do_fetch in (False, True):
            pl.when((n_live == n) & (fetch == do_fetch))(step(n, do_fetch))


def _moe_call(tile_expert, tile_live, src_rows, h, w1, w3, w2):
    nf = FFN_EXPERT // MOE_TF

    def fidx(i, f, nl):
        return jnp.where(nl[i] > 0, f, nf - 1)

    kc = D // W_SPLIT
    up = [pl.BlockSpec((1, kc, MOE_TF), functools.partial(lambda i, f, te, nl, src, k: (te[i], k, fidx(i, f, nl)), k=k))
          for k in range(W_SPLIT)]
    down = [pl.BlockSpec((1, MOE_TF, kc), functools.partial(lambda i, f, te, nl, src, k: (te[i], fidx(i, f, nl), k), k=k))
            for k in range(W_SPLIT)]
    return pl.pallas_call(
        _moe_kernel,
        grid_spec=pltpu.PrefetchScalarGridSpec(
            num_scalar_prefetch=3,
            grid=(MOE_TILES, nf),
            in_specs=[pl.BlockSpec(memory_space=pl.ANY)] + up + up + down,
            out_specs=pl.BlockSpec((MOE_TM, D), lambda i, f, te, nl, src: (i, 0), pipeline_mode=pl.Buffered(1)),
            scratch_shapes=[pltpu.VMEM((MOE_TM, D), F32), pltpu.VMEM((MOE_TM, D), BF16),
                            pltpu.SemaphoreType.DMA(())],
        ),
        out_shape=jax.ShapeDtypeStruct((MOE_ROWS, D), F32),
        compiler_params=_cparams(("arbitrary", "arbitrary"), 58),
        name="moe_experts",
    )(tile_expert, tile_live, src_rows, h, *([w1] * W_SPLIT), *([w3] * W_SPLIT), *([w2] * W_SPLIT))


def _row_copy(src_hbm, row, dst_ref, r, sem):
    return pltpu.make_async_copy(src_hbm.at[pl.ds(row, 1), :], dst_ref.at[pl.ds(r, 1), :], sem)


def _combine_kernel(dest_ref, x_ref, r_ref, gm_ref, gf_ref, y_ref, o_ref, yb_ref, sem, *, tc, n_steps):
    step = pl.program_id(0) * (SEQ // tc) + pl.program_id(1)
    slot = lax.rem(step, 2)

    def start_rows(step_, slot_):
        def issue(r, c):
            tok = step_ * tc + r
            _row_copy(y_ref, dest_ref[2 * tok], yb_ref.at[slot_, 0], r, sem.at[slot_]).start()
            _row_copy(y_ref, dest_ref[2 * tok + 1], yb_ref.at[slot_, 1], r, sem.at[slot_]).start()
            return c

        lax.fori_loop(0, tc, issue, 0, unroll=8)

    @pl.when(step == 0)
    def _():
        start_rows(0, 0)

    @pl.when(step + 1 < n_steps)
    def _():
        start_rows(step + 1, 1 - slot)

    for j in range(2):
        pltpu.make_async_copy(y_ref.at[pl.ds(0, tc), :], yb_ref.at[slot, j], sem.at[slot]).wait()
    rt = r_ref[0]
    moe = rt[:, 2:3] * yb_ref[slot, 0] + rt[:, 3:4] * yb_ref[slot, 1]
    x = x_ref[0] + gm_ref[0] * moe
    ms = jnp.mean(x * x, axis=-1, keepdims=True)
    o_ref[0] = x * lax.rsqrt(ms + EPS) * gf_ref[...]


def _combine_call(dest, xall, route, mods, gfinal, y):
    tc = 256
    return pl.pallas_call(
        functools.partial(_combine_kernel, tc=tc, n_steps=NB * SEQ // tc),
        grid_spec=pltpu.PrefetchScalarGridSpec(
            num_scalar_prefetch=1,
            grid=(NB, SEQ // tc),
            in_specs=[
                pl.BlockSpec((1, tc, D), lambda b, i, d: (b, i + NCTX // tc, 0)),
                pl.BlockSpec((1, tc, 128), lambda b, i, d: (b, i, 0)),
                pl.BlockSpec((1, 1, D), lambda b, i, d: (b * 6 + 5, 0, 0)),
                pl.BlockSpec((1, D), lambda b, i, d: (0, 0)),
                pl.BlockSpec(memory_space=pl.ANY),
            ],
            out_specs=pl.BlockSpec((1, tc, D), lambda b, i, d: (b, i, 0)),
            scratch_shapes=[pltpu.VMEM((2, 2, tc, D), F32), pltpu.SemaphoreType.DMA((2,))],
        ),
        out_shape=jax.ShapeDtypeStruct((NB, SEQ, D), F32),
        compiler_params=_cparams(("arbitrary", "arbitrary"), 32),
        name="moe_combine",
    )(dest, xall, route, mods, gfinal, y)


def _moe_plan(route):
    n_pairs = 2 * NB * SEQ
    e = route[..., 0:2].astype(jnp.int32).reshape(n_pairs)
    onehot = (e[:, None] == jnp.arange(N_EXPERTS, dtype=jnp.int32)[None, :]).astype(jnp.int32)
    csum = jnp.cumsum(onehot, axis=0)
    rank = jnp.sum(onehot * csum, axis=1) - 1
    counts = csum[-1]
    padded = ((counts + MOE_TM - 1) // MOE_TM) * MOE_TM
    ends = jnp.cumsum(padded)
    starts = ends - padded
    dest = (starts[e] + rank).astype(jnp.int32)
    src_rows = jnp.zeros((MOE_ROWS,), jnp.int32).at[dest].set(jnp.arange(n_pairs, dtype=jnp.int32) // 2, unique_indices=True)
    tile_start = jnp.arange(MOE_TILES, dtype=jnp.int32) * MOE_TM
    t_eff = jnp.minimum(tile_start, ends[-1] - MOE_TM)
    tile_expert = jnp.sum((ends[None, :] <= t_eff[:, None]).astype(jnp.int32), axis=1)
    rows_left = (starts + counts)[tile_expert] - tile_start
    tile_live = jnp.clip((rows_left + MOE_SUB - 1) // MOE_SUB, 0, MOE_TM // MOE_SUB).astype(jnp.int32)
    return dest, src_rows, tile_expert.astype(jnp.int32), tile_live


def _rope_tables():
    per_axis = DIFF_QK // 4
    t = np.arange(SEQ)
    inv = ROPE_BASE ** (-np.arange(per_axis, dtype=np.float32) / per_axis)
    ang = np.concatenate([(t // GRID_W).astype(np.float32)[:, None] * inv,
                          (t % GRID_W).astype(np.float32)[:, None] * inv], axis=-1).astype(np.float32)
    cos = np.concatenate([np.ones((NCTX, 2 * per_axis), np.float32), np.cos(ang)], axis=0)
    sin = np.concatenate([np.zeros((NCTX, 2 * per_axis), np.float32), np.sin(ang)], axis=0)
    reps = 128 // (2 * per_axis)
    return jnp.asarray(np.tile(cos, (1, reps))), jnp.asarray(np.tile(sin, (1, reps)))


def _in_proj_tail(w_t):
    z = w_t[:, IN_MAIN:IN_MAIN + 2 * GLA_RANK]
    r = w_t[:, IN_MAIN + 2 * GLA_RANK:]
    pad = jnp.zeros((w_t.shape[0], TAIL_W - r.shape[1] - z.shape[1], D), w_t.dtype)
    return jnp.concatenate([r, z, pad], axis=1)


def kernel(x, c, ctx, c_ctx, ada_w, ada_b, norm_mix, norm_ffn, w_in, w_out, diff_lambda, diff_norm,
           na_rel_bias, hgrn_lower_bounds, hgrn_norm, gla_gate_w2, gla_gate_b, gla_norm,
           ffn_w1, ffn_w3, ffn_w2, moe_router, moe_w1, moe_w3, moe_w2, final_norm):
    lb_soft = jax.nn.softmax(hgrn_lower_bounds.astype(F32), axis=1)
    lower_bounds = jnp.clip(jnp.cumsum(lb_soft, axis=1) - lb_soft[:, :1], 0.0, 1.0 - 1e-6)
    cond8 = jnp.concatenate([c, c_ctx[None, :], jnp.zeros((8 - NB - 1, D), F32)], axis=0)
    mods_all = _ada_call(cond8, ada_w, ada_b).reshape(DEPTH, 8 * 6, 1, D)
    cos, sin = _rope_tables()
    xall = jnp.concatenate([ctx, x], axis=1)
    w_in_t = jnp.swapaxes(w_in, 1, 2)
    w_tail_t = _in_proj_tail(w_in_t)

    out = None
    pending = None
    for l in range(DEPTH):
        mods = mods_all[l]
        if pending is None:
            h = _norm_call(xall, norm_mix[l][None, :], mods, 0, 1)
        else:
            xall, h = _norm_call(xall, norm_mix[l][None, :], mods, 0, 1, resid=pending)
            pending = None
        p = _wsmm_call([h], w_in_t, l, 1664, IN_MAIN // 1664, w_t=True, vmem_mb=56, name="in_proj")
        pt = _wsmm_call([h], w_tail_t, l, TAIL_W, 1, w_t=True, vmem_mb=32, name="in_proj_tail")

        lambda_init = 0.8 - 0.6 * math.exp(-0.3 * l)
        lp = diff_lambda[l].astype(F32)
        lam = (jnp.exp(jnp.sum(lp[0] * lp[1])) - jnp.exp(jnp.sum(lp[2] * lp[3])) + lambda_init).reshape(1)
        a = _attn_a_call(p, lam, cos, sin, jnp.tile(diff_norm[l], 2)[None, :], 1.0 - lambda_init)
        n = _attn_na_call(p, _na_bias_table(na_rel_bias[l]))
        cf, cb = _hgrn_call(p, lower_bounds[:, l])
        w2p = jnp.zeros((2, 128, 4 * GLA_DK), F32)
        w2p = w2p.at[0, 0:GLA_RANK].set(gla_gate_w2[l, 0]).at[1, GLA_RANK:2 * GLA_RANK].set(gla_gate_w2[l, 1])
        df, db = _gla_call(p, pt, w2p, gla_gate_b[l][:, None, :])
        cd = _cdpost_call(cf, cb, df, db, p, pt, hgrn_norm[l][None, :], gla_norm[l][None, :])
        xall = _wsmm_call([a, n, cd], w_out, l, 1024, D // 1024, resid=(xall, mods, 2), name="out_proj")

        if l % 2 == 0:
            h2 = _norm_call(xall, norm_ffn[l][None, :], mods, 3, 4)
            y = _ffn_call(h2, ffn_w1[l // 2], ffn_w3[l // 2], ffn_w2[l // 2])
            pending = (y, mods, 5)
        else:
            assert l == DEPTH - 1
            wr = jnp.zeros((D, 128), F32).at[:, :N_EXPERTS].set(moe_router[l // 2])
            hr, route = _route_call(xall, norm_ffn[l][None, :], mods, wr)
            dest, src_rows, tile_expert, tile_live = _moe_plan(route)
            y = _moe_call(tile_expert, tile_live, src_rows, hr.reshape(NB * SEQ, D),
                          moe_w1[l // 2], moe_w3[l // 2], moe_w2[l // 2])
            out = _combine_call(dest, xall, route, mods, final_norm[None, :], y)
    return out
```

```python
import functools
import math

import numpy as np
import jax
import jax.numpy as jnp
from jax import lax
from jax.experimental import pallas as pl
from jax.experimental.pallas import tpu as pltpu

F32 = jnp.float32
BF16 = jnp.bfloat16

D = 2048
NB = 4
SEQ = 2048
NCTX = 256
T = NCTX + SEQ
DEPTH = 2
GRID_W = 64
ROPE_BASE = 10000.0
EPS = 1e-6
LOG2E = math.log2(math.e)

DIFF_QK = 32
NA_DIM = 64
NA_WIN_R = 8
NA_WIN_C = 16
NA_GROUP = 4
NA_KROWS = NA_GROUP + NA_WIN_R - 1
GLA_DK = 64
GLA_TAU = 16.0
GLA_RANK = 16
CH = 64
SUB = 8

FFN_DENSE = 5632
N_EXPERTS = 8
FFN_EXPERT = 7168

IN_MAIN = 6656
A_Q, A_K, A_V = 0, 4, 8
B_Q, B_K, B_V = 12, 16, 20
C_Q, C_I, C_FF, C_FB, C_G = 6, 7, 8, 9, 10
D_QK, D_V = 11, 12
TAIL_W = 640
TAIL_R, TAIL_Z = 0, 4

V7X_VMEM_BYTES = 64 * 1024 * 1024
W_SPLIT = 1


def _cparams(sem, vmem_mb):
    assert vmem_mb * 1024 * 1024 < V7X_VMEM_BYTES
    return pltpu.CompilerParams(dimension_semantics=sem, vmem_limit_bytes=vmem_mb * 1024 * 1024)


def _sigmoid(x):
    return 1.0 / (1.0 + jnp.exp(-x))


def _silu(x):
    return x * _sigmoid(x)


def _log_sigmoid(x):
    return jnp.minimum(x, 0.0) - jnp.log(1.0 + jnp.exp(-jnp.abs(x)))


def _nt(a, b):
    return lax.dot_general(a, b, (((1,), (1,)), ((), ())), preferred_element_type=F32)


def _tn(a, b):
    return lax.dot_general(a, b, (((0,), (0,)), ((), ())), preferred_element_type=F32)


def _interleave(gens):
    live = list(gens)
    while live:
        still = []
        for gen in live:
            try:
                next(gen)
                still.append(gen)
            except StopIteration:
                pass
        live = still


def _ada_kernel(c_ref, *refs):
    w_refs, b_ref, o_ref = refs[:W_SPLIT], refs[W_SPLIT], refs[W_SPLIT + 1]
    s = _silu(c_ref[...]).astype(BF16)
    kc = D // W_SPLIT
    acc = b_ref[0]
    for k in range(W_SPLIT):
        acc = acc + jnp.dot(s[:, k * kc:(k + 1) * kc], w_refs[k][0].astype(BF16), preferred_element_type=F32)
    o_ref[0] = acc


def _ada_call(cond8, ada_w, ada_b):
    tn = 1536
    n = 6 * D
    kc = D // W_SPLIT
    w_specs = [pl.BlockSpec((1, kc, tn), functools.partial(lambda l, j, k: (l, k, j), k=k)) for k in range(W_SPLIT)]
    return pl.pallas_call(
        _ada_kernel,
        grid=(DEPTH, n // tn),
        in_specs=[pl.BlockSpec((8, D), lambda l, j: (0, 0))] + w_specs
        + [pl.BlockSpec((1, 1, tn), lambda l, j: (l, 0, j))],
        out_specs=pl.BlockSpec((1, 8, tn), lambda l, j: (l, 0, j)),
        out_shape=jax.ShapeDtypeStruct((DEPTH, 8, n), F32),
        compiler_params=_cparams(("parallel", "parallel"), 40),
        name="ada_mod",
    )(cond8, *([ada_w] * W_SPLIT), ada_b.reshape(DEPTH, 1, n))


def _mod_spec(k, ctx):
    if ctx:
        return pl.BlockSpec((1, 1, D), lambda b, i: (NB * 6 + k, 0, 0))
    return pl.BlockSpec((1, 1, D), lambda b, i: (b * 6 + k, 0, 0))


def _row_is_ctx(row0, tm):
    return (row0 + lax.broadcasted_iota(jnp.int32, (tm, 1), 0)) < NCTX


def _modnorm(x, g, is_c, shl, scl, shc, scc):
    ms = jnp.mean(x * x, axis=-1, keepdims=True)
    y = x * lax.rsqrt(ms + EPS) * g
    sc = jnp.where(is_c, scc, scl)
    sh = jnp.where(is_c, shc, shl)
    return y * (1.0 + sc) + sh


def _norm_kernel(*refs, tm, resid):
    if resid:
        x_ref, y_ref, gl_ref, gx_ref, g_ref, shl_ref, scl_ref, shc_ref, scc_ref, xo_ref, h_ref = refs
    else:
        x_ref, g_ref, shl_ref, scl_ref, shc_ref, scc_ref, h_ref = refs
    is_c = _row_is_ctx(pl.program_id(1) * tm, tm)
    x = x_ref[0]
    if resid:
        x = x + jnp.where(is_c, gx_ref[0], gl_ref[0]) * y_ref[0]
        xo_ref[0] = x
    h_ref[0] = _modnorm(x, g_ref[...], is_c, shl_ref[0], scl_ref[0], shc_ref[0], scc_ref[0]).astype(BF16)


def _norm_call(xall, g, mods, k_shift, k_scale, resid=None):
    tm = 384
    row = pl.BlockSpec((1, tm, D), lambda b, i: (b, i, 0))
    in_specs, args = [row], [xall]
    if resid is not None:
        y, mods_prev, k_gate = resid
        in_specs += [row, _mod_spec(k_gate, False), _mod_spec(k_gate, True)]
        args += [y, mods_prev, mods_prev]
    in_specs += [pl.BlockSpec((1, D), lambda b, i: (0, 0)),
                 _mod_spec(k_shift, False), _mod_spec(k_scale, False),
                 _mod_spec(k_shift, True), _mod_spec(k_scale, True)]
    args += [g, mods, mods, mods, mods]
    h_shape = jax.ShapeDtypeStruct((NB, T, D), BF16)
    out_shape, out_specs = h_shape, row
    if resid is not None:
        out_shape, out_specs = [jax.ShapeDtypeStruct((NB, T, D), F32), h_shape], [row, row]
    return pl.pallas_call(
        functools.partial(_norm_kernel, tm=tm, resid=resid is not None),
        grid=(NB, T // tm),
        in_specs=in_specs, out_specs=out_specs, out_shape=out_shape,
        compiler_params=_cparams(("parallel", "parallel"), 32),
        name="mod_norm",
    )(*args)


def _wsmm_kernel(*refs, ksizes, tm, epilogue, w_t):
    n_in = len(ksizes)
    ins, w_ref = refs[:n_in], refs[n_in]
    o_ref, wb_ref = refs[-2], refs[-1]

    @pl.when((pl.program_id(1) == 0) & (pl.program_id(2) == 0))
    def _():
        wb_ref[...] = w_ref[0].astype(BF16)

    acc, k0 = None, 0
    for r, ks in zip(ins, ksizes):
        if w_t:
            part = _nt(r[0], wb_ref[:, k0:k0 + ks])
        else:
            part = jnp.dot(r[0], wb_ref[k0:k0 + ks, :], preferred_element_type=F32)
        acc = part if acc is None else acc + part
        k0 += ks
    if epilogue:
        x_ref, gl_ref, gx_ref = refs[n_in + 1:n_in + 4]
        gate = jnp.where(_row_is_ctx(pl.program_id(2) * tm, tm), gx_ref[0], gl_ref[0])
        acc = x_ref[0] + gate * acc
    o_ref[0] = acc


def _wsmm_call(acts, w, layer, tn, n_j, resid=None, w_t=False, vmem_mb=48, name="matmul"):
    tm = 768
    ksizes = tuple(a.shape[-1] for a in acts)
    kdim = sum(ksizes)
    in_specs = [pl.BlockSpec((1, tm, ks), lambda j, b, i: (b, i, 0)) for ks in ksizes]
    if w_t:
        in_specs.append(pl.BlockSpec((1, tn, kdim), lambda j, b, i: (layer, j, 0)))
    else:
        in_specs.append(pl.BlockSpec((1, kdim, tn), lambda j, b, i: (layer, 0, j)))
    args = list(acts) + [w]
    if resid is not None:
        x, mods, k_gate = resid
        in_specs += [pl.BlockSpec((1, tm, tn), lambda j, b, i: (b, i, j)),
                     pl.BlockSpec((1, 1, tn), lambda j, b, i: (b * 6 + k_gate, 0, j)),
                     pl.BlockSpec((1, 1, tn), lambda j, b, i: (NB * 6 + k_gate, 0, j))]
        args += [x, mods, mods]
    return pl.pallas_call(
        functools.partial(_wsmm_kernel, ksizes=ksizes, tm=tm, epilogue=resid is not None, w_t=w_t),
        grid=(n_j, NB, T // tm),
        in_specs=in_specs,
        out_specs=pl.BlockSpec((1, tm, tn), lambda j, b, i: (b, i, j)),
        out_shape=jax.ShapeDtypeStruct((NB, T, n_j * tn), F32),
        scratch_shapes=[pltpu.VMEM((tn, kdim) if w_t else (kdim, tn), BF16)],
        compiler_params=_cparams(("arbitrary", "arbitrary", "arbitrary"), vmem_mb),
        name=name,
    )(*args)


def _rope(x, cos, sin):
    lane = lax.broadcasted_iota(jnp.int32, x.shape, 1)
    first = (lane & (DIFF_QK - 1)) < (DIFF_QK // 2)
    rot = jnp.where(first, -pltpu.roll(x, 128 - DIFF_QK // 2, 1), pltpu.roll(x, DIFF_QK // 2, 1))
    return x * cos + rot * sin


def _attn_a_kernel(lam_ref, q_ref, k_ref, v_ref, cos_ref, sin_ref, g_ref, o_ref, ks_ref, va_ref, *, post_scale):
    t = pl.program_id(2)
    lane = lax.broadcasted_iota(jnp.int32, (1, 128), 1)
    ones_lane = (64, 0)

    @pl.when(t == 0)
    def _():
        ks_ref[...] = _rope(k_ref[0], cos_ref[...], sin_ref[...]).astype(BF16)
        v = v_ref[0]
        va_ref[0] = jnp.where(lane < 64, v, jnp.where(lane == ones_lane[0], 1.0, 0.0)).astype(BF16)
        va_ref[1] = jnp.where(lane >= 64, v, jnp.where(lane == ones_lane[1], 1.0, 0.0)).astype(BF16)

    lam = lam_ref[0]

    def attend(q0, nq, nk):
        rows = pl.ds(q0, nq)
        q = _rope(q_ref[0, rows, :], cos_ref[rows, :], sin_ref[rows, :]) * (DIFF_QK ** -0.5 * LOG2E)
        kk = ks_ref[0:nk, :]
        terms = {}

        def softmax_pv(hh, m):
            lo = 64 * hh + DIFF_QK * m
            qm = jnp.where((lane >= lo) & (lane < lo + DIFF_QK), q, 0.0).astype(BF16)
            s = _nt(qm, kk)
            yield
            mx = jnp.max(s, axis=-1, keepdims=True)
            yield
            e = jnp.exp2((s - mx).astype(BF16))
            yield
            num = jnp.dot(e, va_ref[hh, 0:nk, :], preferred_element_type=F32)
            yield
            den = jnp.sum(jnp.where(lane == ones_lane[hh], num, 0.0), axis=-1, keepdims=True)
            terms[hh, m] = num * ((1.0 if m == 0 else lam) / den)

        _interleave([softmax_pv(hh, m) for hh in range(2) for m in range(2)])
        outs = [terms[hh, 0] - terms[hh, 1] for hh in range(2)]
        o = jnp.where(lane < 64, outs[0], outs[1])
        sq = o * o
        s0 = jnp.sum(jnp.where(lane < 64, sq, 0.0), axis=-1, keepdims=True)
        s1 = jnp.sum(jnp.where(lane >= 64, sq, 0.0), axis=-1, keepdims=True)
        ms = jnp.where(lane < 64, s0, s1) * (1.0 / 64.0)
        o_ref[0, rows, :] = (o * lax.rsqrt(ms + EPS) * (g_ref[...] * post_scale)).astype(o_ref.dtype)

    @pl.when(t == 0)
    def _():
        attend(0, NCTX, NCTX)

    @pl.when(t > 0)
    def _():
        attend(pl.multiple_of(NCTX + (t - 1) * ATTN_TQ, ATTN_TQ // 2), ATTN_TQ, T)


ATTN_TQ = 512


def _attn_a_call(p, lam, cos, sin, g2, post_scale):
    full = lambda col: pl.BlockSpec((1, T, 128), lambda b, h, t: (b, 0, col + h))
    tab = pl.BlockSpec((T, 128), lambda b, h, t: (0, 0))
    return pl.pallas_call(
        functools.partial(_attn_a_kernel, post_scale=post_scale),
        grid=(NB, 4, 1 + SEQ // ATTN_TQ),
        in_specs=[pl.BlockSpec(memory_space=pltpu.SMEM), full(A_Q), full(A_K), full(A_V), tab, tab,
                  pl.BlockSpec((1, 128), lambda b, h, t: (0, 0))],
        out_specs=pl.BlockSpec((1, T, 128), lambda b, h, t: (b, 0, h)),
        out_shape=jax.ShapeDtypeStruct((NB, T, 512), BF16),
        scratch_shapes=[pltpu.VMEM((T, 128), BF16), pltpu.VMEM((2, T, 128), BF16)],
        compiler_params=_cparams(("parallel", "parallel", "arbitrary"), 48),
        name="diff_attn",
    )(lam, p, p, p, cos, sin, g2)


def _attn_na_kernel(q_ref, k_ref, v_ref, bias_ref, o_ref):
    g = pl.program_id(1)
    lane = lax.broadcasted_iota(jnp.int32, (1, 128), 1)

    def run(chain):
        results = {}
        gens = []
        for p in range(4):
            sl = slice(128 * p, 128 * p + 128)
            q = q_ref[0, :, sl] * (NA_DIM ** -0.5 * LOG2E)
            for hh in range(2):
                qm = jnp.where((lane >= 64 * hh) & (lane < 64 * hh + 64), q, 0.0).astype(BF16)
                gens.append(chain(2 * p + hh, qm, sl, results))
        _interleave(gens)
        for p in range(4):
            o_ref[0, :, 128 * p:128 * p + 128] = jnp.where(lane < 64, results[2 * p], results[2 * p + 1]
                                                           ).astype(o_ref.dtype)

    @pl.when(g == 0)
    def _():
        def ctx_head(h, qm, sl, results):
            s = _nt(qm, k_ref[0, 0:NCTX, sl].astype(BF16))
            yield
            e = jnp.exp2(s - jnp.max(s, axis=-1, keepdims=True))
            yield
            r = 1.0 / jnp.sum(e, axis=-1, keepdims=True)
            results[h] = jnp.dot(e.astype(BF16), v_ref[0, 0:NCTX, sl].astype(BF16), preferred_element_type=F32) * r

        run(ctx_head)

    @pl.when(g > 0)
    def _():
        u0 = jnp.clip(NA_GROUP * (g - 1) - NA_WIN_R // 2, 0, SEQ // GRID_W - NA_KROWS)
        win = pl.ds(pl.multiple_of(NCTX + GRID_W * u0, GRID_W), NA_KROWS * GRID_W)

        def lat_head(h, qm, sl, results):
            sw = _nt(qm, k_ref[0, win, sl].astype(BF16)) + bias_ref[h, 0]
            sc = _nt(qm, k_ref[0, 0:NCTX, sl].astype(BF16))
            yield
            mx = jnp.maximum(jnp.max(sw, axis=-1, keepdims=True), jnp.max(sc, axis=-1, keepdims=True))
            yield
            ew = jnp.exp2(sw - mx)
            ec = jnp.exp2(sc - mx)
            yield
            r = 1.0 / (jnp.sum(ew, axis=-1, keepdims=True) + jnp.sum(ec, axis=-1, keepdims=True))
            o = (jnp.dot(ew.astype(BF16), v_ref[0, win, sl].astype(BF16), preferred_element_type=F32)
                 + jnp.dot(ec.astype(BF16), v_ref[0, 0:NCTX, sl].astype(BF16), preferred_element_type=F32))
            yield
            results[h] = o * r

        run(lat_head)


def _attn_na_call(p, bias):
    tq = NA_GROUP * GRID_W
    nk = NA_KROWS * GRID_W
    n_grp = SEQ // tq

    def bias_map(b, g):
        grp = jnp.maximum(g - 1, 0)
        return (0, jnp.where(grp == 0, 0, jnp.where(grp == n_grp - 1, 2, 1)), 0, 0)

    return pl.pallas_call(
        _attn_na_kernel,
        grid=(NB, T // tq),
        in_specs=[
            pl.BlockSpec((1, tq, 512), lambda b, g: (b, g, B_Q // 4)),
            pl.BlockSpec((1, T, 512), lambda b, g: (b, 0, B_K // 4)),
            pl.BlockSpec((1, T, 512), lambda b, g: (b, 0, B_V // 4)),
            pl.BlockSpec((8, 1, tq, nk), bias_map),
        ],
        out_specs=pl.BlockSpec((1, tq, 512), lambda b, g: (b, g, 0)),
        out_shape=jax.ShapeDtypeStruct((NB, T, 512), BF16),
        compiler_params=_cparams(("parallel", "arbitrary"), 48),
        name="nbr_attn",
    )(p, p, p, bias)


def _na_bias_table(rel_bias):
    rows = SEQ // GRID_W
    n_grp = rows // NA_GROUP
    cq = np.arange(GRID_W)
    col_start = np.clip(cq - NA_WIN_C // 2, 0, GRID_W - NA_WIN_C)
    col_ok = (cq[None, :] >= col_start[:, None]) & (cq[None, :] < col_start[:, None] + NA_WIN_C)
    dc = np.clip(cq[None, :] - cq[:, None], -(NA_WIN_C - 1), NA_WIN_C - 1) + (NA_WIN_C - 1)
    gi = np.arange(n_grp)[:, None, None]
    qr = np.arange(NA_GROUP)[None, :, None]
    ku = np.arange(NA_KROWS)[None, None, :]
    r = NA_GROUP * gi + qr
    u = np.clip(NA_GROUP * gi - NA_WIN_R // 2, 0, rows - NA_KROWS) + ku
    row_start = np.clip(r - NA_WIN_R // 2, 0, rows - NA_WIN_R)
    row_ok = (u >= row_start) & (u < row_start + NA_WIN_R)
    dr = np.where(row_ok, u - r + (NA_WIN_R - 1), -1)
    for g in range(2, n_grp - 1):
        assert np.array_equal(dr[g], dr[1])
    assert np.all((dc == cq[None, :] - cq[:, None] + NA_WIN_C - 1)[col_ok])
    rbp = jnp.pad(rel_bias.astype(F32) * LOG2E, ((0, 0), (0, 0), (GRID_W, GRID_W)))
    rbt = jnp.stack([rbp[:, :, GRID_W + NA_WIN_C - 1 - q:2 * GRID_W + NA_WIN_C - 1 - q] for q in range(GRID_W)],
                    axis=2)
    rbm = jnp.where(col_ok[None, None], rbt, -jnp.inf)
    ninf = jnp.full((rel_bias.shape[0], GRID_W, GRID_W), -jnp.inf, F32)
    pats = []
    for g in (0, 1, n_grp - 1):
        qrows = []
        for a in range(NA_GROUP):
            blocks = [rbm[:, dr[g, a, b]] if dr[g, a, b] >= 0 else ninf for b in range(NA_KROWS)]
            qrows.append(jnp.concatenate(blocks, axis=-1))
        pats.append(jnp.concatenate(qrows, axis=-2))
    return jnp.stack(pats, axis=1)


def _scan_consts(rev):
    r2 = lax.broadcasted_iota(jnp.int32, (CH, CH), 0)
    c2 = lax.broadcasted_iota(jnp.int32, (CH, CH), 1)
    tri = jnp.where((c2 >= r2) if rev else (c2 <= r2), 1.0, 0.0).astype(BF16)
    levels = []
    n = CH // 2
    while n >= SUB:
        sh = (2 * n).bit_length() - 1
        same = (r2 >> sh) == (c2 >> sh)
        rin = r2 & (2 * n - 1)
        cin = c2 & (2 * n - 1)
        levels.append((same & (rin < n) & (cin >= n)) if rev else (same & (rin >= n) & (cin < n)))
        n //= 2
    sub_sh = SUB.bit_length() - 1
    dvalid = ((r2 >> sub_sh) == (c2 >> sub_sh)) & ((c2 >= r2) if rev else (c2 <= r2))
    hits = [(c2 & (SUB - 1)) == s for s in range(SUB)]
    return tri, levels, dvalid, hits


def _cumsum_rows(g, tri):
    g1 = g.astype(BF16)
    r1 = g - g1.astype(F32)
    g2 = r1.astype(BF16)
    g3 = (r1 - g2.astype(F32)).astype(BF16)
    dot = lambda a: jnp.dot(tri, a, preferred_element_type=F32)
    return dot(g1) + dot(g2) + dot(g3)


def _block_rows(a, first, period):
    return jnp.concatenate(
        [jnp.broadcast_to(a[r:r + 1, :], (period, 128)) for r in range(first, CH, period)], axis=0)


def _gla_chunk(load, get_state, put, masks, rev, consts):
    tri, level_valid, dvalid, hits = consts
    q, k, g, vs = load()
    nh = len(vs)
    b = _cumsum_rows(g * LOG2E, tri)
    b_end = b[0:1] if rev else b[CH - 1:CH]
    yield

    def headq(a, h):
        return a if masks[h] is None else jnp.where(masks[h], a, 0.0)

    atts = [jnp.zeros((CH, CH), F32) for _ in range(nh)]
    n = CH // 2
    for valid in level_valid:
        first = n if rev else n - 1
        ref = _block_rows(b, first, 2 * n) if 2 * n < CH else b[first:first + 1]
        qn = (q * jnp.exp2(b - ref)).astype(BF16)
        kn = (k * jnp.exp2(ref - b)).astype(BF16)
        for h in range(nh):
            a = _nt(headq(qn, h), kn)
            atts[h] = jnp.where(valid, a, atts[h])
        n //= 2
        yield

    diag = [jnp.zeros((CH, CH), F32) for _ in range(nh)]
    for s in range(SUB):
        e = q * _block_rows(k, s, SUB) * jnp.exp2(b - _block_rows(b, s, SUB))
        for h in range(nh):
            rs = jnp.sum(headq(e, h), axis=-1, keepdims=True)
            diag[h] = jnp.where(hits[s], rs, diag[h])
        yield

    qe = (q * jnp.exp2(b)).astype(BF16)
    kd = k * jnp.exp2(b_end - b)
    dec = jnp.exp2(b_end)
    yield
    sts = get_state()
    outs, new = [], []
    for h in range(nh):
        att = jnp.where(dvalid, diag[h], atts[h])
        vb = vs[h].astype(BF16)
        o = jnp.dot(att.astype(BF16), vb, preferred_element_type=F32) + _nt(qe, sts[h].astype(BF16))
        outs.append(o)
        new.append(sts[h] * dec + _tn(vb, headq(kd, h).astype(BF16)))
    put(outs, new)


SCAN_CHUNKS = 2
SCAN_ROWS = SCAN_CHUNKS * CH


def _chunk_maps(col_block):
    n_c = NCTX // SCAN_ROWS
    n_all = T // SCAN_ROWS

    def fwd(b, s):
        return (b, s, col_block)

    def bwd(b, s):
        return (b, jnp.where(s < n_c, n_c - 1 - s, n_all + n_c - 1 - s), col_block)

    return fwd, bwd


def _scan_order(rev):
    return tuple(reversed(range(SCAN_CHUNKS))) if rev else tuple(range(SCAN_CHUNKS))


def _hgrn_kernel(qf_ref, if_ref, ff_ref, qb_ref, ib_ref, fb_ref, lb_ref, of_ref, ob_ref, st_ref):
    @pl.when(pl.program_id(1) == 0)
    def _():
        st_ref[...] = jnp.zeros_like(st_ref)

    dirs = ((qf_ref, if_ref, ff_ref, of_ref), (qb_ref, ib_ref, fb_ref, ob_ref))
    consts = [_scan_consts(False), _scan_consts(True)]
    gens, finals = [], []
    for d, (q_ref, i_ref, f_ref, o_ref) in enumerate(dirs):
        for h in range(4):
            sl = slice(128 * h, 128 * h + 128)
            state = [[st_ref[d, h]]]
            finals.append((d, h, state))
            for j in _scan_order(d == 1):
                rows = slice(CH * j, CH * j + CH)

                def load(q_ref=q_ref, i_ref=i_ref, f_ref=f_ref, d=d, sl=sl, rows=rows):
                    lb = lb_ref[d:d + 1, sl]
                    q = _silu(q_ref[0, rows, sl])
                    f = f_ref[0, rows, sl]
                    e = jnp.exp(-jnp.abs(f))
                    inv = 1.0 / (1.0 + e)
                    k = (1.0 - lb) * jnp.where(f >= 0.0, e * inv, inv)
                    a = jnp.log(lb)
                    c = jnp.log(1.0 - lb) + (jnp.minimum(f, 0.0) - jnp.log(1.0 + e))
                    g = jnp.maximum(a, c) + jnp.log(1.0 + jnp.exp(-jnp.abs(a - c)))
                    return q, k, g, [i_ref[0, rows, sl]]

                def put(outs, new, o_ref=o_ref, sl=sl, rows=rows, state=state):
                    o_ref[0, rows, sl] = outs[0]
                    state[0] = new

                gens.append(_gla_chunk(load, lambda state=state: state[0], put, [None], d == 1, consts[d]))
    _interleave(gens)
    for d, h, state in finals:
        st_ref[d, h] = state[0][0]


def _hgrn_call(p, lb):
    fq, bq = _chunk_maps(C_Q)
    fi, bi = _chunk_maps(C_I)
    ff, _ = _chunk_maps(C_FF)
    _, bf = _chunk_maps(C_FB)
    fo, bo = _chunk_maps(0)
    blk = lambda m: pl.BlockSpec((1, SCAN_ROWS, 512), m)
    return pl.pallas_call(
        _hgrn_kernel,
        grid=(NB, T // SCAN_ROWS),
        in_specs=[blk(fq), blk(fi), blk(ff), blk(bq), blk(bi), blk(bf),
                  pl.BlockSpec((2, 512), lambda b, s: (0, 0))],
        out_specs=[blk(fo), blk(bo)],
        out_shape=[jax.ShapeDtypeStruct((NB, T, 512), F32)] * 2,
        scratch_shapes=[pltpu.VMEM((2, 4, 128, 128), F32)],
        compiler_params=_cparams(("parallel", "arbitrary"), 32),
        name="hgrn2_scan",
    )(p, p, p, p, p, p, lb)


def _gla_kernel(qkf_ref, vf_ref, zf_ref, qkb_ref, vb_ref, zb_ref, w2_ref, b2_ref, of_ref, ob_ref, st_ref):
    @pl.when(pl.program_id(1) == 0)
    def _():
        st_ref[...] = jnp.zeros_like(st_ref)

    lane = lax.broadcasted_iota(jnp.int32, (1, 128), 1)
    masks = [lane < GLA_DK, lane >= GLA_DK]
    dirs = ((qkf_ref, vf_ref, zf_ref, of_ref), (qkb_ref, vb_ref, zb_ref, ob_ref))
    consts = [_scan_consts(False), _scan_consts(True)]
    gens, finals = [], []
    for d, (qk_ref, v_ref, z_ref, o_ref) in enumerate(dirs):
        logit = jnp.dot(z_ref[0].astype(BF16), w2_ref[d].astype(BF16), preferred_element_type=F32) + b2_ref[d]
        g = _log_sigmoid(logit) * (1.0 / GLA_TAU)
        for grp in range(2):
            sl = slice(128 * grp, 128 * grp + 128)
            heads = (2 * grp, 2 * grp + 1)
            state = [[st_ref[d, h] for h in heads]]
            finals.append((d, heads, state))
            for j in _scan_order(d == 1):
                rows = slice(CH * j, CH * j + CH)

                def load(qk_ref=qk_ref, v_ref=v_ref, g=g, grp=grp, sl=sl, heads=heads, rows=rows):
                    q = qk_ref[0, rows, sl] * (GLA_DK ** -0.5)
                    k = qk_ref[0, rows, 256 + 128 * grp:256 + 128 * grp + 128]
                    return q, k, g[rows, sl], [v_ref[0, rows, 128 * h:128 * h + 128] for h in heads]

                def put(outs, new, o_ref=o_ref, heads=heads, rows=rows, state=state):
                    for o, h in zip(outs, heads):
                        o_ref[0, rows, 128 * h:128 * h + 128] = o
                    state[0] = new

                gens.append(_gla_chunk(load, lambda state=state: state[0], put, masks, d == 1, consts[d]))
    _interleave(gens)
    for d, heads, state in finals:
        for st, h in zip(state[0], heads):
            st_ref[d, h] = st


def _gla_call(p, pt, w2p, b2):
    fqk, bqk = _chunk_maps(D_QK)
    fv, bv = _chunk_maps(D_V)
    fz, bz = _chunk_maps(TAIL_Z)
    fo, bo = _chunk_maps(0)
    blk = lambda m: pl.BlockSpec((1, SCAN_ROWS, 512), m)
    zblk = lambda m: pl.BlockSpec((1, SCAN_ROWS, 128), m)
    return pl.pallas_call(
        _gla_kernel,
        grid=(NB, T // SCAN_ROWS),
        in_specs=[blk(fqk), blk(fv), zblk(fz), blk(bqk), blk(bv), zblk(bz),
                  pl.BlockSpec((2, 128, 256), lambda b, s: (0, 0, 0)),
                  pl.BlockSpec((2, 1, 256), lambda b, s: (0, 0, 0))],
        out_specs=[blk(fo), blk(bo)],
        out_shape=[jax.ShapeDtypeStruct((NB, T, 512), F32)] * 2,
        scratch_shapes=[pltpu.VMEM((2, 4, 128, 128), F32)],
        compiler_params=_cparams(("parallel", "arbitrary"), 32),
        name="gla_scan",
    )(p, p, pt, p, p, pt, w2p, b2)


def _gated_norm(o, gate, g):
    parts = []
    for h in range(4):
        sl = slice(128 * h, 128 * h + 128)
        oh = o[:, sl]
        ms = jnp.mean(oh * oh, axis=-1, keepdims=True)
        parts.append((oh * lax.rsqrt(ms + EPS) * g * _silu(gate[:, sl])).astype(BF16))
    return parts


def _cdpost_kernel(cf_ref, cb_ref, cg_ref, df_ref, db_ref, dr_ref, gc_ref, gd_ref, o_ref):
    for h, part in enumerate(_gated_norm(cf_ref[0] + cb_ref[0], cg_ref[0], gc_ref[...])):
        o_ref[0, :, 128 * h:128 * h + 128] = part
    for h, part in enumerate(_gated_norm(df_ref[0] + db_ref[0], dr_ref[0], gd_ref[...])):
        o_ref[0, :, 512 + 128 * h:512 + 128 * h + 128] = part


def _cdpost_call(cf, cb, df, db, p, pt, gc, gd):
    tm = 256
    row = lambda col: pl.BlockSpec((1, tm, 512), lambda b, i: (b, i, col))
    vec = pl.BlockSpec((1, 128), lambda b, i: (0, 0))
    return pl.pallas_call(
        _cdpost_kernel,
        grid=(NB, T // tm),
        in_specs=[row(0), row(0), row(C_G), row(0), row(0), row(TAIL_R), vec, vec],
        out_specs=pl.BlockSpec((1, tm, 1024), lambda b, i: (b, i, 0)),
        out_shape=jax.ShapeDtypeStruct((NB, T, 1024), BF16),
        compiler_params=_cparams(("parallel", "parallel"), 32),
        name="scan_post",
    )(cf, cb, p, df, db, pt, gc, gd)


def _swiglu_split(h, w1_refs, w3_refs, w2_refs, lead):
    get = (lambda r: r[...]) if lead is None else (lambda r: r[lead])
    kc = D // W_SPLIT
    u = v = None
    for k in range(W_SPLIT):
        hk = h[:, k * kc:(k + 1) * kc]
        pu = jnp.dot(hk, get(w1_refs[k]).astype(BF16), preferred_element_type=F32)
        pv = jnp.dot(hk, get(w3_refs[k]).astype(BF16), preferred_element_type=F32)
        u = pu if u is None else u + pu
        v = pv if v is None else v + pv
    g = (_silu(u) * v).astype(BF16)
    return [jnp.dot(g, get(w2_refs[k]).astype(BF16), preferred_element_type=F32) for k in range(W_SPLIT)]


def _ffn_kernel(h_ref, *refs):
    w1_refs, w3_refs, w2_refs = refs[:W_SPLIT], refs[W_SPLIT:2 * W_SPLIT], refs[2 * W_SPLIT:3 * W_SPLIT]
    o_ref = refs[3 * W_SPLIT]

    @pl.when(pl.program_id(2) == 0)
    def _():
        o_ref[...] = jnp.zeros_like(o_ref)

    nc = D // W_SPLIT
    for k, part in enumerate(_swiglu_split(h_ref[0], w1_refs, w3_refs, w2_refs, None)):
        o_ref[0, :, k * nc:(k + 1) * nc] += part


def _ffn_call(h, w1, w3, w2):
    tm, tf = 1152, 512
    kc = D // W_SPLIT
    up = [pl.BlockSpec((kc, tf), functools.partial(lambda b, i, f, k: (k, f), k=k)) for k in range(W_SPLIT)]
    down = [pl.BlockSpec((tf, kc), functools.partial(lambda b, i, f, k: (f, k), k=k)) for k in range(W_SPLIT)]
    return pl.pallas_call(
        _ffn_kernel,
        grid=(NB, T // tm, FFN_DENSE // tf),
        in_specs=[pl.BlockSpec((1, tm, D), lambda b, i, f: (b, i, 0))] + up + up + down,
        out_specs=pl.BlockSpec((1, tm, D), lambda b, i, f: (b, i, 0), pipeline_mode=pl.Buffered(1)),
        out_shape=jax.ShapeDtypeStruct((NB, T, D), F32),
        compiler_params=_cparams(("parallel", "parallel", "arbitrary"), 56),
        name="dense_ffn",
    )(h, *([w1] * W_SPLIT), *([w3] * W_SPLIT), *([w2] * W_SPLIT))


MOE_TM = 1152
MOE_SUB = 192
MOE_CASES = (5, 6)
MOE_TF = 512
MOE_NF = FFN_EXPERT // MOE_TF
MOE_TILES = -(-(2 * NB * SEQ + N_EXPERTS * (MOE_TM - 1)) // MOE_TM)
MOE_ROWS = MOE_TILES * MOE_TM
MOE_FETCH_ROWS = MOE_TM // MOE_NF
MOE_FETCH_LEFT = MOE_TM - MOE_NF * MOE_FETCH_ROWS


def _route_kernel(x_ref, g_ref, sh_ref, sc_ref, wr_ref, h_ref, r_ref):
    x = x_ref[0]
    ms = jnp.mean(x * x, axis=-1, keepdims=True)
    h = x * lax.rsqrt(ms + EPS) * g_ref[...] * (1.0 + sc_ref[0]) + sh_ref[0]
    h_ref[0] = h
    wr = wr_ref[...]
    h_hi = h.astype(BF16)
    h_lo = (h - h_hi.astype(F32)).astype(BF16)
    w_hi = wr.astype(BF16)
    w_lo = (wr - w_hi.astype(F32)).astype(BF16)
    dot = lambda a, b: jnp.dot(a, b, preferred_element_type=F32)
    logits = dot(h_hi, w_hi) + (dot(h_lo, w_hi) + dot(h_hi, w_lo))
    lane = lax.broadcasted_iota(jnp.int32, logits.shape, 1)
    lanef = lane.astype(F32)
    lg = jnp.where(lane < N_EXPERTS, logits, -jnp.inf)
    m1 = jnp.max(lg, axis=-1, keepdims=True)
    i1 = jnp.min(jnp.where(lg == m1, lanef, 128.0), axis=-1, keepdims=True)
    lg2 = jnp.where(lanef == i1, -jnp.inf, lg)
    m2 = jnp.max(lg2, axis=-1, keepdims=True)
    i2 = jnp.min(jnp.where(lg2 == m2, lanef, 128.0), axis=-1, keepdims=True)
    e = jnp.exp(m2 - m1)
    w1 = 1.0 / (1.0 + e)
    w2 = e * w1
    r_ref[0] = jnp.where(lane == 0, i1, jnp.where(lane == 1, i2, jnp.where(lane == 2, w1,
                         jnp.where(lane == 3, w2, 0.0))))


def _route_call(xall, g, mods, wr):
    tm = 256
    return pl.pallas_call(
        _route_kernel,
        grid=(NB, SEQ // tm),
        in_specs=[
            pl.BlockSpec((1, tm, D), lambda b, i: (b, i + NCTX // tm, 0)),
            pl.BlockSpec((1, D), lambda b, i: (0, 0)),
            _mod_spec(3, False), _mod_spec(4, False),
            pl.BlockSpec((D, 128), lambda b, i: (0, 0)),
        ],
        out_specs=[pl.BlockSpec((1, tm, D), lambda b, i: (b, i, 0)),
                   pl.BlockSpec((1, tm, 128), lambda b, i: (b, i, 0))],
        out_shape=[jax.ShapeDtypeStruct((NB, SEQ, D), F32), jax.ShapeDtypeStruct((NB, SEQ, 128), F32)],
        compiler_params=_cparams(("parallel", "parallel"), 32),
        name="moe_route",
    )(xall, g, mods, mods, wr)


def _moe_kernel(te_ref, nl_ref, src_ref, h_ref, *refs):
    w1_refs, w3_refs, w2_refs = refs[:W_SPLIT], refs[W_SPLIT:2 * W_SPLIT], refs[2 * W_SPLIT:3 * W_SPLIT]
    o_ref, xg_ref, xb_ref, sem = refs[3 * W_SPLIT:]
    i = pl.program_id(0)
    f = pl.program_id(1)
    n_live = nl_ref[i]
    live = n_live > 0
    prev_live = nl_ref[jnp.maximum(i - 1, 0)] > 0
    nxt = jnp.minimum(i + 1, MOE_TILES - 1)

    def row_copy(tile, r):
        row = src_ref[tile * MOE_TM + r]
        return pltpu.make_async_copy(h_ref.at[pl.ds(row, 1), :], xg_ref.at[pl.ds(r, 1), :], sem)

    def wait_rows():
        pltpu.make_async_copy(h_ref.at[pl.ds(0, MOE_TM), :], xg_ref, sem).wait()

    @pl.when(f == 0)
    def _():
        o_ref[...] = jnp.zeros_like(o_ref)

    @pl.when((i == 0) & (f == 0) & live)
    def _():
        def body(r, c):
            row_copy(0, r).start()
            return c

        lax.fori_loop(0, MOE_TM, body, 0)
        wait_rows()

    @pl.when((i > 0) & (f == 0) & prev_live)
    def _():
        wait_rows()

    @pl.when((f == 0) & live)
    def _():
        xb_ref[...] = xg_ref[...].astype(BF16)

    def step(n):
        def body():
            @pl.when(f < MOE_FETCH_LEFT)
            def _():
                row_copy(nxt, MOE_NF * MOE_FETCH_ROWS + f).start()

            for r in range(MOE_FETCH_ROWS):
                row_copy(nxt, f * MOE_FETCH_ROWS + r).start()
            rows = slice(0, n * MOE_SUB)
            nc = D // W_SPLIT
            for k, part in enumerate(_swiglu_split(xb_ref[rows, :], w1_refs, w3_refs, w2_refs, 0)):
                o_ref[rows, k * nc:(k + 1) * nc] += part

        return body

    n_case = jnp.maximum(n_live, MOE_CASES[0])
    for n in MOE_CASES:
        pl.when(live & (n_case == n))(step(n))

    @pl.when((i == MOE_TILES - 1) & (f == MOE_NF - 1) & live)
    def _():
        wait_rows()


def _moe_call(tile_expert, tile_live, src_rows, h, w1, w3, w2):
    nf = MOE_NF

    def fidx(i, f, nl):
        return jnp.where(nl[i] > 0, f, nf - 1)

    kc = D // W_SPLIT
    up = [pl.BlockSpec((1, kc, MOE_TF), functools.partial(lambda i, f, te, nl, src, k: (te[i], k, fidx(i, f, nl)), k=k))
          for k in range(W_SPLIT)]
    down = [pl.BlockSpec((1, MOE_TF, kc), functools.partial(lambda i, f, te, nl, src, k: (te[i], fidx(i, f, nl), k), k=k))
            for k in range(W_SPLIT)]
    return pl.pallas_call(
        _moe_kernel,
        grid_spec=pltpu.PrefetchScalarGridSpec(
            num_scalar_prefetch=3,
            grid=(MOE_TILES, nf),
            in_specs=[pl.BlockSpec(memory_space=pl.ANY)] + up + up + down,
            out_specs=pl.BlockSpec((MOE_TM, D), lambda i, f, te, nl, src: (i, 0), pipeline_mode=pl.Buffered(1)),
            scratch_shapes=[pltpu.VMEM((MOE_TM, D), F32), pltpu.VMEM((MOE_TM, D), BF16),
                            pltpu.SemaphoreType.DMA(())],
        ),
        out_shape=jax.ShapeDtypeStruct((MOE_ROWS, D), F32),
        compiler_params=_cparams(("arbitrary", "arbitrary"), 58),
        name="moe_experts",
    )(tile_expert, tile_live, src_rows, h, *([w1] * W_SPLIT), *([w3] * W_SPLIT), *([w2] * W_SPLIT))


def _row_copy(src_hbm, row, dst_ref, r, sem):
    return pltpu.make_async_copy(src_hbm.at[pl.ds(row, 1), :], dst_ref.at[pl.ds(r, 1), :], sem)


def _combine_kernel(dest_ref, x_ref, r_ref, gm_ref, gf_ref, y_ref, o_ref, yb_ref, sem, *, tc, n_steps):
    step = pl.program_id(0) * (SEQ // tc) + pl.program_id(1)
    slot = lax.rem(step, 2)

    def start_rows(step_, slot_):
        def issue(r, c):
            tok = step_ * tc + r
            _row_copy(y_ref, dest_ref[2 * tok], yb_ref.at[slot_, 0], r, sem.at[slot_]).start()
            _row_copy(y_ref, dest_ref[2 * tok + 1], yb_ref.at[slot_, 1], r, sem.at[slot_]).start()
            return c

        lax.fori_loop(0, tc, issue, 0, unroll=8)

    @pl.when(step == 0)
    def _():
        start_rows(0, 0)

    @pl.when(step + 1 < n_steps)
    def _():
        start_rows(step + 1, 1 - slot)

    for j in range(2):
        pltpu.make_async_copy(y_ref.at[pl.ds(0, tc), :], yb_ref.at[slot, j], sem.at[slot]).wait()
    rt = r_ref[0]
    moe = rt[:, 2:3] * yb_ref[slot, 0] + rt[:, 3:4] * yb_ref[slot, 1]
    x = x_ref[0] + gm_ref[0] * moe
    ms = jnp.mean(x * x, axis=-1, keepdims=True)
    o_ref[0] = x * lax.rsqrt(ms + EPS) * gf_ref[...]


def _combine_call(dest, xall, route, mods, gfinal, y):
    tc = 256
    return pl.pallas_call(
        functools.partial(_combine_kernel, tc=tc, n_steps=NB * SEQ // tc),
        grid_spec=pltpu.PrefetchScalarGridSpec(
            num_scalar_prefetch=1,
            grid=(NB, SEQ // tc),
            in_specs=[
                pl.BlockSpec((1, tc, D), lambda b, i, d: (b, i + NCTX // tc, 0)),
                pl.BlockSpec((1, tc, 128), lambda b, i, d: (b, i, 0)),
                pl.BlockSpec((1, 1, D), lambda b, i, d: (b * 6 + 5, 0, 0)),
                pl.BlockSpec((1, D), lambda b, i, d: (0, 0)),
                pl.BlockSpec(memory_space=pl.ANY),
            ],
            out_specs=pl.BlockSpec((1, tc, D), lambda b, i, d: (b, i, 0)),
            scratch_shapes=[pltpu.VMEM((2, 2, tc, D), F32), pltpu.SemaphoreType.DMA((2,))],
        ),
        out_shape=jax.ShapeDtypeStruct((NB, SEQ, D), F32),
        compiler_params=_cparams(("arbitrary", "arbitrary"), 32),
        name="moe_combine",
    )(dest, xall, route, mods, gfinal, y)


def _moe_plan(route):
    n_pairs = 2 * NB * SEQ
    e = route[..., 0:2].astype(jnp.int32).reshape(n_pairs)
    onehot = (e[:, None] == jnp.arange(N_EXPERTS, dtype=jnp.int32)[None, :]).astype(jnp.int32)
    csum = jnp.cumsum(onehot, axis=0)
    rank = jnp.sum(onehot * csum, axis=1) - 1
    counts = csum[-1]
    padded = ((counts + MOE_TM - 1) // MOE_TM) * MOE_TM
    ends = jnp.cumsum(padded)
    starts = ends - padded
    dest = (starts[e] + rank).astype(jnp.int32)
    src_rows = jnp.zeros((MOE_ROWS,), jnp.int32).at[dest].set(jnp.arange(n_pairs, dtype=jnp.int32) // 2, unique_indices=True)
    tile_start = jnp.arange(MOE_TILES, dtype=jnp.int32) * MOE_TM
    t_eff = jnp.minimum(tile_start, ends[-1] - MOE_TM)
    tile_expert = jnp.sum((ends[None, :] <= t_eff[:, None]).astype(jnp.int32), axis=1)
    rows_left = (starts + counts)[tile_expert] - tile_start
    tile_live = jnp.clip((rows_left + MOE_SUB - 1) // MOE_SUB, 0, MOE_TM // MOE_SUB).astype(jnp.int32)
    return dest, src_rows, tile_expert.astype(jnp.int32), tile_live


def _rope_tables():
    per_axis = DIFF_QK // 4
    t = np.arange(SEQ)
    inv = ROPE_BASE ** (-np.arange(per_axis, dtype=np.float32) / per_axis)
    ang = np.concatenate([(t // GRID_W).astype(np.float32)[:, None] * inv,
                          (t % GRID_W).astype(np.float32)[:, None] * inv], axis=-1).astype(np.float32)
    cos = np.concatenate([np.ones((NCTX, 2 * per_axis), np.float32), np.cos(ang)], axis=0)
    sin = np.concatenate([np.zeros((NCTX, 2 * per_axis), np.float32), np.sin(ang)], axis=0)
    reps = 128 // (2 * per_axis)
    return jnp.asarray(np.tile(cos, (1, reps))), jnp.asarray(np.tile(sin, (1, reps)))


def _in_proj_tail(w_t):
    z = w_t[:, IN_MAIN:IN_MAIN + 2 * GLA_RANK]
    r = w_t[:, IN_MAIN + 2 * GLA_RANK:]
    pad = jnp.zeros((w_t.shape[0], TAIL_W - r.shape[1] - z.shape[1], D), w_t.dtype)
    return jnp.concatenate([r, z, pad], axis=1)


def kernel(x, c, ctx, c_ctx, ada_w, ada_b, norm_mix, norm_ffn, w_in, w_out, diff_lambda, diff_norm,
           na_rel_bias, hgrn_lower_bounds, hgrn_norm, gla_gate_w2, gla_gate_b, gla_norm,
           ffn_w1, ffn_w3, ffn_w2, moe_router, moe_w1, moe_w3, moe_w2, final_norm):
    lb_soft = jax.nn.softmax(hgrn_lower_bounds.astype(F32), axis=1)
    lower_bounds = jnp.clip(jnp.cumsum(lb_soft, axis=1) - lb_soft[:, :1], 0.0, 1.0 - 1e-6)
    cond8 = jnp.concatenate([c, c_ctx[None, :], jnp.zeros((8 - NB - 1, D), F32)], axis=0)
    mods_all = _ada_call(cond8, ada_w, ada_b).reshape(DEPTH, 8 * 6, 1, D)
    cos, sin = _rope_tables()
    xall = jnp.concatenate([ctx, x], axis=1)
    w_in_t = jnp.swapaxes(w_in, 1, 2)
    w_tail_t = _in_proj_tail(w_in_t)

    out = None
    pending = None
    for l in range(DEPTH):
        mods = mods_all[l]
        if pending is None:
            h = _norm_call(xall, norm_mix[l][None, :], mods, 0, 1)
        else:
            xall, h = _norm_call(xall, norm_mix[l][None, :], mods, 0, 1, resid=pending)
            pending = None
        p = _wsmm_call([h], w_in_t, l, 1664, IN_MAIN // 1664, w_t=True, vmem_mb=56, name="in_proj")
        pt = _wsmm_call([h], w_tail_t, l, TAIL_W, 1, w_t=True, vmem_mb=32, name="in_proj_tail")

        lambda_init = 0.8 - 0.6 * math.exp(-0.3 * l)
        lp = diff_lambda[l].astype(F32)
        lam = (jnp.exp(jnp.sum(lp[0] * lp[1])) - jnp.exp(jnp.sum(lp[2] * lp[3])) + lambda_init).reshape(1)
        a = _attn_a_call(p, lam, cos, sin, jnp.tile(diff_norm[l], 2)[None, :], 1.0 - lambda_init)
        n = _attn_na_call(p, _na_bias_table(na_rel_bias[l]))
        cf, cb = _hgrn_call(p, lower_bounds[:, l])
        w2p = jnp.zeros((2, 128, 4 * GLA_DK), F32)
        w2p = w2p.at[0, 0:GLA_RANK].set(gla_gate_w2[l, 0]).at[1, GLA_RANK:2 * GLA_RANK].set(gla_gate_w2[l, 1])
        df, db = _gla_call(p, pt, w2p, gla_gate_b[l][:, None, :])
        cd = _cdpost_call(cf, cb, df, db, p, pt, hgrn_norm[l][None, :], gla_norm[l][None, :])
        xall = _wsmm_call([a, n, cd], w_out, l, 1024, D // 1024, resid=(xall, mods, 2), name="out_proj")

        if l % 2 == 0:
            h2 = _norm_call(xall, norm_ffn[l][None, :], mods, 3, 4)
            y = _ffn_call(h2, ffn_w1[l // 2], ffn_w3[l // 2], ffn_w2[l // 2])
            pending = (y, mods, 5)
        else:
            assert l == DEPTH - 1
            wr = jnp.zeros((D, 128), F32).at[:, :N_EXPERTS].set(moe_router[l // 2])
            hr, route = _route_call(xall, norm_ffn[l][None, :], mods, wr)
            dest, src_rows, tile_expert, tile_live = _moe_plan(route)
            y = _moe_call(tile_expert, tile_live, src_rows, hr.reshape(NB * SEQ, D),
                          moe_w1[l // 2], moe_w3[l // 2], moe_w2[l // 2])
            out = _combine_call(dest, xall, route, mods, final_norm[None, :], y)
    return out
```

```python
import functools
import math

import numpy as np
import jax
import jax.numpy as jnp
from jax import lax
from jax.experimental import pallas as pl
from jax.experimental.pallas import tpu as pltpu

F32 = jnp.float32
BF16 = jnp.bfloat16

D = 2048
NB = 4
SEQ = 2048
NCTX = 256
T = NCTX + SEQ
DEPTH = 2
GRID_W = 64
ROPE_BASE = 10000.0
EPS = 1e-6
LOG2E = math.log2(math.e)

DIFF_QK = 32
NA_DIM = 64
NA_WIN_R = 8
NA_WIN_C = 16
NA_GROUP = 4
NA_KROWS = NA_GROUP + NA_WIN_R - 1
GLA_DK = 64
GLA_TAU = 16.0
GLA_RANK = 16
CH = 64
SUB = 8

FFN_DENSE = 5632
N_EXPERTS = 8
FFN_EXPERT = 7168

IN_MAIN = 6656
A_Q, A_K, A_V = 0, 4, 8
B_Q, B_K, B_V = 12, 16, 20
C_Q, C_I, C_FF, C_FB, C_G = 6, 7, 8, 9, 10
D_QK, D_V = 11, 12
TAIL_W = 640
TAIL_R, TAIL_Z = 0, 4

V7X_VMEM_BYTES = 64 * 1024 * 1024
W_SPLIT = 1


def _cparams(sem, vmem_mb):
    assert vmem_mb * 1024 * 1024 < V7X_VMEM_BYTES
    return pltpu.CompilerParams(dimension_semantics=sem, vmem_limit_bytes=vmem_mb * 1024 * 1024)


def _sigmoid(x):
    return 1.0 / (1.0 + jnp.exp(-x))


def _silu(x):
    return x * _sigmoid(x)


def _log_sigmoid(x):
    return jnp.minimum(x, 0.0) - jnp.log(1.0 + jnp.exp(-jnp.abs(x)))


def _nt(a, b):
    return lax.dot_general(a, b, (((1,), (1,)), ((), ())), preferred_element_type=F32)


def _tn(a, b):
    return lax.dot_general(a, b, (((0,), (0,)), ((), ())), preferred_element_type=F32)


def _interleave(gens):
    live = list(gens)
    while live:
        still = []
        for gen in live:
            try:
                next(gen)
                still.append(gen)
            except StopIteration:
                pass
        live = still


def _ada_kernel(c_ref, *refs):
    w_refs, b_ref, o_ref = refs[:W_SPLIT], refs[W_SPLIT], refs[W_SPLIT + 1]
    s = _silu(c_ref[...]).astype(BF16)
    kc = D // W_SPLIT
    acc = b_ref[0]
    for k in range(W_SPLIT):
        acc = acc + jnp.dot(s[:, k * kc:(k + 1) * kc], w_refs[k][0].astype(BF16), preferred_element_type=F32)
    o_ref[0] = acc


def _ada_call(cond8, ada_w, ada_b):
    tn = 1536
    n = 6 * D
    kc = D // W_SPLIT
    w_specs = [pl.BlockSpec((1, kc, tn), functools.partial(lambda l, j, k: (l, k, j), k=k)) for k in range(W_SPLIT)]
    return pl.pallas_call(
        _ada_kernel,
        grid=(DEPTH, n // tn),
        in_specs=[pl.BlockSpec((8, D), lambda l, j: (0, 0))] + w_specs
        + [pl.BlockSpec((1, 1, tn), lambda l, j: (l, 0, j))],
        out_specs=pl.BlockSpec((1, 8, tn), lambda l, j: (l, 0, j)),
        out_shape=jax.ShapeDtypeStruct((DEPTH, 8, n), F32),
        compiler_params=_cparams(("parallel", "parallel"), 40),
        name="ada_mod",
    )(cond8, *([ada_w] * W_SPLIT), ada_b.reshape(DEPTH, 1, n))


def _mod_spec(k, ctx):
    if ctx:
        return pl.BlockSpec((1, 1, D), lambda b, i: (NB * 6 + k, 0, 0))
    return pl.BlockSpec((1, 1, D), lambda b, i: (b * 6 + k, 0, 0))


def _row_is_ctx(row0, tm):
    return (row0 + lax.broadcasted_iota(jnp.int32, (tm, 1), 0)) < NCTX


def _modnorm(x, g, is_c, shl, scl, shc, scc):
    ms = jnp.mean(x * x, axis=-1, keepdims=True)
    y = x * lax.rsqrt(ms + EPS) * g
    sc = jnp.where(is_c, scc, scl)
    sh = jnp.where(is_c, shc, shl)
    return y * (1.0 + sc) + sh


def _norm_kernel(*refs, tm, resid):
    if resid:
        x_ref, y_ref, gl_ref, gx_ref, g_ref, shl_ref, scl_ref, shc_ref, scc_ref, xo_ref, h_ref = refs
    else:
        x_ref, g_ref, shl_ref, scl_ref, shc_ref, scc_ref, h_ref = refs
    is_c = _row_is_ctx(pl.program_id(1) * tm, tm)
    x = x_ref[0]
    if resid:
        x = x + jnp.where(is_c, gx_ref[0], gl_ref[0]) * y_ref[0]
        xo_ref[0] = x
    h_ref[0] = _modnorm(x, g_ref[...], is_c, shl_ref[0], scl_ref[0], shc_ref[0], scc_ref[0]).astype(BF16)


def _norm_call(xall, g, mods, k_shift, k_scale, resid=None):
    tm = 384
    row = pl.BlockSpec((1, tm, D), lambda b, i: (b, i, 0))
    in_specs, args = [row], [xall]
    if resid is not None:
        y, mods_prev, k_gate = resid
        in_specs += [row, _mod_spec(k_gate, False), _mod_spec(k_gate, True)]
        args += [y, mods_prev, mods_prev]
    in_specs += [pl.BlockSpec((1, D), lambda b, i: (0, 0)),
                 _mod_spec(k_shift, False), _mod_spec(k_scale, False),
                 _mod_spec(k_shift, True), _mod_spec(k_scale, True)]
    args += [g, mods, mods, mods, mods]
    h_shape = jax.ShapeDtypeStruct((NB, T, D), BF16)
    out_shape, out_specs = h_shape, row
    if resid is not None:
        out_shape, out_specs = [jax.ShapeDtypeStruct((NB, T, D), F32), h_shape], [row, row]
    return pl.pallas_call(
        functools.partial(_norm_kernel, tm=tm, resid=resid is not None),
        grid=(NB, T // tm),
        in_specs=in_specs, out_specs=out_specs, out_shape=out_shape,
        compiler_params=_cparams(("parallel", "parallel"), 32),
        name="mod_norm",
    )(*args)


def _wsmm_kernel(*refs, ksizes, tm, epilogue, w_t):
    n_in = len(ksizes)
    ins, w_ref = refs[:n_in], refs[n_in]
    o_ref, wb_ref = refs[-2], refs[-1]

    @pl.when((pl.program_id(1) == 0) & (pl.program_id(2) == 0))
    def _():
        wb_ref[...] = w_ref[0].astype(BF16)

    acc, k0 = None, 0
    for r, ks in zip(ins, ksizes):
        if w_t:
            part = _nt(r[0], wb_ref[:, k0:k0 + ks])
        else:
            part = jnp.dot(r[0], wb_ref[k0:k0 + ks, :], preferred_element_type=F32)
        acc = part if acc is None else acc + part
        k0 += ks
    if epilogue:
        x_ref, gl_ref, gx_ref = refs[n_in + 1:n_in + 4]
        gate = jnp.where(_row_is_ctx(pl.program_id(2) * tm, tm), gx_ref[0], gl_ref[0])
        acc = x_ref[0] + gate * acc
    o_ref[0] = acc


def _wsmm_call(acts, w, layer, tn, n_j, resid=None, w_t=False, vmem_mb=48, name="matmul"):
    tm = 768
    ksizes = tuple(a.shape[-1] for a in acts)
    kdim = sum(ksizes)
    in_specs = [pl.BlockSpec((1, tm, ks), lambda j, b, i: (b, i, 0)) for ks in ksizes]
    if w_t:
        in_specs.append(pl.BlockSpec((1, tn, kdim), lambda j, b, i: (layer, j, 0)))
    else:
        in_specs.append(pl.BlockSpec((1, kdim, tn), lambda j, b, i: (layer, 0, j)))
    args = list(acts) + [w]
    if resid is not None:
        x, mods, k_gate = resid
        in_specs += [pl.BlockSpec((1, tm, tn), lambda j, b, i: (b, i, j)),
                     pl.BlockSpec((1, 1, tn), lambda j, b, i: (b * 6 + k_gate, 0, j)),
                     pl.BlockSpec((1, 1, tn), lambda j, b, i: (NB * 6 + k_gate, 0, j))]
        args += [x, mods, mods]
    return pl.pallas_call(
        functools.partial(_wsmm_kernel, ksizes=ksizes, tm=tm, epilogue=resid is not None, w_t=w_t),
        grid=(n_j, NB, T // tm),
        in_specs=in_specs,
        out_specs=pl.BlockSpec((1, tm, tn), lambda j, b, i: (b, i, j)),
        out_shape=jax.ShapeDtypeStruct((NB, T, n_j * tn), F32),
        scratch_shapes=[pltpu.VMEM((tn, kdim) if w_t else (kdim, tn), BF16)],
        compiler_params=_cparams(("arbitrary", "arbitrary", "arbitrary"), vmem_mb),
        name=name,
    )(*args)


def _rope(x, cos, sin):
    lane = lax.broadcasted_iota(jnp.int32, x.shape, 1)
    first = (lane & (DIFF_QK - 1)) < (DIFF_QK // 2)
    rot = jnp.where(first, -pltpu.roll(x, 128 - DIFF_QK // 2, 1), pltpu.roll(x, DIFF_QK // 2, 1))
    return x * cos + rot * sin


def _attn_a_kernel(lam_ref, q_ref, k_ref, v_ref, cos_ref, sin_ref, g_ref, o_ref, ks_ref, va_ref, *, post_scale):
    t = pl.program_id(2)
    lane = lax.broadcasted_iota(jnp.int32, (1, 128), 1)
    ones_lane = (64, 0)

    @pl.when(t == 0)
    def _():
        ks_ref[...] = _rope(k_ref[0], cos_ref[...], sin_ref[...]).astype(BF16)
        v = v_ref[0]
        va_ref[0] = jnp.where(lane < 64, v, jnp.where(lane == ones_lane[0], 1.0, 0.0)).astype(BF16)
        va_ref[1] = jnp.where(lane >= 64, v, jnp.where(lane == ones_lane[1], 1.0, 0.0)).astype(BF16)

    lam = lam_ref[0]

    def attend(q0, nq, nk):
        rows = pl.ds(q0, nq)
        q = _rope(q_ref[0, rows, :], cos_ref[rows, :], sin_ref[rows, :]) * (DIFF_QK ** -0.5 * LOG2E)
        kk = ks_ref[0:nk, :]
        terms = {}

        def softmax_pv(hh, m):
            lo = 64 * hh + DIFF_QK * m
            qm = jnp.where((lane >= lo) & (lane < lo + DIFF_QK), q, 0.0).astype(BF16)
            s = _nt(qm, kk)
            yield
            mx = jnp.max(s, axis=-1, keepdims=True)
            yield
            e = jnp.exp2((s - mx).astype(BF16))
            yield
            num = jnp.dot(e, va_ref[hh, 0:nk, :], preferred_element_type=F32)
            yield
            den = jnp.sum(jnp.where(lane == ones_lane[hh], num, 0.0), axis=-1, keepdims=True)
            terms[hh, m] = num * ((1.0 if m == 0 else lam) / den)

        _interleave([softmax_pv(hh, m) for hh in range(2) for m in range(2)])
        outs = [terms[hh, 0] - terms[hh, 1] for hh in range(2)]
        o = jnp.where(lane < 64, outs[0], outs[1])
        sq = o * o
        s0 = jnp.sum(jnp.where(lane < 64, sq, 0.0), axis=-1, keepdims=True)
        s1 = jnp.sum(jnp.where(lane >= 64, sq, 0.0), axis=-1, keepdims=True)
        ms = jnp.where(lane < 64, s0, s1) * (1.0 / 64.0)
        o_ref[0, rows, :] = (o * lax.rsqrt(ms + EPS) * (g_ref[...] * post_scale)).astype(o_ref.dtype)

    @pl.when(t == 0)
    def _():
        attend(0, NCTX, NCTX)

    @pl.when(t > 0)
    def _():
        attend(pl.multiple_of(NCTX + (t - 1) * ATTN_TQ, ATTN_TQ // 2), ATTN_TQ, T)


ATTN_TQ = 512


def _attn_a_call(p, lam, cos, sin, g2, post_scale):
    full = lambda col: pl.BlockSpec((1, T, 128), lambda b, h, t: (b, 0, col + h))
    tab = pl.BlockSpec((T, 128), lambda b, h, t: (0, 0))
    return pl.pallas_call(
        functools.partial(_attn_a_kernel, post_scale=post_scale),
        grid=(NB, 4, 1 + SEQ // ATTN_TQ),
        in_specs=[pl.BlockSpec(memory_space=pltpu.SMEM), full(A_Q), full(A_K), full(A_V), tab, tab,
                  pl.BlockSpec((1, 128), lambda b, h, t: (0, 0))],
        out_specs=pl.BlockSpec((1, T, 128), lambda b, h, t: (b, 0, h)),
        out_shape=jax.ShapeDtypeStruct((NB, T, 512), BF16),
        scratch_shapes=[pltpu.VMEM((T, 128), BF16), pltpu.VMEM((2, T, 128), BF16)],
        compiler_params=_cparams(("parallel", "parallel", "arbitrary"), 48),
        name="diff_attn",
    )(lam, p, p, p, cos, sin, g2)


def _attn_na_kernel(q_ref, k_ref, v_ref, bias_ref, o_ref):
    g = pl.program_id(1)
    lane = lax.broadcasted_iota(jnp.int32, (1, 128), 1)

    def run(chain):
        results = {}
        gens = []
        for p in range(4):
            sl = slice(128 * p, 128 * p + 128)
            q = q_ref[0, :, sl] * (NA_DIM ** -0.5 * LOG2E)
            for hh in range(2):
                qm = jnp.where((lane >= 64 * hh) & (lane < 64 * hh + 64), q, 0.0).astype(BF16)
                gens.append(chain(2 * p + hh, qm, sl, results))
        _interleave(gens)
        for p in range(4):
            o_ref[0, :, 128 * p:128 * p + 128] = jnp.where(lane < 64, results[2 * p], results[2 * p + 1]
                                                           ).astype(o_ref.dtype)

    @pl.when(g == 0)
    def _():
        def ctx_head(h, qm, sl, results):
            s = _nt(qm, k_ref[0, 0:NCTX, sl].astype(BF16))
            yield
            e = jnp.exp2(s - jnp.max(s, axis=-1, keepdims=True))
            yield
            r = 1.0 / jnp.sum(e, axis=-1, keepdims=True)
            results[h] = jnp.dot(e.astype(BF16), v_ref[0, 0:NCTX, sl].astype(BF16), preferred_element_type=F32) * r

        run(ctx_head)

    @pl.when(g > 0)
    def _():
        u0 = jnp.clip(NA_GROUP * (g - 1) - NA_WIN_R // 2, 0, SEQ // GRID_W - NA_KROWS)
        win = pl.ds(pl.multiple_of(NCTX + GRID_W * u0, GRID_W), NA_KROWS * GRID_W)

        def lat_head(h, qm, sl, results):
            sw = _nt(qm, k_ref[0, win, sl].astype(BF16)) + bias_ref[h, 0]
            sc = _nt(qm, k_ref[0, 0:NCTX, sl].astype(BF16))
            yield
            mx = jnp.maximum(jnp.max(sw, axis=-1, keepdims=True), jnp.max(sc, axis=-1, keepdims=True))
            yield
            ew = jnp.exp2(sw - mx)
            ec = jnp.exp2(sc - mx)
            yield
            r = 1.0 / (jnp.sum(ew, axis=-1, keepdims=True) + jnp.sum(ec, axis=-1, keepdims=True))
            o = (jnp.dot(ew.astype(BF16), v_ref[0, win, sl].astype(BF16), preferred_element_type=F32)
                 + jnp.dot(ec.astype(BF16), v_ref[0, 0:NCTX, sl].astype(BF16), preferred_element_type=F32))
            yield
            results[h] = o * r

        run(lat_head)


def _attn_na_call(p, bias):
    tq = NA_GROUP * GRID_W
    nk = NA_KROWS * GRID_W
    n_grp = SEQ // tq

    def bias_map(b, g):
        grp = jnp.maximum(g - 1, 0)
        return (0, jnp.where(grp == 0, 0, jnp.where(grp == n_grp - 1, 2, 1)), 0, 0)

    return pl.pallas_call(
        _attn_na_kernel,
        grid=(NB, T // tq),
        in_specs=[
            pl.BlockSpec((1, tq, 512), lambda b, g: (b, g, B_Q // 4)),
            pl.BlockSpec((1, T, 512), lambda b, g: (b, 0, B_K // 4)),
            pl.BlockSpec((1, T, 512), lambda b, g: (b, 0, B_V // 4)),
            pl.BlockSpec((8, 1, tq, nk), bias_map),
        ],
        out_specs=pl.BlockSpec((1, tq, 512), lambda b, g: (b, g, 0)),
        out_shape=jax.ShapeDtypeStruct((NB, T, 512), BF16),
        compiler_params=_cparams(("parallel", "arbitrary"), 48),
        name="nbr_attn",
    )(p, p, p, bias)


def _na_bias_table(rel_bias):
    rows = SEQ // GRID_W
    n_grp = rows // NA_GROUP
    cq = np.arange(GRID_W)
    col_start = np.clip(cq - NA_WIN_C // 2, 0, GRID_W - NA_WIN_C)
    col_ok = (cq[None, :] >= col_start[:, None]) & (cq[None, :] < col_start[:, None] + NA_WIN_C)
    dc = np.clip(cq[None, :] - cq[:, None], -(NA_WIN_C - 1), NA_WIN_C - 1) + (NA_WIN_C - 1)
    gi = np.arange(n_grp)[:, None, None]
    qr = np.arange(NA_GROUP)[None, :, None]
    ku = np.arange(NA_KROWS)[None, None, :]
    r = NA_GROUP * gi + qr
    u = np.clip(NA_GROUP * gi - NA_WIN_R // 2, 0, rows - NA_KROWS) + ku
    row_start = np.clip(r - NA_WIN_R // 2, 0, rows - NA_WIN_R)
    row_ok = (u >= row_start) & (u < row_start + NA_WIN_R)
    dr = np.where(row_ok, u - r + (NA_WIN_R - 1), -1)
    for g in range(2, n_grp - 1):
        assert np.array_equal(dr[g], dr[1])
    assert np.all((dc == cq[None, :] - cq[:, None] + NA_WIN_C - 1)[col_ok])
    rbp = jnp.pad(rel_bias.astype(F32) * LOG2E, ((0, 0), (0, 0), (GRID_W, GRID_W)))
    rbt = jnp.stack([rbp[:, :, GRID_W + NA_WIN_C - 1 - q:2 * GRID_W + NA_WIN_C - 1 - q] for q in range(GRID_W)],
                    axis=2)
    rbm = jnp.where(col_ok[None, None], rbt, -jnp.inf)
    ninf = jnp.full((rel_bias.shape[0], GRID_W, GRID_W), -jnp.inf, F32)
    pats = []
    for g in (0, 1, n_grp - 1):
        qrows = []
        for a in range(NA_GROUP):
            blocks = [rbm[:, dr[g, a, b]] if dr[g, a, b] >= 0 else ninf for b in range(NA_KROWS)]
            qrows.append(jnp.concatenate(blocks, axis=-1))
        pats.append(jnp.concatenate(qrows, axis=-2))
    return jnp.stack(pats, axis=1)


def _scan_consts(rev):
    r2 = lax.broadcasted_iota(jnp.int32, (CH, CH), 0)
    c2 = lax.broadcasted_iota(jnp.int32, (CH, CH), 1)
    tri = jnp.where((c2 >= r2) if rev else (c2 <= r2), 1.0, 0.0).astype(BF16)
    levels = []
    n = CH // 2
    while n >= SUB:
        sh = (2 * n).bit_length() - 1
        same = (r2 >> sh) == (c2 >> sh)
        rin = r2 & (2 * n - 1)
        cin = c2 & (2 * n - 1)
        levels.append((same & (rin < n) & (cin >= n)) if rev else (same & (rin >= n) & (cin < n)))
        n //= 2
    sub_sh = SUB.bit_length() - 1
    dvalid = ((r2 >> sub_sh) == (c2 >> sub_sh)) & ((c2 >= r2) if rev else (c2 <= r2))
    hits = [(c2 & (SUB - 1)) == s for s in range(SUB)]
    return tri, levels, dvalid, hits


def _cumsum_rows(g, tri):
    g1 = g.astype(BF16)
    r1 = g - g1.astype(F32)
    g2 = r1.astype(BF16)
    g3 = (r1 - g2.astype(F32)).astype(BF16)
    dot = lambda a: jnp.dot(tri, a, preferred_element_type=F32)
    return dot(g1) + dot(g2) + dot(g3)


def _block_rows(a, first, period):
    return jnp.concatenate(
        [jnp.broadcast_to(a[r:r + 1, :], (period, 128)) for r in range(first, CH, period)], axis=0)


def _gla_chunk(load, get_state, put, masks, rev, consts):
    tri, level_valid, dvalid, hits = consts
    q, k, g, vs = load()
    nh = len(vs)
    b = _cumsum_rows(g * LOG2E, tri)
    b_end = b[0:1] if rev else b[CH - 1:CH]
    yield

    def headq(a, h):
        return a if masks[h] is None else jnp.where(masks[h], a, 0.0)

    atts = [jnp.zeros((CH, CH), F32) for _ in range(nh)]
    n = CH // 2
    for valid in level_valid:
        first = n if rev else n - 1
        ref = _block_rows(b, first, 2 * n) if 2 * n < CH else b[first:first + 1]
        qn = (q * jnp.exp2(b - ref)).astype(BF16)
        kn = (k * jnp.exp2(ref - b)).astype(BF16)
        for h in range(nh):
            a = _nt(headq(qn, h), kn)
            atts[h] = jnp.where(valid, a, atts[h])
        n //= 2
        yield

    diag = [jnp.zeros((CH, CH), F32) for _ in range(nh)]
    for s in range(SUB):
        e = q * _block_rows(k, s, SUB) * jnp.exp2(b - _block_rows(b, s, SUB))
        for h in range(nh):
            rs = jnp.sum(headq(e, h), axis=-1, keepdims=True)
            diag[h] = jnp.where(hits[s], rs, diag[h])
        yield

    qe = (q * jnp.exp2(b)).astype(BF16)
    kd = k * jnp.exp2(b_end - b)
    dec = jnp.exp2(b_end)
    yield
    sts = get_state()
    outs, new = [], []
    for h in range(nh):
        att = jnp.where(dvalid, diag[h], atts[h])
        vb = vs[h].astype(BF16)
        o = jnp.dot(att.astype(BF16), vb, preferred_element_type=F32) + _nt(qe, sts[h].astype(BF16))
        outs.append(o)
        new.append(sts[h] * dec + _tn(vb, headq(kd, h).astype(BF16)))
    put(outs, new)


SCAN_CHUNKS = 4
SCAN_ROWS = SCAN_CHUNKS * CH


def _chunk_maps(col_block):
    n_c = NCTX // SCAN_ROWS
    n_all = T // SCAN_ROWS

    def fwd(b, s):
        return (b, s, col_block)

    def bwd(b, s):
        return (b, jnp.where(s < n_c, n_c - 1 - s, n_all + n_c - 1 - s), col_block)

    return fwd, bwd


def _scan_order(rev):
    return tuple(reversed(range(SCAN_CHUNKS))) if rev else tuple(range(SCAN_CHUNKS))


def _hgrn_kernel(qf_ref, if_ref, ff_ref, qb_ref, ib_ref, fb_ref, lb_ref, of_ref, ob_ref, st_ref):
    @pl.when(pl.program_id(1) == 0)
    def _():
        st_ref[...] = jnp.zeros_like(st_ref)

    dirs = ((qf_ref, if_ref, ff_ref, of_ref), (qb_ref, ib_ref, fb_ref, ob_ref))
    consts = [_scan_consts(False), _scan_consts(True)]
    gens, finals = [], []
    for d, (q_ref, i_ref, f_ref, o_ref) in enumerate(dirs):
        for h in range(4):
            sl = slice(128 * h, 128 * h + 128)
            state = [[st_ref[d, h]]]
            finals.append((d, h, state))
            for j in _scan_order(d == 1):
                rows = slice(CH * j, CH * j + CH)

                def load(q_ref=q_ref, i_ref=i_ref, f_ref=f_ref, d=d, sl=sl, rows=rows):
                    lb = lb_ref[d:d + 1, sl]
                    q = _silu(q_ref[0, rows, sl])
                    f = f_ref[0, rows, sl]
                    e = jnp.exp(-jnp.abs(f))
                    inv = 1.0 / (1.0 + e)
                    k = (1.0 - lb) * jnp.where(f >= 0.0, e * inv, inv)
                    a = jnp.log(lb)
                    c = jnp.log(1.0 - lb) + (jnp.minimum(f, 0.0) - jnp.log(1.0 + e))
                    g = jnp.maximum(a, c) + jnp.log(1.0 + jnp.exp(-jnp.abs(a - c)))
                    return q, k, g, [i_ref[0, rows, sl]]

                def put(outs, new, o_ref=o_ref, sl=sl, rows=rows, state=state):
                    o_ref[0, rows, sl] = outs[0]
                    state[0] = new

                gens.append(_gla_chunk(load, lambda state=state: state[0], put, [None], d == 1, consts[d]))
    _interleave(gens)
    for d, h, state in finals:
        st_ref[d, h] = state[0][0]


def _hgrn_call(p, lb):
    fq, bq = _chunk_maps(C_Q)
    fi, bi = _chunk_maps(C_I)
    ff, _ = _chunk_maps(C_FF)
    _, bf = _chunk_maps(C_FB)
    fo, bo = _chunk_maps(0)
    blk = lambda m: pl.BlockSpec((1, SCAN_ROWS, 512), m)
    return pl.pallas_call(
        _hgrn_kernel,
        grid=(NB, T // SCAN_ROWS),
        in_specs=[blk(fq), blk(fi), blk(ff), blk(bq), blk(bi), blk(bf),
                  pl.BlockSpec((2, 512), lambda b, s: (0, 0))],
        out_specs=[blk(fo), blk(bo)],
        out_shape=[jax.ShapeDtypeStruct((NB, T, 512), F32)] * 2,
        scratch_shapes=[pltpu.VMEM((2, 4, 128, 128), F32)],
        compiler_params=_cparams(("parallel", "arbitrary"), 32),
        name="hgrn2_scan",
    )(p, p, p, p, p, p, lb)


def _gla_kernel(qkf_ref, vf_ref, zf_ref, qkb_ref, vb_ref, zb_ref, w2_ref, b2_ref, of_ref, ob_ref, st_ref):
    @pl.when(pl.program_id(1) == 0)
    def _():
        st_ref[...] = jnp.zeros_like(st_ref)

    lane = lax.broadcasted_iota(jnp.int32, (1, 128), 1)
    masks = [lane < GLA_DK, lane >= GLA_DK]
    dirs = ((qkf_ref, vf_ref, zf_ref, of_ref), (qkb_ref, vb_ref, zb_ref, ob_ref))
    consts = [_scan_consts(False), _scan_consts(True)]
    gens, finals = [], []
    for d, (qk_ref, v_ref, z_ref, o_ref) in enumerate(dirs):
        logit = jnp.dot(z_ref[0].astype(BF16), w2_ref[d].astype(BF16), preferred_element_type=F32) + b2_ref[d]
        g = _log_sigmoid(logit) * (1.0 / GLA_TAU)
        for grp in range(2):
            sl = slice(128 * grp, 128 * grp + 128)
            heads = (2 * grp, 2 * grp + 1)
            state = [[st_ref[d, h] for h in heads]]
            finals.append((d, heads, state))
            for j in _scan_order(d == 1):
                rows = slice(CH * j, CH * j + CH)

                def load(qk_ref=qk_ref, v_ref=v_ref, g=g, grp=grp, sl=sl, heads=heads, rows=rows):
                    q = qk_ref[0, rows, sl] * (GLA_DK ** -0.5)
                    k = qk_ref[0, rows, 256 + 128 * grp:256 + 128 * grp + 128]
                    return q, k, g[rows, sl], [v_ref[0, rows, 128 * h:128 * h + 128] for h in heads]

                def put(outs, new, o_ref=o_ref, heads=heads, rows=rows, state=state):
                    for o, h in zip(outs, heads):
                        o_ref[0, rows, 128 * h:128 * h + 128] = o
                    state[0] = new

                gens.append(_gla_chunk(load, lambda state=state: state[0], put, masks, d == 1, consts[d]))
    _interleave(gens)
    for d, heads, state in finals:
        for st, h in zip(state[0], heads):
            st_ref[d, h] = st


def _gla_call(p, pt, w2p, b2):
    fqk, bqk = _chunk_maps(D_QK)
    fv, bv = _chunk_maps(D_V)
    fz, bz = _chunk_maps(TAIL_Z)
    fo, bo = _chunk_maps(0)
    blk = lambda m: pl.BlockSpec((1, SCAN_ROWS, 512), m)
    zblk = lambda m: pl.BlockSpec((1, SCAN_ROWS, 128), m)
    return pl.pallas_call(
        _gla_kernel,
        grid=(NB, T // SCAN_ROWS),
        in_specs=[blk(fqk), blk(fv), zblk(fz), blk(bqk), blk(bv), zblk(bz),
                  pl.BlockSpec((2, 128, 256), lambda b, s: (0, 0, 0)),
                  pl.BlockSpec((2, 1, 256), lambda b, s: (0, 0, 0))],
        out_specs=[blk(fo), blk(bo)],
        out_shape=[jax.ShapeDtypeStruct((NB, T, 512), F32)] * 2,
        scratch_shapes=[pltpu.VMEM((2, 4, 128, 128), F32)],
        compiler_params=_cparams(("parallel", "arbitrary"), 32),
        name="gla_scan",
    )(p, p, pt, p, p, pt, w2p, b2)


def _gated_norm(o, gate, g):
    parts = []
    for h in range(4):
        sl = slice(128 * h, 128 * h + 128)
        oh = o[:, sl]
        ms = jnp.mean(oh * oh, axis=-1, keepdims=True)
        parts.append((oh * lax.rsqrt(ms + EPS) * g * _silu(gate[:, sl])).astype(BF16))
    return parts


def _cdpost_kernel(cf_ref, cb_ref, cg_ref, df_ref, db_ref, dr_ref, gc_ref, gd_ref, o_ref):
    for h, part in enumerate(_gated_norm(cf_ref[0] + cb_ref[0], cg_ref[0], gc_ref[...])):
        o_ref[0, :, 128 * h:128 * h + 128] = part
    for h, part in enumerate(_gated_norm(df_ref[0] + db_ref[0], dr_ref[0], gd_ref[...])):
        o_ref[0, :, 512 + 128 * h:512 + 128 * h + 128] = part


def _cdpost_call(cf, cb, df, db, p, pt, gc, gd):
    tm = 256
    row = lambda col: pl.BlockSpec((1, tm, 512), lambda b, i: (b, i, col))
    vec = pl.BlockSpec((1, 128), lambda b, i: (0, 0))
    return pl.pallas_call(
        _cdpost_kernel,
        grid=(NB, T // tm),
        in_specs=[row(0), row(0), row(C_G), row(0), row(0), row(TAIL_R), vec, vec],
        out_specs=pl.BlockSpec((1, tm, 1024), lambda b, i: (b, i, 0)),
        out_shape=jax.ShapeDtypeStruct((NB, T, 1024), BF16),
        compiler_params=_cparams(("parallel", "parallel"), 32),
        name="scan_post",
    )(cf, cb, p, df, db, pt, gc, gd)


def _swiglu_split(h, w1_refs, w3_refs, w2_refs, lead):
    get = (lambda r: r[...]) if lead is None else (lambda r: r[lead])
    kc = D // W_SPLIT
    u = v = None
    for k in range(W_SPLIT):
        hk = h[:, k * kc:(k + 1) * kc]
        pu = jnp.dot(hk, get(w1_refs[k]).astype(BF16), preferred_element_type=F32)
        pv = jnp.dot(hk, get(w3_refs[k]).astype(BF16), preferred_element_type=F32)
        u = pu if u is None else u + pu
        v = pv if v is None else v + pv
    g = (_silu(u) * v).astype(BF16)
    return [jnp.dot(g, get(w2_refs[k]).astype(BF16), preferred_element_type=F32) for k in range(W_SPLIT)]


def _ffn_kernel(h_ref, *refs):
    w1_refs, w3_refs, w2_refs = refs[:W_SPLIT], refs[W_SPLIT:2 * W_SPLIT], refs[2 * W_SPLIT:3 * W_SPLIT]
    o_ref = refs[3 * W_SPLIT]

    @pl.when(pl.program_id(2) == 0)
    def _():
        o_ref[...] = jnp.zeros_like(o_ref)

    nc = D // W_SPLIT
    for k, part in enumerate(_swiglu_split(h_ref[0], w1_refs, w3_refs, w2_refs, None)):
        o_ref[0, :, k * nc:(k + 1) * nc] += part


def _ffn_call(h, w1, w3, w2):
    tm, tf = 1152, 512
    kc = D // W_SPLIT
    up = [pl.BlockSpec((kc, tf), functools.partial(lambda b, i, f, k: (k, f), k=k)) for k in range(W_SPLIT)]
    down = [pl.BlockSpec((tf, kc), functools.partial(lambda b, i, f, k: (f, k), k=k)) for k in range(W_SPLIT)]
    return pl.pallas_call(
        _ffn_kernel,
        grid=(NB, T // tm, FFN_DENSE // tf),
        in_specs=[pl.BlockSpec((1, tm, D), lambda b, i, f: (b, i, 0))] + up + up + down,
        out_specs=pl.BlockSpec((1, tm, D), lambda b, i, f: (b, i, 0), pipeline_mode=pl.Buffered(1)),
        out_shape=jax.ShapeDtypeStruct((NB, T, D), F32),
        compiler_params=_cparams(("parallel", "parallel", "arbitrary"), 56),
        name="dense_ffn",
    )(h, *([w1] * W_SPLIT), *([w3] * W_SPLIT), *([w2] * W_SPLIT))


MOE_TM = 1152
MOE_SUB = 192
MOE_CASES = (5, 6)
MOE_TF = 512
MOE_NF = FFN_EXPERT // MOE_TF
MOE_TILES = -(-(2 * NB * SEQ + N_EXPERTS * (MOE_TM - 1)) // MOE_TM)
MOE_ROWS = MOE_TILES * MOE_TM
MOE_FETCH_ROWS = MOE_TM // MOE_NF
MOE_FETCH_LEFT = MOE_TM - MOE_NF * MOE_FETCH_ROWS


def _route_kernel(x_ref, g_ref, sh_ref, sc_ref, wr_ref, h_ref, r_ref):
    x = x_ref[0]
    ms = jnp.mean(x * x, axis=-1, keepdims=True)
    h = x * lax.rsqrt(ms + EPS) * g_ref[...] * (1.0 + sc_ref[0]) + sh_ref[0]
    h_ref[0] = h
    wr = wr_ref[...]
    h_hi = h.astype(BF16)
    h_lo = (h - h_hi.astype(F32)).astype(BF16)
    w_hi = wr.astype(BF16)
    w_lo = (wr - w_hi.astype(F32)).astype(BF16)
    dot = lambda a, b: jnp.dot(a, b, preferred_element_type=F32)
    logits = dot(h_hi, w_hi) + (dot(h_lo, w_hi) + dot(h_hi, w_lo))
    lane = lax.broadcasted_iota(jnp.int32, logits.shape, 1)
    lanef = lane.astype(F32)
    lg = jnp.where(lane < N_EXPERTS, logits, -jnp.inf)
    m1 = jnp.max(lg, axis=-1, keepdims=True)
    i1 = jnp.min(jnp.where(lg == m1, lanef, 128.0), axis=-1, keepdims=True)
    lg2 = jnp.where(lanef == i1, -jnp.inf, lg)
    m2 = jnp.max(lg2, axis=-1, keepdims=True)
    i2 = jnp.min(jnp.where(lg2 == m2, lanef, 128.0), axis=-1, keepdims=True)
    e = jnp.exp(m2 - m1)
    w1 = 1.0 / (1.0 + e)
    w2 = e * w1
    r_ref[0] = jnp.where(lane == 0, i1, jnp.where(lane == 1, i2, jnp.where(lane == 2, w1,
                         jnp.where(lane == 3, w2, 0.0))))


def _route_call(xall, g, mods, wr):
    tm = 256
    return pl.pallas_call(
        _route_kernel,
        grid=(NB, SEQ // tm),
        in_specs=[
            pl.BlockSpec((1, tm, D), lambda b, i: (b, i + NCTX // tm, 0)),
            pl.BlockSpec((1, D), lambda b, i: (0, 0)),
            _mod_spec(3, False), _mod_spec(4, False),
            pl.BlockSpec((D, 128), lambda b, i: (0, 0)),
        ],
        out_specs=[pl.BlockSpec((1, tm, D), lambda b, i: (b, i, 0)),
                   pl.BlockSpec((1, tm, 128), lambda b, i: (b, i, 0))],
        out_shape=[jax.ShapeDtypeStruct((NB, SEQ, D), F32), jax.ShapeDtypeStruct((NB, SEQ, 128), F32)],
        compiler_params=_cparams(("parallel", "parallel"), 32),
        name="moe_route",
    )(xall, g, mods, mods, wr)


def _moe_kernel(te_ref, nl_ref, src_ref, h_ref, *refs):
    w1_refs, w3_refs, w2_refs = refs[:W_SPLIT], refs[W_SPLIT:2 * W_SPLIT], refs[2 * W_SPLIT:3 * W_SPLIT]
    o_ref, xg_ref, xb_ref, sem = refs[3 * W_SPLIT:]
    i = pl.program_id(0)
    f = pl.program_id(1)
    n_live = nl_ref[i]
    live = n_live > 0
    prev_live = nl_ref[jnp.maximum(i - 1, 0)] > 0
    nxt = jnp.minimum(i + 1, MOE_TILES - 1)

    def row_copy(tile, r):
        row = src_ref[tile * MOE_TM + r]
        return pltpu.make_async_copy(h_ref.at[pl.ds(row, 1), :], xg_ref.at[pl.ds(r, 1), :], sem)

    def wait_rows():
        pltpu.make_async_copy(h_ref.at[pl.ds(0, MOE_TM), :], xg_ref, sem).wait()

    @pl.when(f == 0)
    def _():
        o_ref[...] = jnp.zeros_like(o_ref)

    @pl.when((i == 0) & (f == 0) & live)
    def _():
        def body(r, c):
            row_copy(0, r).start()
            return c

        lax.fori_loop(0, MOE_TM, body, 0)
        wait_rows()

    @pl.when((i > 0) & (f == 0) & prev_live)
    def _():
        wait_rows()

    @pl.when((f == 0) & live)
    def _():
        xb_ref[...] = xg_ref[...].astype(BF16)

    def step(n):
        def body():
            @pl.when(f < MOE_FETCH_LEFT)
            def _():
                row_copy(nxt, MOE_NF * MOE_FETCH_ROWS + f).start()

            for r in range(MOE_FETCH_ROWS):
                row_copy(nxt, f * MOE_FETCH_ROWS + r).start()
            rows = slice(0, n * MOE_SUB)
            nc = D // W_SPLIT
            for k, part in enumerate(_swiglu_split(xb_ref[rows, :], w1_refs, w3_refs, w2_refs, 0)):
                o_ref[rows, k * nc:(k + 1) * nc] += part

        return body

    n_case = jnp.maximum(n_live, MOE_CASES[0])
    for n in MOE_CASES:
        pl.when(live & (n_case == n))(step(n))

    @pl.when((i == MOE_TILES - 1) & (f == MOE_NF - 1) & live)
    def _():
        wait_rows()


def _moe_call(tile_expert, tile_live, src_rows, h, w1, w3, w2):
    nf = MOE_NF

    def fidx(i, f, nl):
        return jnp.where(nl[i] > 0, f, nf - 1)

    kc = D // W_SPLIT
    up = [pl.BlockSpec((1, kc, MOE_TF), functools.partial(lambda i, f, te, nl, src, k: (te[i], k, fidx(i, f, nl)), k=k))
          for k in range(W_SPLIT)]
    down = [pl.BlockSpec((1, MOE_TF, kc), functools.partial(lambda i, f, te, nl, src, k: (te[i], fidx(i, f, nl), k), k=k))
            for k in range(W_SPLIT)]
    return pl.pallas_call(
        _moe_kernel,
        grid_spec=pltpu.PrefetchScalarGridSpec(
            num_scalar_prefetch=3,
            grid=(MOE_TILES, nf),
            in_specs=[pl.BlockSpec(memory_space=pl.ANY)] + up + up + down,
            out_specs=pl.BlockSpec((MOE_TM, D), lambda i, f, te, nl, src: (i, 0), pipeline_mode=pl.Buffered(1)),
            scratch_shapes=[pltpu.VMEM((MOE_TM, D), F32), pltpu.VMEM((MOE_TM, D), BF16),
                            pltpu.SemaphoreType.DMA(())],
        ),
        out_shape=jax.ShapeDtypeStruct((MOE_ROWS, D), F32),
        compiler_params=_cparams(("arbitrary", "arbitrary"), 58),
        name="moe_experts",
    )(tile_expert, tile_live, src_rows, h, *([w1] * W_SPLIT), *([w3] * W_SPLIT), *([w2] * W_SPLIT))


def _row_copy(src_hbm, row, dst_ref, r, sem):
    return pltpu.make_async_copy(src_hbm.at[pl.ds(row, 1), :], dst_ref.at[pl.ds(r, 1), :], sem)


def _combine_kernel(dest_ref, x_ref, r_ref, gm_ref, gf_ref, y_ref, o_ref, yb_ref, sem, *, tc, n_steps):
    step = pl.program_id(0) * (SEQ // tc) + pl.program_id(1)
    slot = lax.rem(step, 2)

    def start_rows(step_, slot_):
        def issue(r, c):
            tok = step_ * tc + r
            _row_copy(y_ref, dest_ref[2 * tok], yb_ref.at[slot_, 0], r, sem.at[slot_]).start()
            _row_copy(y_ref, dest_ref[2 * tok + 1], yb_ref.at[slot_, 1], r, sem.at[slot_]).start()
            return c

        lax.fori_loop(0, tc, issue, 0, unroll=8)

    @pl.when(step == 0)
    def _():
        start_rows(0, 0)

    @pl.when(step + 1 < n_steps)
    def _():
        start_rows(step + 1, 1 - slot)

    for j in range(2):
        pltpu.make_async_copy(y_ref.at[pl.ds(0, tc), :], yb_ref.at[slot, j], sem.at[slot]).wait()
    rt = r_ref[0]
    moe = rt[:, 2:3] * yb_ref[slot, 0] + rt[:, 3:4] * yb_ref[slot, 1]
    x = x_ref[0] + gm_ref[0] * moe
    ms = jnp.mean(x * x, axis=-1, keepdims=True)
    o_ref[0] = x * lax.rsqrt(ms + EPS) * gf_ref[...]


def _combine_call(dest, xall, route, mods, gfinal, y):
    tc = 256
    return pl.pallas_call(
        functools.partial(_combine_kernel, tc=tc, n_steps=NB * SEQ // tc),
        grid_spec=pltpu.PrefetchScalarGridSpec(
            num_scalar_prefetch=1,
            grid=(NB, SEQ // tc),
            in_specs=[
                pl.BlockSpec((1, tc, D), lambda b, i, d: (b, i + NCTX // tc, 0)),
                pl.BlockSpec((1, tc, 128), lambda b, i, d: (b, i, 0)),
                pl.BlockSpec((1, 1, D), lambda b, i, d: (b * 6 + 5, 0, 0)),
                pl.BlockSpec((1, D), lambda b, i, d: (0, 0)),
                pl.BlockSpec(memory_space=pl.ANY),
            ],
            out_specs=pl.BlockSpec((1, tc, D), lambda b, i, d: (b, i, 0)),
            scratch_shapes=[pltpu.VMEM((2, 2, tc, D), F32), pltpu.SemaphoreType.DMA((2,))],
        ),
        out_shape=jax.ShapeDtypeStruct((NB, SEQ, D), F32),
        compiler_params=_cparams(("arbitrary", "arbitrary"), 32),
        name="moe_combine",
    )(dest, xall, route, mods, gfinal, y)


def _moe_plan(route):
    n_pairs = 2 * NB * SEQ
    e = route[..., 0:2].astype(jnp.int32).reshape(n_pairs)
    onehot = (e[:, None] == jnp.arange(N_EXPERTS, dtype=jnp.int32)[None, :]).astype(jnp.int32)
    csum = jnp.cumsum(onehot, axis=0)
    rank = jnp.sum(onehot * csum, axis=1) - 1
    counts = csum[-1]
    padded = ((counts + MOE_TM - 1) // MOE_TM) * MOE_TM
    ends = jnp.cumsum(padded)
    starts = ends - padded
    dest = (starts[e] + rank).astype(jnp.int32)
    src_rows = jnp.zeros((MOE_ROWS,), jnp.int32).at[dest].set(jnp.arange(n_pairs, dtype=jnp.int32) // 2, unique_indices=True)
    tile_start = jnp.arange(MOE_TILES, dtype=jnp.int32) * MOE_TM
    t_eff = jnp.minimum(tile_start, ends[-1] - MOE_TM)
    tile_expert = jnp.sum((ends[None, :] <= t_eff[:, None]).astype(jnp.int32), axis=1)
    rows_left = (starts + counts)[tile_expert] - tile_start
    tile_live = jnp.clip((rows_left + MOE_SUB - 1) // MOE_SUB, 0, MOE_TM // MOE_SUB).astype(jnp.int32)
    return dest, src_rows, tile_expert.astype(jnp.int32), tile_live


def _rope_tables():
    per_axis = DIFF_QK // 4
    t = np.arange(SEQ)
    inv = ROPE_BASE ** (-np.arange(per_axis, dtype=np.float32) / per_axis)
    ang = np.concatenate([(t // GRID_W).astype(np.float32)[:, None] * inv,
                          (t % GRID_W).astype(np.float32)[:, None] * inv], axis=-1).astype(np.float32)
    cos = np.concatenate([np.ones((NCTX, 2 * per_axis), np.float32), np.cos(ang)], axis=0)
    sin = np.concatenate([np.zeros((NCTX, 2 * per_axis), np.float32), np.sin(ang)], axis=0)
    reps = 128 // (2 * per_axis)
    return jnp.asarray(np.tile(cos, (1, reps))), jnp.asarray(np.tile(sin, (1, reps)))


def _in_proj_tail(w_t):
    z = w_t[:, IN_MAIN:IN_MAIN + 2 * GLA_RANK]
    r = w_t[:, IN_MAIN + 2 * GLA_RANK:]
    pad = jnp.zeros((w_t.shape[0], TAIL_W - r.shape[1] - z.shape[1], D), w_t.dtype)
    return jnp.concatenate([r, z, pad], axis=1)


def kernel(x, c, ctx, c_ctx, ada_w, ada_b, norm_mix, norm_ffn, w_in, w_out, diff_lambda, diff_norm,
           na_rel_bias, hgrn_lower_bounds, hgrn_norm, gla_gate_w2, gla_gate_b, gla_norm,
           ffn_w1, ffn_w3, ffn_w2, moe_router, moe_w1, moe_w3, moe_w2, final_norm):
    lb_soft = jax.nn.softmax(hgrn_lower_bounds.astype(F32), axis=1)
    lower_bounds = jnp.clip(jnp.cumsum(lb_soft, axis=1) - lb_soft[:, :1], 0.0, 1.0 - 1e-6)
    cond8 = jnp.concatenate([c, c_ctx[None, :], jnp.zeros((8 - NB - 1, D), F32)], axis=0)
    mods_all = _ada_call(cond8, ada_w, ada_b).reshape(DEPTH, 8 * 6, 1, D)
    cos, sin = _rope_tables()
    xall = jnp.concatenate([ctx, x], axis=1)
    w_in_t = jnp.swapaxes(w_in, 1, 2)
    w_tail_t = _in_proj_tail(w_in_t)

    out = None
    pending = None
    for l in range(DEPTH):
        mods = mods_all[l]
        if pending is None:
            h = _norm_call(xall, norm_mix[l][None, :], mods, 0, 1)
        else:
            xall, h = _norm_call(xall, norm_mix[l][None, :], mods, 0, 1, resid=pending)
            pending = None
        p = _wsmm_call([h], w_in_t, l, 1664, IN_MAIN // 1664, w_t=True, vmem_mb=56, name="in_proj")
        pt = _wsmm_call([h], w_tail_t, l, TAIL_W, 1, w_t=True, vmem_mb=32, name="in_proj_tail")

        lambda_init = 0.8 - 0.6 * math.exp(-0.3 * l)
        lp = diff_lambda[l].astype(F32)
        lam = (jnp.exp(jnp.sum(lp[0] * lp[1])) - jnp.exp(jnp.sum(lp[2] * lp[3])) + lambda_init).reshape(1)
        a = _attn_a_call(p, lam, cos, sin, jnp.tile(diff_norm[l], 2)[None, :], 1.0 - lambda_init)
        n = _attn_na_call(p, _na_bias_table(na_rel_bias[l]))
        cf, cb = _hgrn_call(p, lower_bounds[:, l])
        w2p = jnp.zeros((2, 128, 4 * GLA_DK), F32)
        w2p = w2p.at[0, 0:GLA_RANK].set(gla_gate_w2[l, 0]).at[1, GLA_RANK:2 * GLA_RANK].set(gla_gate_w2[l, 1])
        df, db = _gla_call(p, pt, w2p, gla_gate_b[l][:, None, :])
        cd = _cdpost_call(cf, cb, df, db, p, pt, hgrn_norm[l][None, :], gla_norm[l][None, :])
        xall = _wsmm_call([a, n, cd], w_out, l, 1024, D // 1024, resid=(xall, mods, 2), name="out_proj")

        if l % 2 == 0:
            h2 = _norm_call(xall, norm_ffn[l][None, :], mods, 3, 4)
            y = _ffn_call(h2, ffn_w1[l // 2], ffn_w3[l // 2], ffn_w2[l // 2])
            pending = (y, mods, 5)
        else:
            assert l == DEPTH - 1
            wr = jnp.zeros((D, 128), F32).at[:, :N_EXPERTS].set(moe_router[l // 2])
            hr, route = _route_call(xall, norm_ffn[l][None, :], mods, wr)
            dest, src_rows, tile_expert, tile_live = _moe_plan(route)
            y = _moe_call(tile_expert, tile_live, src_rows, hr.reshape(NB * SEQ, D),
                          moe_w1[l // 2], moe_w3[l // 2], moe_w2[l // 2])
            out = _combine_call(dest, xall, route, mods, final_norm[None, :], y)
    return out
```

```python
import functools
import math

import numpy as np
import jax
import jax.numpy as jnp
from jax import lax
from jax.experimental import pallas as pl
from jax.experimental.pallas import tpu as pltpu

F32 = jnp.float32
BF16 = jnp.bfloat16

D = 2048
NB = 4
SEQ = 2048
NCTX = 256
T = NCTX + SEQ
DEPTH = 2
GRID_W = 64
ROPE_BASE = 10000.0
EPS = 1e-6
LOG2E = math.log2(math.e)

DIFF_QK = 32
NA_DIM = 64
NA_WIN_R = 8
NA_WIN_C = 16
NA_GROUP = 4
NA_KROWS = NA_GROUP + NA_WIN_R - 1
GLA_DK = 64
GLA_TAU = 16.0
GLA_RANK = 16
CH = 64
SUB = 8

FFN_DENSE = 5632
N_EXPERTS = 8
FFN_EXPERT = 7168

IN_MAIN = 6656
A_Q, A_K, A_V = 0, 4, 8
B_Q, B_K, B_V = 12, 16, 20
C_Q, C_I, C_FF, C_FB, C_G = 6, 7, 8, 9, 10
D_QK, D_V = 11, 12
TAIL_W = 640
TAIL_R, TAIL_Z = 0, 4

V7X_VMEM_BYTES = 64 * 1024 * 1024
W_SPLIT = 1


def _cparams(sem, vmem_mb):
    assert vmem_mb * 1024 * 1024 < V7X_VMEM_BYTES
    return pltpu.CompilerParams(dimension_semantics=sem, vmem_limit_bytes=vmem_mb * 1024 * 1024)


def _sigmoid(x):
    return 1.0 / (1.0 + jnp.exp(-x))


def _silu(x):
    return x * _sigmoid(x)


def _log_sigmoid(x):
    return jnp.minimum(x, 0.0) - jnp.log(1.0 + jnp.exp(-jnp.abs(x)))


def _nt(a, b):
    return lax.dot_general(a, b, (((1,), (1,)), ((), ())), preferred_element_type=F32)


def _tn(a, b):
    return lax.dot_general(a, b, (((0,), (0,)), ((), ())), preferred_element_type=F32)


def _interleave(gens):
    live = list(gens)
    while live:
        still = []
        for gen in live:
            try:
                next(gen)
                still.append(gen)
            except StopIteration:
                pass
        live = still


def _ada_kernel(c_ref, *refs):
    w_refs, b_ref, o_ref = refs[:W_SPLIT], refs[W_SPLIT], refs[W_SPLIT + 1]
    s = _silu(c_ref[...]).astype(BF16)
    kc = D // W_SPLIT
    acc = b_ref[0]
    for k in range(W_SPLIT):
        acc = acc + jnp.dot(s[:, k * kc:(k + 1) * kc], w_refs[k][0].astype(BF16), preferred_element_type=F32)
    o_ref[0] = acc


def _ada_call(cond8, ada_w, ada_b):
    tn = 1536
    n = 6 * D
    kc = D // W_SPLIT
    w_specs = [pl.BlockSpec((1, kc, tn), functools.partial(lambda l, j, k: (l, k, j), k=k)) for k in range(W_SPLIT)]
    return pl.pallas_call(
        _ada_kernel,
        grid=(DEPTH, n // tn),
        in_specs=[pl.BlockSpec((8, D), lambda l, j: (0, 0))] + w_specs
        + [pl.BlockSpec((1, 1, tn), lambda l, j: (l, 0, j))],
        out_specs=pl.BlockSpec((1, 8, tn), lambda l, j: (l, 0, j)),
        out_shape=jax.ShapeDtypeStruct((DEPTH, 8, n), F32),
        compiler_params=_cparams(("parallel", "parallel"), 40),
        name="ada_mod",
    )(cond8, *([ada_w] * W_SPLIT), ada_b.reshape(DEPTH, 1, n))


def _mod_spec(k, ctx):
    if ctx:
        return pl.BlockSpec((1, 1, D), lambda b, i: (NB * 6 + k, 0, 0))
    return pl.BlockSpec((1, 1, D), lambda b, i: (b * 6 + k, 0, 0))


def _row_is_ctx(row0, tm):
    return (row0 + lax.broadcasted_iota(jnp.int32, (tm, 1), 0)) < NCTX


def _modnorm(x, g, is_c, shl, scl, shc, scc):
    ms = jnp.mean(x * x, axis=-1, keepdims=True)
    y = x * lax.rsqrt(ms + EPS) * g
    sc = jnp.where(is_c, scc, scl)
    sh = jnp.where(is_c, shc, shl)
    return y * (1.0 + sc) + sh


def _norm_kernel(*refs, tm, resid):
    if resid:
        x_ref, y_ref, gl_ref, gx_ref, g_ref, shl_ref, scl_ref, shc_ref, scc_ref, xo_ref, h_ref = refs
    else:
        x_ref, g_ref, shl_ref, scl_ref, shc_ref, scc_ref, h_ref = refs
    is_c = _row_is_ctx(pl.program_id(1) * tm, tm)
    x = x_ref[0]
    if resid:
        x = x + jnp.where(is_c, gx_ref[0], gl_ref[0]) * y_ref[0]
        xo_ref[0] = x
    h_ref[0] = _modnorm(x, g_ref[...], is_c, shl_ref[0], scl_ref[0], shc_ref[0], scc_ref[0]).astype(BF16)


def _norm_call(xall, g, mods, k_shift, k_scale, resid=None):
    tm = 384
    row = pl.BlockSpec((1, tm, D), lambda b, i: (b, i, 0))
    in_specs, args = [row], [xall]
    if resid is not None:
        y, mods_prev, k_gate = resid
        in_specs += [row, _mod_spec(k_gate, False), _mod_spec(k_gate, True)]
        args += [y, mods_prev, mods_prev]
    in_specs += [pl.BlockSpec((1, D), lambda b, i: (0, 0)),
                 _mod_spec(k_shift, False), _mod_spec(k_scale, False),
                 _mod_spec(k_shift, True), _mod_spec(k_scale, True)]
    args += [g, mods, mods, mods, mods]
    h_shape = jax.ShapeDtypeStruct((NB, T, D), BF16)
    out_shape, out_specs = h_shape, row
    if resid is not None:
        out_shape, out_specs = [jax.ShapeDtypeStruct((NB, T, D), F32), h_shape], [row, row]
    return pl.pallas_call(
        functools.partial(_norm_kernel, tm=tm, resid=resid is not None),
        grid=(NB, T // tm),
        in_specs=in_specs, out_specs=out_specs, out_shape=out_shape,
        compiler_params=_cparams(("parallel", "parallel"), 32),
        name="mod_norm",
    )(*args)


def _wsmm_kernel(*refs, ksizes, tm, epilogue, w_t):
    n_in = len(ksizes)
    ins, w_ref = refs[:n_in], refs[n_in]
    o_ref, wb_ref = refs[-2], refs[-1]

    @pl.when((pl.program_id(1) == 0) & (pl.program_id(2) == 0))
    def _():
        wb_ref[...] = w_ref[0].astype(BF16)

    acc, k0 = None, 0
    for r, ks in zip(ins, ksizes):
        if w_t:
            part = _nt(r[0], wb_ref[:, k0:k0 + ks])
        else:
            part = jnp.dot(r[0], wb_ref[k0:k0 + ks, :], preferred_element_type=F32)
        acc = part if acc is None else acc + part
        k0 += ks
    if epilogue:
        x_ref, gl_ref, gx_ref = refs[n_in + 1:n_in + 4]
        gate = jnp.where(_row_is_ctx(pl.program_id(2) * tm, tm), gx_ref[0], gl_ref[0])
        acc = x_ref[0] + gate * acc
    o_ref[0] = acc


def _wsmm_call(acts, w, layer, tn, n_j, resid=None, w_t=False, vmem_mb=48, name="matmul"):
    tm = 768
    ksizes = tuple(a.shape[-1] for a in acts)
    kdim = sum(ksizes)
    in_specs = [pl.BlockSpec((1, tm, ks), lambda j, b, i: (b, i, 0)) for ks in ksizes]
    if w_t:
        in_specs.append(pl.BlockSpec((1, tn, kdim), lambda j, b, i: (layer, j, 0)))
    else:
        in_specs.append(pl.BlockSpec((1, kdim, tn), lambda j, b, i: (layer, 0, j)))
    args = list(acts) + [w]
    if resid is not None:
        x, mods, k_gate = resid
        in_specs += [pl.BlockSpec((1, tm, tn), lambda j, b, i: (b, i, j)),
                     pl.BlockSpec((1, 1, tn), lambda j, b, i: (b * 6 + k_gate, 0, j)),
                     pl.BlockSpec((1, 1, tn), lambda j, b, i: (NB * 6 + k_gate, 0, j))]
        args += [x, mods, mods]
    return pl.pallas_call(
        functools.partial(_wsmm_kernel, ksizes=ksizes, tm=tm, epilogue=resid is not None, w_t=w_t),
        grid=(n_j, NB, T // tm),
        in_specs=in_specs,
        out_specs=pl.BlockSpec((1, tm, tn), lambda j, b, i: (b, i, j)),
        out_shape=jax.ShapeDtypeStruct((NB, T, n_j * tn), F32),
        scratch_shapes=[pltpu.VMEM((tn, kdim) if w_t else (kdim, tn), BF16)],
        compiler_params=_cparams(("arbitrary", "arbitrary", "arbitrary"), vmem_mb),
        name=name,
    )(*args)


def _rope(x, cos, sin):
    lane = lax.broadcasted_iota(jnp.int32, x.shape, 1)
    first = (lane & (DIFF_QK - 1)) < (DIFF_QK // 2)
    rot = jnp.where(first, -pltpu.roll(x, 128 - DIFF_QK // 2, 1), pltpu.roll(x, DIFF_QK // 2, 1))
    return x * cos + rot * sin


def _attn_a_kernel(lam_ref, q_ref, k_ref, v_ref, cos_ref, sin_ref, g_ref, o_ref, ks_ref, va_ref, *, post_scale):
    t = pl.program_id(2)
    lane = lax.broadcasted_iota(jnp.int32, (1, 128), 1)
    ones_lane = (64, 0)

    @pl.when(t == 0)
    def _():
        ks_ref[...] = _rope(k_ref[0], cos_ref[...], sin_ref[...]).astype(BF16)
        v = v_ref[0]
        va_ref[0] = jnp.where(lane < 64, v, jnp.where(lane == ones_lane[0], 1.0, 0.0)).astype(BF16)
        va_ref[1] = jnp.where(lane >= 64, v, jnp.where(lane == ones_lane[1], 1.0, 0.0)).astype(BF16)

    lam = lam_ref[0]

    def attend(q0, nq, nk):
        rows = pl.ds(q0, nq)
        q = _rope(q_ref[0, rows, :], cos_ref[rows, :], sin_ref[rows, :]) * (DIFF_QK ** -0.5 * LOG2E)
        kk = ks_ref[0:nk, :]
        terms = {}

        def softmax_pv(hh, m):
            lo = 64 * hh + DIFF_QK * m
            qm = jnp.where((lane >= lo) & (lane < lo + DIFF_QK), q, 0.0).astype(BF16)
            s = _nt(qm, kk)
            yield
            mx = jnp.max(s, axis=-1, keepdims=True)
            yield
            e = jnp.exp2((s - mx).astype(BF16))
            yield
            num = jnp.dot(e, va_ref[hh, 0:nk, :], preferred_element_type=F32)
            yield
            den = jnp.sum(jnp.where(lane == ones_lane[hh], num, 0.0), axis=-1, keepdims=True)
            terms[hh, m] = num * ((1.0 if m == 0 else lam) / den)

        _interleave([softmax_pv(hh, m) for hh in range(2) for m in range(2)])
        outs = [terms[hh, 0] - terms[hh, 1] for hh in range(2)]
        o = jnp.where(lane < 64, outs[0], outs[1])
        sq = o * o
        s0 = jnp.sum(jnp.where(lane < 64, sq, 0.0), axis=-1, keepdims=True)
        s1 = jnp.sum(jnp.where(lane >= 64, sq, 0.0), axis=-1, keepdims=True)
        ms = jnp.where(lane < 64, s0, s1) * (1.0 / 64.0)
        o_ref[0, rows, :] = (o * lax.rsqrt(ms + EPS) * (g_ref[...] * post_scale)).astype(o_ref.dtype)

    @pl.when(t == 0)
    def _():
        attend(0, NCTX, NCTX)

    @pl.when(t > 0)
    def _():
        attend(pl.multiple_of(NCTX + (t - 1) * ATTN_TQ, ATTN_TQ // 2), ATTN_TQ, T)


ATTN_TQ = 512


def _attn_a_call(p, lam, cos, sin, g2, post_scale):
    full = lambda col: pl.BlockSpec((1, T, 128), lambda b, h, t: (b, 0, col + h))
    tab = pl.BlockSpec((T, 128), lambda b, h, t: (0, 0))
    return pl.pallas_call(
        functools.partial(_attn_a_kernel, post_scale=post_scale),
        grid=(NB, 4, 1 + SEQ // ATTN_TQ),
        in_specs=[pl.BlockSpec(memory_space=pltpu.SMEM), full(A_Q), full(A_K), full(A_V), tab, tab,
                  pl.BlockSpec((1, 128), lambda b, h, t: (0, 0))],
        out_specs=pl.BlockSpec((1, T, 128), lambda b, h, t: (b, 0, h)),
        out_shape=jax.ShapeDtypeStruct((NB, T, 512), BF16),
        scratch_shapes=[pltpu.VMEM((T, 128), BF16), pltpu.VMEM((2, T, 128), BF16)],
        compiler_params=_cparams(("parallel", "parallel", "arbitrary"), 48),
        name="diff_attn",
    )(lam, p, p, p, cos, sin, g2)


def _attn_na_kernel(q_ref, k_ref, v_ref, bias_ref, o_ref):
    g = pl.program_id(1)
    lane = lax.broadcasted_iota(jnp.int32, (1, 128), 1)

    def run(chain):
        results = {}
        gens = []
        for p in range(4):
            sl = slice(128 * p, 128 * p + 128)
            q = q_ref[0, :, sl] * (NA_DIM ** -0.5 * LOG2E)
            for hh in range(2):
                qm = jnp.where((lane >= 64 * hh) & (lane < 64 * hh + 64), q, 0.0).astype(BF16)
                gens.append(chain(2 * p + hh, qm, sl, results))
        _interleave(gens)
        for p in range(4):
            o_ref[0, :, 128 * p:128 * p + 128] = jnp.where(lane < 64, results[2 * p], results[2 * p + 1]
                                                           ).astype(o_ref.dtype)

    @pl.when(g == 0)
    def _():
        def ctx_head(h, qm, sl, results):
            s = _nt(qm, k_ref[0, 0:NCTX, sl].astype(BF16))
            yield
            e = jnp.exp2(s - jnp.max(s, axis=-1, keepdims=True))
            yield
            r = 1.0 / jnp.sum(e, axis=-1, keepdims=True)
            results[h] = jnp.dot(e.astype(BF16), v_ref[0, 0:NCTX, sl].astype(BF16), preferred_element_type=F32) * r

        run(ctx_head)

    @pl.when(g > 0)
    def _():
        u0 = jnp.clip(NA_GROUP * (g - 1) - NA_WIN_R // 2, 0, SEQ // GRID_W - NA_KROWS)
        win = pl.ds(pl.multiple_of(NCTX + GRID_W * u0, GRID_W), NA_KROWS * GRID_W)

        def lat_head(h, qm, sl, results):
            sw = _nt(qm, k_ref[0, win, sl].astype(BF16)) + bias_ref[h, 0]
            sc = _nt(qm, k_ref[0, 0:NCTX, sl].astype(BF16))
            yield
            mx = jnp.maximum(jnp.max(sw, axis=-1, keepdims=True), jnp.max(sc, axis=-1, keepdims=True))
            yield
            ew = jnp.exp2(sw - mx)
            ec = jnp.exp2(sc - mx)
            yield
            r = 1.0 / (jnp.sum(ew, axis=-1, keepdims=True) + jnp.sum(ec, axis=-1, keepdims=True))
            o = (jnp.dot(ew.astype(BF16), v_ref[0, win, sl].astype(BF16), preferred_element_type=F32)
                 + jnp.dot(ec.astype(BF16), v_ref[0, 0:NCTX, sl].astype(BF16), preferred_element_type=F32))
            yield
            results[h] = o * r

        run(lat_head)


def _attn_na_call(p, bias):
    tq = NA_GROUP * GRID_W
    nk = NA_KROWS * GRID_W
    n_grp = SEQ // tq

    def bias_map(b, g):
        grp = jnp.maximum(g - 1, 0)
        return (0, jnp.where(grp == 0, 0, jnp.where(grp == n_grp - 1, 2, 1)), 0, 0)

    return pl.pallas_call(
        _attn_na_kernel,
        grid=(NB, T // tq),
        in_specs=[
            pl.BlockSpec((1, tq, 512), lambda b, g: (b, g, B_Q // 4)),
            pl.BlockSpec((1, T, 512), lambda b, g: (b, 0, B_K // 4)),
            pl.BlockSpec((1, T, 512), lambda b, g: (b, 0, B_V // 4)),
            pl.BlockSpec((8, 1, tq, nk), bias_map),
        ],
        out_specs=pl.BlockSpec((1, tq, 512), lambda b, g: (b, g, 0)),
        out_shape=jax.ShapeDtypeStruct((NB, T, 512), BF16),
        compiler_params=_cparams(("parallel", "arbitrary"), 48),
        name="nbr_attn",
    )(p, p, p, bias)


def _na_bias_table(rel_bias):
    rows = SEQ // GRID_W
    n_grp = rows // NA_GROUP
    cq = np.arange(GRID_W)
    col_start = np.clip(cq - NA_WIN_C // 2, 0, GRID_W - NA_WIN_C)
    col_ok = (cq[None, :] >= col_start[:, None]) & (cq[None, :] < col_start[:, None] + NA_WIN_C)
    dc = np.clip(cq[None, :] - cq[:, None], -(NA_WIN_C - 1), NA_WIN_C - 1) + (NA_WIN_C - 1)
    gi = np.arange(n_grp)[:, None, None]
    qr = np.arange(NA_GROUP)[None, :, None]
    ku = np.arange(NA_KROWS)[None, None, :]
    r = NA_GROUP * gi + qr
    u = np.clip(NA_GROUP * gi - NA_WIN_R // 2, 0, rows - NA_KROWS) + ku
    row_start = np.clip(r - NA_WIN_R // 2, 0, rows - NA_WIN_R)
    row_ok = (u >= row_start) & (u < row_start + NA_WIN_R)
    dr = np.where(row_ok, u - r + (NA_WIN_R - 1), -1)
    for g in range(2, n_grp - 1):
        assert np.array_equal(dr[g], dr[1])
    assert np.all((dc == cq[None, :] - cq[:, None] + NA_WIN_C - 1)[col_ok])
    rbp = jnp.pad(rel_bias.astype(F32) * LOG2E, ((0, 0), (0, 0), (GRID_W, GRID_W)))
    rbt = jnp.stack([rbp[:, :, GRID_W + NA_WIN_C - 1 - q:2 * GRID_W + NA_WIN_C - 1 - q] for q in range(GRID_W)],
                    axis=2)
    rbm = jnp.where(col_ok[None, None], rbt, -jnp.inf)
    ninf = jnp.full((rel_bias.shape[0], GRID_W, GRID_W), -jnp.inf, F32)
    pats = []
    for g in (0, 1, n_grp - 1):
        qrows = []
        for a in range(NA_GROUP):
            blocks = [rbm[:, dr[g, a, b]] if dr[g, a, b] >= 0 else ninf for b in range(NA_KROWS)]
            qrows.append(jnp.concatenate(blocks, axis=-1))
        pats.append(jnp.concatenate(qrows, axis=-2))
    return jnp.stack(pats, axis=1)


def _scan_consts(rev):
    r2 = lax.broadcasted_iota(jnp.int32, (CH, CH), 0)
    c2 = lax.broadcasted_iota(jnp.int32, (CH, CH), 1)
    tri = jnp.where((c2 >= r2) if rev else (c2 <= r2), 1.0, 0.0).astype(BF16)
    levels = []
    n = CH // 2
    while n >= SUB:
        sh = (2 * n).bit_length() - 1
        same = (r2 >> sh) == (c2 >> sh)
        rin = r2 & (2 * n - 1)
        cin = c2 & (2 * n - 1)
        levels.append((same & (rin < n) & (cin >= n)) if rev else (same & (rin >= n) & (cin < n)))
        n //= 2
    sub_sh = SUB.bit_length() - 1
    dvalid = ((r2 >> sub_sh) == (c2 >> sub_sh)) & ((c2 >= r2) if rev else (c2 <= r2))
    hits = [(c2 & (SUB - 1)) == s for s in range(SUB)]
    return tri, levels, dvalid, hits


def _cumsum_rows(g, tri):
    g1 = g.astype(BF16)
    r1 = g - g1.astype(F32)
    g2 = r1.astype(BF16)
    g3 = (r1 - g2.astype(F32)).astype(BF16)
    dot = lambda a: jnp.dot(tri, a, preferred_element_type=F32)
    return dot(g1) + dot(g2) + dot(g3)


def _block_rows(a, first, period):
    return jnp.concatenate(
        [jnp.broadcast_to(a[r:r + 1, :], (period, 128)) for r in range(first, CH, period)], axis=0)


def _gla_chunk(load, get_state, put, masks, rev, consts):
    tri, level_valid, dvalid, hits = consts
    q, k, g, vs = load()
    nh = len(vs)
    b = _cumsum_rows(g * LOG2E, tri)
    b_end = b[0:1] if rev else b[CH - 1:CH]
    yield

    def headq(a, h):
        return a if masks[h] is None else jnp.where(masks[h], a, 0.0)

    atts = [jnp.zeros((CH, CH), F32) for _ in range(nh)]
    n = CH // 2
    for valid in level_valid:
        first = n if rev else n - 1
        ref = _block_rows(b, first, 2 * n) if 2 * n < CH else b[first:first + 1]
        qn = (q * jnp.exp2(b - ref)).astype(BF16)
        kn = (k * jnp.exp2(ref - b)).astype(BF16)
        for h in range(nh):
            a = _nt(headq(qn, h), kn)
            atts[h] = jnp.where(valid, a, atts[h])
        n //= 2
        yield

    diag = [jnp.zeros((CH, CH), F32) for _ in range(nh)]
    for s in range(SUB):
        e = q * _block_rows(k, s, SUB) * jnp.exp2(b - _block_rows(b, s, SUB))
        for h in range(nh):
            rs = jnp.sum(headq(e, h), axis=-1, keepdims=True)
            diag[h] = jnp.where(hits[s], rs, diag[h])
        yield

    qe = (q * jnp.exp2(b)).astype(BF16)
    kd = k * jnp.exp2(b_end - b)
    dec = jnp.exp2(b_end)
    yield
    sts = get_state()
    outs, new = [], []
    for h in range(nh):
        att = jnp.where(dvalid, diag[h], atts[h])
        vb = vs[h].astype(BF16)
        o = jnp.dot(att.astype(BF16), vb, preferred_element_type=F32) + _nt(qe, sts[h].astype(BF16))
        outs.append(o)
        new.append(sts[h] * dec + _tn(vb, headq(kd, h).astype(BF16)))
    put(outs, new)


SCAN_CHUNKS = 2
SCAN_ROWS = SCAN_CHUNKS * CH


def _chunk_maps(col_block):
    n_c = NCTX // SCAN_ROWS
    n_all = T // SCAN_ROWS

    def fwd(b, s):
        return (b, s, col_block)

    def bwd(b, s):
        return (b, jnp.where(s < n_c, n_c - 1 - s, n_all + n_c - 1 - s), col_block)

    return fwd, bwd


def _scan_order(rev):
    return tuple(reversed(range(SCAN_CHUNKS))) if rev else tuple(range(SCAN_CHUNKS))


def _hgrn_kernel(qf_ref, if_ref, ff_ref, qb_ref, ib_ref, fb_ref, lb_ref, of_ref, ob_ref, st_ref):
    @pl.when(pl.program_id(1) == 0)
    def _():
        st_ref[...] = jnp.zeros_like(st_ref)

    dirs = ((qf_ref, if_ref, ff_ref, of_ref), (qb_ref, ib_ref, fb_ref, ob_ref))
    consts = [_scan_consts(False), _scan_consts(True)]
    gens, finals = [], []
    for d, (q_ref, i_ref, f_ref, o_ref) in enumerate(dirs):
        for h in range(4):
            sl = slice(128 * h, 128 * h + 128)
            state = [[st_ref[d, h]]]
            finals.append((d, h, state))
            for j in _scan_order(d == 1):
                rows = slice(CH * j, CH * j + CH)

                def load(q_ref=q_ref, i_ref=i_ref, f_ref=f_ref, d=d, sl=sl, rows=rows):
                    lb = lb_ref[d:d + 1, sl]
                    q = _silu(q_ref[0, rows, sl])
                    f = f_ref[0, rows, sl]
                    e = jnp.exp(-jnp.abs(f))
                    inv = 1.0 / (1.0 + e)
                    k = (1.0 - lb) * jnp.where(f >= 0.0, e * inv, inv)
                    a = jnp.log(lb)
                    c = jnp.log(1.0 - lb) + (jnp.minimum(f, 0.0) - jnp.log(1.0 + e))
                    g = jnp.maximum(a, c) + jnp.log(1.0 + jnp.exp(-jnp.abs(a - c)))
                    return q, k, g, [i_ref[0, rows, sl]]

                def put(outs, new, o_ref=o_ref, sl=sl, rows=rows, state=state):
                    o_ref[0, rows, sl] = outs[0]
                    state[0] = new

                gens.append(_gla_chunk(load, lambda state=state: state[0], put, [None], d == 1, consts[d]))
    _interleave(gens)
    for d, h, state in finals:
        st_ref[d, h] = state[0][0]


def _hgrn_call(p, lb):
    fq, bq = _chunk_maps(C_Q)
    fi, bi = _chunk_maps(C_I)
    ff, _ = _chunk_maps(C_FF)
    _, bf = _chunk_maps(C_FB)
    fo, bo = _chunk_maps(0)
    blk = lambda m: pl.BlockSpec((1, SCAN_ROWS, 512), m)
    return pl.pallas_call(
        _hgrn_kernel,
        grid=(NB, T // SCAN_ROWS),
        in_specs=[blk(fq), blk(fi), blk(ff), blk(bq), blk(bi), blk(bf),
                  pl.BlockSpec((2, 512), lambda b, s: (0, 0))],
        out_specs=[blk(fo), blk(bo)],
        out_shape=[jax.ShapeDtypeStruct((NB, T, 512), F32)] * 2,
        scratch_shapes=[pltpu.VMEM((2, 4, 128, 128), F32)],
        compiler_params=_cparams(("parallel", "arbitrary"), 32),
        name="hgrn2_scan",
    )(p, p, p, p, p, p, lb)


def _gla_kernel(qkf_ref, vf_ref, zf_ref, qkb_ref, vb_ref, zb_ref, w2_ref, b2_ref, of_ref, ob_ref, st_ref):
    @pl.when(pl.program_id(1) == 0)
    def _():
        st_ref[...] = jnp.zeros_like(st_ref)

    lane = lax.broadcasted_iota(jnp.int32, (1, 128), 1)
    masks = [lane < GLA_DK, lane >= GLA_DK]
    dirs = ((qkf_ref, vf_ref, zf_ref, of_ref), (qkb_ref, vb_ref, zb_ref, ob_ref))
    consts = [_scan_consts(False), _scan_consts(True)]
    gens, finals = [], []
    for d, (qk_ref, v_ref, z_ref, o_ref) in enumerate(dirs):
        logit = jnp.dot(z_ref[0].astype(BF16), w2_ref[d].astype(BF16), preferred_element_type=F32) + b2_ref[d]
        g = _log_sigmoid(logit) * (1.0 / GLA_TAU)
        for grp in range(2):
            sl = slice(128 * grp, 128 * grp + 128)
            heads = (2 * grp, 2 * grp + 1)
            state = [[st_ref[d, h] for h in heads]]
            finals.append((d, heads, state))
            for j in _scan_order(d == 1):
                rows = slice(CH * j, CH * j + CH)

                def load(qk_ref=qk_ref, v_ref=v_ref, g=g, grp=grp, sl=sl, heads=heads, rows=rows):
                    q = qk_ref[0, rows, sl] * (GLA_DK ** -0.5)
                    k = qk_ref[0, rows, 256 + 128 * grp:256 + 128 * grp + 128]
                    return q, k, g[rows, sl], [v_ref[0, rows, 128 * h:128 * h + 128] for h in heads]

                def put(outs, new, o_ref=o_ref, heads=heads, rows=rows, state=state):
                    for o, h in zip(outs, heads):
                        o_ref[0, rows, 128 * h:128 * h + 128] = o
                    state[0] = new

                gens.append(_gla_chunk(load, lambda state=state: state[0], put, masks, d == 1, consts[d]))
    _interleave(gens)
    for d, heads, state in finals:
        for st, h in zip(state[0], heads):
            st_ref[d, h] = st


def _gla_call(p, pt, w2p, b2):
    fqk, bqk = _chunk_maps(D_QK)
    fv, bv = _chunk_maps(D_V)
    fz, bz = _chunk_maps(TAIL_Z)
    fo, bo = _chunk_maps(0)
    blk = lambda m: pl.BlockSpec((1, SCAN_ROWS, 512), m)
    zblk = lambda m: pl.BlockSpec((1, SCAN_ROWS, 128), m)
    return pl.pallas_call(
        _gla_kernel,
        grid=(NB, T // SCAN_ROWS),
        in_specs=[blk(fqk), blk(fv), zblk(fz), blk(bqk), blk(bv), zblk(bz),
                  pl.BlockSpec((2, 128, 256), lambda b, s: (0, 0, 0)),
                  pl.BlockSpec((2, 1, 256), lambda b, s: (0, 0, 0))],
        out_specs=[blk(fo), blk(bo)],
        out_shape=[jax.ShapeDtypeStruct((NB, T, 512), F32)] * 2,
        scratch_shapes=[pltpu.VMEM((2, 4, 128, 128), F32)],
        compiler_params=_cparams(("parallel", "arbitrary"), 32),
        name="gla_scan",
    )(p, p, pt, p, p, pt, w2p, b2)


def _gated_norm(o, gate, g):
    parts = []
    for h in range(4):
        sl = slice(128 * h, 128 * h + 128)
        oh = o[:, sl]
        ms = jnp.mean(oh * oh, axis=-1, keepdims=True)
        parts.append((oh * lax.rsqrt(ms + EPS) * g * _silu(gate[:, sl])).astype(BF16))
    return parts


def _cdpost_kernel(cf_ref, cb_ref, cg_ref, df_ref, db_ref, dr_ref, gc_ref, gd_ref, o_ref):
    for h, part in enumerate(_gated_norm(cf_ref[0] + cb_ref[0], cg_ref[0], gc_ref[...])):
        o_ref[0, :, 128 * h:128 * h + 128] = part
    for h, part in enumerate(_gated_norm(df_ref[0] + db_ref[0], dr_ref[0], gd_ref[...])):
        o_ref[0, :, 512 + 128 * h:512 + 128 * h + 128] = part


def _cdpost_call(cf, cb, df, db, p, pt, gc, gd):
    tm = 256
    row = lambda col: pl.BlockSpec((1, tm, 512), lambda b, i: (b, i, col))
    vec = pl.BlockSpec((1, 128), lambda b, i: (0, 0))
    return pl.pallas_call(
        _cdpost_kernel,
        grid=(NB, T // tm),
        in_specs=[row(0), row(0), row(C_G), row(0), row(0), row(TAIL_R), vec, vec],
        out_specs=pl.BlockSpec((1, tm, 1024), lambda b, i: (b, i, 0)),
        out_shape=jax.ShapeDtypeStruct((NB, T, 1024), BF16),
        compiler_params=_cparams(("parallel", "parallel"), 32),
        name="scan_post",
    )(cf, cb, p, df, db, pt, gc, gd)


def _swiglu_split(h, w1_refs, w3_refs, w2_refs, lead):
    get = (lambda r: r[...]) if lead is None else (lambda r: r[lead])
    kc = D // W_SPLIT
    u = v = None
    for k in range(W_SPLIT):
        hk = h[:, k * kc:(k + 1) * kc]
        pu = jnp.dot(hk, get(w1_refs[k]).astype(BF16), preferred_element_type=F32)
        pv = jnp.dot(hk, get(w3_refs[k]).astype(BF16), preferred_element_type=F32)
        u = pu if u is None else u + pu
        v = pv if v is None else v + pv
    g = (_silu(u) * v).astype(BF16)
    return [jnp.dot(g, get(w2_refs[k]).astype(BF16), preferred_element_type=F32) for k in range(W_SPLIT)]


def _ffn_kernel(h_ref, *refs):
    w1_refs, w3_refs, w2_refs = refs[:W_SPLIT], refs[W_SPLIT:2 * W_SPLIT], refs[2 * W_SPLIT:3 * W_SPLIT]
    o_ref = refs[3 * W_SPLIT]

    @pl.when(pl.program_id(2) == 0)
    def _():
        o_ref[...] = jnp.zeros_like(o_ref)

    nc = D // W_SPLIT
    for k, part in enumerate(_swiglu_split(h_ref[0], w1_refs, w3_refs, w2_refs, None)):
        o_ref[0, :, k * nc:(k + 1) * nc] += part


def _ffn_call(h, w1, w3, w2):
    tm, tf = 1152, 512
    kc = D // W_SPLIT
    up = [pl.BlockSpec((kc, tf), functools.partial(lambda b, i, f, k: (k, f), k=k)) for k in range(W_SPLIT)]
    down = [pl.BlockSpec((tf, kc), functools.partial(lambda b, i, f, k: (f, k), k=k)) for k in range(W_SPLIT)]
    return pl.pallas_call(
        _ffn_kernel,
        grid=(NB, T // tm, FFN_DENSE // tf),
        in_specs=[pl.BlockSpec((1, tm, D), lambda b, i, f: (b, i, 0))] + up + up + down,
        out_specs=pl.BlockSpec((1, tm, D), lambda b, i, f: (b, i, 0), pipeline_mode=pl.Buffered(1)),
        out_shape=jax.ShapeDtypeStruct((NB, T, D), F32),
        compiler_params=_cparams(("parallel", "parallel", "arbitrary"), 56),
        name="dense_ffn",
    )(h, *([w1] * W_SPLIT), *([w3] * W_SPLIT), *([w2] * W_SPLIT))


MOE_TM = 1152
MOE_SUB = 192
MOE_CASES = (5, 6)
MOE_TF = 512
MOE_NF = FFN_EXPERT // MOE_TF
MOE_TILES = -(-(2 * NB * SEQ + N_EXPERTS * (MOE_TM - 1)) // MOE_TM)
MOE_ROWS = MOE_TILES * MOE_TM
MOE_FETCH_ROWS = MOE_TM // MOE_NF
MOE_FETCH_LEFT = MOE_TM - MOE_NF * MOE_FETCH_ROWS


def _route_kernel(x_ref, g_ref, sh_ref, sc_ref, wr_ref, h_ref, r_ref):
    x = x_ref[0]
    ms = jnp.mean(x * x, axis=-1, keepdims=True)
    h = x * lax.rsqrt(ms + EPS) * g_ref[...] * (1.0 + sc_ref[0]) + sh_ref[0]
    h_ref[0] = h
    wr = wr_ref[...]
    h_hi = h.astype(BF16)
    h_lo = (h - h_hi.astype(F32)).astype(BF16)
    w_hi = wr.astype(BF16)
    w_lo = (wr - w_hi.astype(F32)).astype(BF16)
    dot = lambda a, b: jnp.dot(a, b, preferred_element_type=F32)
    logits = dot(h_hi, w_hi) + (dot(h_lo, w_hi) + dot(h_hi, w_lo))
    lane = lax.broadcasted_iota(jnp.int32, logits.shape, 1)
    lanef = lane.astype(F32)
    lg = jnp.where(lane < N_EXPERTS, logits, -jnp.inf)
    m1 = jnp.max(lg, axis=-1, keepdims=True)
    i1 = jnp.min(jnp.where(lg == m1, lanef, 128.0), axis=-1, keepdims=True)
    lg2 = jnp.where(lanef == i1, -jnp.inf, lg)
    m2 = jnp.max(lg2, axis=-1, keepdims=True)
    i2 = jnp.min(jnp.where(lg2 == m2, lanef, 128.0), axis=-1, keepdims=True)
    e = jnp.exp(m2 - m1)
    w1 = 1.0 / (1.0 + e)
    w2 = e * w1
    r_ref[0] = jnp.where(lane == 0, i1, jnp.where(lane == 1, i2, jnp.where(lane == 2, w1,
                         jnp.where(lane == 3, w2, 0.0))))


def _route_call(xall, g, mods, wr):
    tm = 256
    return pl.pallas_call(
        _route_kernel,
        grid=(NB, SEQ // tm),
        in_specs=[
            pl.BlockSpec((1, tm, D), lambda b, i: (b, i + NCTX // tm, 0)),
            pl.BlockSpec((1, D), lambda b, i: (0, 0)),
            _mod_spec(3, False), _mod_spec(4, False),
            pl.BlockSpec((D, 128), lambda b, i: (0, 0)),
        ],
        out_specs=[pl.BlockSpec((1, tm, D), lambda b, i: (b, i, 0)),
                   pl.BlockSpec((1, tm, 128), lambda b, i: (b, i, 0))],
        out_shape=[jax.ShapeDtypeStruct((NB, SEQ, D), F32), jax.ShapeDtypeStruct((NB, SEQ, 128), F32)],
        compiler_params=_cparams(("parallel", "parallel"), 32),
        name="moe_route",
    )(xall, g, mods, mods, wr)


def _moe_kernel(te_ref, nl_ref, src_ref, h_ref, *refs):
    w1_refs, w3_refs, w2_refs = refs[:W_SPLIT], refs[W_SPLIT:2 * W_SPLIT], refs[2 * W_SPLIT:3 * W_SPLIT]
    o_ref, xg_ref, xb_ref, sem = refs[3 * W_SPLIT:]
    i = pl.program_id(0)
    f = pl.program_id(1)
    n_live = nl_ref[i]
    live = n_live > 0
    prev_live = nl_ref[jnp.maximum(i - 1, 0)] > 0
    nxt = jnp.minimum(i + 1, MOE_TILES - 1)

    def row_copy(tile, r):
        row = src_ref[tile * MOE_TM + r]
        return pltpu.make_async_copy(h_ref.at[pl.ds(row, 1), :], xg_ref.at[pl.ds(r, 1), :], sem)

    def wait_rows():
        pltpu.make_async_copy(h_ref.at[pl.ds(0, MOE_TM), :], xg_ref, sem).wait()

    @pl.when(f == 0)
    def _():
        o_ref[...] = jnp.zeros_like(o_ref)

    @pl.when((i == 0) & (f == 0) & live)
    def _():
        def body(r, c):
            row_copy(0, r).start()
            return c

        lax.fori_loop(0, MOE_TM, body, 0)
        wait_rows()

    @pl.when((i > 0) & (f == 0) & prev_live)
    def _():
        wait_rows()

    @pl.when((f == 0) & live)
    def _():
        xb_ref[...] = xg_ref[...].astype(BF16)

    def step(n):
        def body():
            @pl.when(f < MOE_FETCH_LEFT)
            def _():
                row_copy(nxt, MOE_NF * MOE_FETCH_ROWS + f).start()

            for r in range(MOE_FETCH_ROWS):
                row_copy(nxt, f * MOE_FETCH_ROWS + r).start()
            rows = slice(0, n * MOE_SUB)
            nc = D // W_SPLIT
            for k, part in enumerate(_swiglu_split(xb_ref[rows, :], w1_refs, w3_refs, w2_refs, 0)):
                o_ref[rows, k * nc:(k + 1) * nc] += part

        return body

    n_case = jnp.maximum(n_live, MOE_CASES[0])
    for n in MOE_CASES:
        pl.when(live & (n_case == n))(step(n))

    @pl.when((i == MOE_TILES - 1) & (f == MOE_NF - 1) & live)
    def _():
        wait_rows()


def _moe_call(tile_expert, tile_live, src_rows, h, w1, w3, w2):
    nf = MOE_NF

    def fidx(i, f, nl):
        return jnp.where(nl[i] > 0, f, nf - 1)

    kc = D // W_SPLIT
    up = [pl.BlockSpec((1, kc, MOE_TF), functools.partial(lambda i, f, te, nl, src, k: (te[i], k, fidx(i, f, nl)), k=k))
          for k in range(W_SPLIT)]
    down = [pl.BlockSpec((1, MOE_TF, kc), functools.partial(lambda i, f, te, nl, src, k: (te[i], fidx(i, f, nl), k), k=k))
            for k in range(W_SPLIT)]
    return pl.pallas_call(
        _moe_kernel,
        grid_spec=pltpu.PrefetchScalarGridSpec(
            num_scalar_prefetch=3,
            grid=(MOE_TILES, nf),
            in_specs=[pl.BlockSpec(memory_space=pl.ANY)] + up + up + down,
            out_specs=pl.BlockSpec((MOE_TM, D), lambda i, f, te, nl, src: (i, 0), pipeline_mode=pl.Buffered(1)),
            scratch_shapes=[pltpu.VMEM((MOE_TM, D), F32), pltpu.VMEM((MOE_TM, D), BF16),
                            pltpu.SemaphoreType.DMA(())],
        ),
        out_shape=jax.ShapeDtypeStruct((MOE_ROWS, D), F32),
        compiler_params=_cparams(("arbitrary", "arbitrary"), 58),
        name="moe_experts",
    )(tile_expert, tile_live, src_rows, h, *([w1] * W_SPLIT), *([w3] * W_SPLIT), *([w2] * W_SPLIT))


def _row_copy(src_hbm, row, dst_ref, r, sem):
    return pltpu.make_async_copy(src_hbm.at[pl.ds(row, 1), :], dst_ref.at[pl.ds(r, 1), :], sem)


def _combine_kernel(dest_ref, x_ref, r_ref, gm_ref, gf_ref, y_ref, o_ref, yb_ref, sem, *, tc, n_steps):
    step = pl.program_id(0) * (SEQ // tc) + pl.program_id(1)
    slot = lax.rem(step, 2)

    def start_rows(step_, slot_):
        def issue(r, c):
            tok = step_ * tc + r
            _row_copy(y_ref, dest_ref[2 * tok], yb_ref.at[slot_, 0], r, sem.at[slot_]).start(priority=0)
            _row_copy(y_ref, dest_ref[2 * tok + 1], yb_ref.at[slot_, 1], r, sem.at[slot_]).start(priority=1)
            return c

        lax.fori_loop(0, tc, issue, 0, unroll=8)

    @pl.when(step == 0)
    def _():
        start_rows(0, 0)

    @pl.when(step + 1 < n_steps)
    def _():
        start_rows(step + 1, 1 - slot)

    for j in range(2):
        pltpu.make_async_copy(y_ref.at[pl.ds(0, tc), :], yb_ref.at[slot, j], sem.at[slot]).wait()
    rt = r_ref[0]
    moe = rt[:, 2:3] * yb_ref[slot, 0] + rt[:, 3:4] * yb_ref[slot, 1]
    x = x_ref[0] + gm_ref[0] * moe
    ms = jnp.mean(x * x, axis=-1, keepdims=True)
    o_ref[0] = x * lax.rsqrt(ms + EPS) * gf_ref[...]


def _combine_call(dest, xall, route, mods, gfinal, y):
    tc = 256
    return pl.pallas_call(
        functools.partial(_combine_kernel, tc=tc, n_steps=NB * SEQ // tc),
        grid_spec=pltpu.PrefetchScalarGridSpec(
            num_scalar_prefetch=1,
            grid=(NB, SEQ // tc),
            in_specs=[
                pl.BlockSpec((1, tc, D), lambda b, i, d: (b, i + NCTX // tc, 0)),
                pl.BlockSpec((1, tc, 128), lambda b, i, d: (b, i, 0)),
                pl.BlockSpec((1, 1, D), lambda b, i, d: (b * 6 + 5, 0, 0)),
                pl.BlockSpec((1, D), lambda b, i, d: (0, 0)),
                pl.BlockSpec(memory_space=pl.ANY),
            ],
            out_specs=pl.BlockSpec((1, tc, D), lambda b, i, d: (b, i, 0)),
            scratch_shapes=[pltpu.VMEM((2, 2, tc, D), F32), pltpu.SemaphoreType.DMA((2,))],
        ),
        out_shape=jax.ShapeDtypeStruct((NB, SEQ, D), F32),
        compiler_params=_cparams(("arbitrary", "arbitrary"), 32),
        name="moe_combine",
    )(dest, xall, route, mods, gfinal, y)


def _moe_plan(route):
    n_pairs = 2 * NB * SEQ
    e = route[..., 0:2].astype(jnp.int32).reshape(n_pairs)
    onehot = (e[:, None] == jnp.arange(N_EXPERTS, dtype=jnp.int32)[None, :]).astype(jnp.int32)
    csum = jnp.cumsum(onehot, axis=0)
    rank = jnp.sum(onehot * csum, axis=1) - 1
    counts = csum[-1]
    padded = ((counts + MOE_TM - 1) // MOE_TM) * MOE_TM
    ends = jnp.cumsum(padded)
    starts = ends - padded
    dest = (starts[e] + rank).astype(jnp.int32)
    src_rows = jnp.zeros((MOE_ROWS,), jnp.int32).at[dest].set(jnp.arange(n_pairs, dtype=jnp.int32) // 2, unique_indices=True)
    tile_start = jnp.arange(MOE_TILES, dtype=jnp.int32) * MOE_TM
    t_eff = jnp.minimum(tile_start, ends[-1] - MOE_TM)
    tile_expert = jnp.sum((ends[None, :] <= t_eff[:, None]).astype(jnp.int32), axis=1)
    rows_left = (starts + counts)[tile_expert] - tile_start
    tile_live = jnp.clip((rows_left + MOE_SUB - 1) // MOE_SUB, 0, MOE_TM // MOE_SUB).astype(jnp.int32)
    return dest, src_rows, tile_expert.astype(jnp.int32), tile_live


def _rope_tables():
    per_axis = DIFF_QK // 4
    t = np.arange(SEQ)
    inv = ROPE_BASE ** (-np.arange(per_axis, dtype=np.float32) / per_axis)
    ang = np.concatenate([(t // GRID_W).astype(np.float32)[:, None] * inv,
                          (t % GRID_W).astype(np.float32)[:, None] * inv], axis=-1).astype(np.float32)
    cos = np.concatenate([np.ones((NCTX, 2 * per_axis), np.float32), np.cos(ang)], axis=0)
    sin = np.concatenate([np.zeros((NCTX, 2 * per_axis), np.float32), np.sin(ang)], axis=0)
    reps = 128 // (2 * per_axis)
    return jnp.asarray(np.tile(cos, (1, reps))), jnp.asarray(np.tile(sin, (1, reps)))


def _in_proj_tail(w_t):
    z = w_t[:, IN_MAIN:IN_MAIN + 2 * GLA_RANK]
    r = w_t[:, IN_MAIN + 2 * GLA_RANK:]
    pad = jnp.zeros((w_t.shape[0], TAIL_W - r.shape[1] - z.shape[1], D), w_t.dtype)
    return jnp.concatenate([r, z, pad], axis=1)


def kernel(x, c, ctx, c_ctx, ada_w, ada_b, norm_mix, norm_ffn, w_in, w_out, diff_lambda, diff_norm,
           na_rel_bias, hgrn_lower_bounds, hgrn_norm, gla_gate_w2, gla_gate_b, gla_norm,
           ffn_w1, ffn_w3, ffn_w2, moe_router, moe_w1, moe_w3, moe_w2, final_norm):
    lb_soft = jax.nn.softmax(hgrn_lower_bounds.astype(F32), axis=1)
    lower_bounds = jnp.clip(jnp.cumsum(lb_soft, axis=1) - lb_soft[:, :1], 0.0, 1.0 - 1e-6)
    cond8 = jnp.concatenate([c, c_ctx[None, :], jnp.zeros((8 - NB - 1, D), F32)], axis=0)
    mods_all = _ada_call(cond8, ada_w, ada_b).reshape(DEPTH, 8 * 6, 1, D)
    cos, sin = _rope_tables()
    xall = jnp.concatenate([ctx, x], axis=1)
    w_in_t = jnp.swapaxes(w_in, 1, 2)
    w_tail_t = _in_proj_tail(w_in_t)

    out = None
    pending = None
    for l in range(DEPTH):
        mods = mods_all[l]
        if pending is None:
            h = _norm_call(xall, norm_mix[l][None, :], mods, 0, 1)
        else:
            xall, h = _norm_call(xall, norm_mix[l][None, :], mods, 0, 1, resid=pending)
            pending = None
        p = _wsmm_call([h], w_in_t, l, 1664, IN_MAIN // 1664, w_t=True, vmem_mb=56, name="in_proj")
        pt = _wsmm_call([h], w_tail_t, l, TAIL_W, 1, w_t=True, vmem_mb=32, name="in_proj_tail")

        lambda_init = 0.8 - 0.6 * math.exp(-0.3 * l)
        lp = diff_lambda[l].astype(F32)
        lam = (jnp.exp(jnp.sum(lp[0] * lp[1])) - jnp.exp(jnp.sum(lp[2] * lp[3])) + lambda_init).reshape(1)
        a = _attn_a_call(p, lam, cos, sin, jnp.tile(diff_norm[l], 2)[None, :], 1.0 - lambda_init)
        n = _attn_na_call(p, _na_bias_table(na_rel_bias[l]))
        cf, cb = _hgrn_call(p, lower_bounds[:, l])
        w2p = jnp.zeros((2, 128, 4 * GLA_DK), F32)
        w2p = w2p.at[0, 0:GLA_RANK].set(gla_gate_w2[l, 0]).at[1, GLA_RANK:2 * GLA_RANK].set(gla_gate_w2[l, 1])
        df, db = _gla_call(p, pt, w2p, gla_gate_b[l][:, None, :])
        cd = _cdpost_call(cf, cb, df, db, p, pt, hgrn_norm[l][None, :], gla_norm[l][None, :])
        xall = _wsmm_call([a, n, cd], w_out, l, 1024, D // 1024, resid=(xall, mods, 2), name="out_proj")

        if l % 2 == 0:
            h2 = _norm_call(xall, norm_ffn[l][None, :], mods, 3, 4)
            y = _ffn_call(h2, ffn_w1[l // 2], ffn_w3[l // 2], ffn_w2[l // 2])
            pending = (y, mods, 5)
        else:
            assert l == DEPTH - 1
            wr = jnp.zeros((D, 128), F32).at[:, :N_EXPERTS].set(moe_router[l // 2])
            hr, route = _route_call(xall, norm_ffn[l][None, :], mods, wr)
            dest, src_rows, tile_expert, tile_live = _moe_plan(route)
            y = _moe_call(tile_expert, tile_live, src_rows, hr.reshape(NB * SEQ, D),
                          moe_w1[l // 2], moe_w3[l // 2], moe_w2[l // 2])
            out = _combine_call(dest, xall, route, mods, final_norm[None, :], y)
    return out
```
